```python
import jax, jax.numpy as jnp
from jax import lax
import numpy as np

D_MODEL = 1024
BATCH = 2
SEQ = 8192
DEPTH = 4
DEC_BATCH = 32
DEC_SEQ = 1
PAST_LEN = 8192
PAGE_SIZE = 128

N_A = DEPTH // 2
N_B = DEPTH - N_A
CONV_W = 3
HEAD_DIM = 128
N_GROUPS = 3
WINDOWS = (128, 512, 2048)
DILATIONS = (1, 4, 16)
HEADS = 8
KV_HEADS = 2
Q_PER_KV = HEADS // KV_HEADS
ATTN_WIDTH = HEADS * HEAD_DIM
Q_WIDTH = N_GROUPS * HEADS * HEAD_DIM
KV_WIDTH = 2 * N_GROUPS * KV_HEADS * HEAD_DIM
ROPE_THETA = 10000.0
EPS = 1e-6
SCALE = HEAD_DIM ** -0.5

kernel_name = "yoco_shortconv_dilated_window_decoder"


def rms_norm(x, g):
    x32 = x.astype(jnp.float32)
    y = x32 * lax.rsqrt(jnp.mean(x32 * x32, axis=-1, keepdims=True) + EPS)
    return (y * g.astype(jnp.float32)).astype(x.dtype)


def rope(x, pos):
    half = HEAD_DIM // 2
    inv_freq = ROPE_THETA ** (-jnp.arange(half, dtype=jnp.float32) * 2.0 / HEAD_DIM)
    ang = pos.astype(jnp.float32)[:, None] * inv_freq[None, :]
    cos = jnp.cos(ang)[:, None, :]
    sin = jnp.sin(ang)[:, None, :]
    x32 = x.astype(jnp.float32)
    x1, x2 = x32[..., :half], x32[..., half:]
    return jnp.concatenate([x1 * cos - x2 * sin, x2 * cos + x1 * sin], axis=-1).astype(x.dtype)


def causal_short_conv(up, w, T):
    return sum(up[:, k:k + T] * w[k] for k in range(CONV_W))


def conv_layer(x, past, g_pre, w_in, w_conv, w_out, g_post):
    T = x.shape[1]
    h = rms_norm(x, g_pre)
    b_gate, c_gate, u_in, z = jnp.split(h @ w_in, 4, axis=-1)
    u = c_gate * u_in
    up = jnp.concatenate([past.astype(u.dtype), u], axis=1)
    y = b_gate * causal_short_conv(up, w_conv, T)
    out = (y * jax.nn.silu(z)) @ w_out
    return x + rms_norm(out, g_post), up[:, -(CONV_W - 1):]


def shared_kv(x, g, w_kv, pos):
    Bx, T = x.shape[:2]
    kv = (rms_norm(x, g) @ w_kv).reshape(Bx, T, 2, N_GROUPS * KV_HEADS, HEAD_DIM)
    k = rope(kv[:, :, 0], pos).reshape(Bx, T, N_GROUPS, KV_HEADS, HEAD_DIM)
    v = kv[:, :, 1].reshape(Bx, T, N_GROUPS, KV_HEADS, HEAD_DIM)
    return k, v


def b_queries(h, w_in, pos):
    Bx, T = h.shape[:2]
    proj = h @ w_in
    q = rope(proj[..., :Q_WIDTH].reshape(Bx, T, N_GROUPS * HEADS, HEAD_DIM), pos)
    return q.reshape(Bx, T, N_GROUPS, HEADS, HEAD_DIM), proj[..., Q_WIDTH:]


def band_attn_prompt(q, k, v, window, dilation):
    Bx, S = q.shape[:2]
    n = S // dilation
    w = window // dilation
    blk = w
    n_pad = -(-n // blk) * blk
    nb = n_pad // blk

    def strided(a):
        a = a.reshape(Bx, n, dilation, a.shape[2], HEAD_DIM).transpose(0, 2, 1, 3, 4)
        return jnp.pad(a, ((0, 0), (0, 0), (0, n_pad - n), (0, 0), (0, 0)))

    def band(a):
        a = jnp.pad(a, ((0, 0), (0, 0), (blk, 0), (0, 0), (0, 0)))
        a = a.reshape(Bx, dilation, nb + 1, blk, KV_HEADS, HEAD_DIM)
        return jnp.concatenate([a[:, :, :-1], a[:, :, 1:]], axis=3)

    qs = strided(q).reshape(Bx, dilation, nb, blk, KV_HEADS, Q_PER_KV, HEAD_DIM).astype(jnp.float32)
    ks = band(strided(k)).astype(jnp.float32)
    vs = band(strided(v)).astype(jnp.float32)
    s = jnp.einsum('brnqkgd,brnjkd->brnkgqj', qs, ks) * SCALE
    i_idx = jnp.arange(nb)[:, None, None] * blk + jnp.arange(blk)[None, :, None]
    j_idx = jnp.arange(nb)[:, None, None] * blk - blk + jnp.arange(2 * blk)[None, None, :]
    dist = i_idx - j_idx
    mask = (j_idx >= 0) & (dist >= 0) & (dist <= w)
    s = jnp.where(mask[:, None, None], s, -jnp.inf)
    lse = jax.nn.logsumexp(s, axis=-1)
    p = jnp.exp(s - lse[..., None])
    o = jnp.einsum('brnkgqj,brnjkd->brnqkgd', p, vs)
    o = o.reshape(Bx, dilation, n_pad, HEADS, HEAD_DIM)[:, :, :n]
    o = o.transpose(0, 2, 1, 3, 4).reshape(Bx, S, HEADS, HEAD_DIM)
    lse = lse.transpose(0, 1, 2, 5, 3, 4).reshape(Bx, dilation, n_pad, HEADS)[:, :, :n]
    lse = lse.transpose(0, 2, 1, 3).reshape(Bx, S, HEADS)
    return o, lse


def window_attn_sample(q, kc, vc, n_past, window, dilation):
    Bd, T = q.shape[:2]
    nk = window // dilation + 1
    idx = (n_past + jnp.arange(T))[:, None] - dilation * jnp.arange(nk)[None, :]
    valid = idx >= 0
    idx = jnp.maximum(idx, 0)
    kg = kc[:, idx].astype(jnp.float32)
    vg = vc[:, idx].astype(jnp.float32)
    qg = q.reshape(Bd, T, KV_HEADS, Q_PER_KV, HEAD_DIM).astype(jnp.float32)
    s = jnp.einsum('btkgd,btnkd->btkgn', qg, kg) * SCALE
    s = jnp.where(valid[None, :, None, None, :], s, -jnp.inf)
    lse = jax.nn.logsumexp(s, axis=-1)
    p = jnp.exp(s - lse[..., None])
    o = jnp.einsum('btkgn,btnkd->btkgd', p, vg).reshape(Bd, T, HEADS, HEAD_DIM)
    return o, lse.reshape(Bd, T, HEADS)


def merge_groups(res, z, w_out):
    outs = jnp.stack([r[0] for r in res], axis=0)
    lses = jnp.stack([r[1] for r in res], axis=0)
    alpha = jax.nn.softmax(lses, axis=0)
    o = jnp.sum(alpha[..., None] * outs, axis=0)
    Bx, T = z.shape[:2]
    o = o.reshape(Bx, T, ATTN_WIDTH).astype(z.dtype)
    return (o * jax.nn.silu(z)) @ w_out


def setup_inputs(seed: int = 0) -> dict:
    key = jax.random.key(seed)
    ks = jax.random.split(key, 20)
    D = D_MODEL

    def nrm(k, shape, scale=1.0):
        return jax.random.normal(k, shape, jnp.float32) * scale

    return {
        "x_prompt": nrm(ks[0], (BATCH, SEQ, D)),
        "x_sample": nrm(ks[1], (DEC_BATCH, DEC_SEQ, D)),
        "state_conv": nrm(ks[2], (N_A, DEC_BATCH, CONV_W - 1, D)),
        "cache_kv_g0": nrm(ks[3], (DEC_BATCH, min(WINDOWS[0], PAST_LEN), 2, KV_HEADS, HEAD_DIM)),
        "cache_kv_g1": nrm(ks[4], (DEC_BATCH, min(WINDOWS[1], PAST_LEN), 2, KV_HEADS, HEAD_DIM)),
        "cache_kv_g2": nrm(ks[5], (DEC_BATCH, min(WINDOWS[2], PAST_LEN), 2, KV_HEADS, HEAD_DIM)),
        "a_norm_pre": 1.0 + nrm(ks[6], (N_A, D), 0.05),
        "a_w_in": nrm(ks[7], (N_A, D, 4 * D), D ** -0.5),
        "a_conv": nrm(ks[8], (N_A, CONV_W, D), CONV_W ** -0.5),
        "a_w_out": nrm(ks[9], (N_A, D, D), D ** -0.5),
        "a_norm_post": 1.0 + nrm(ks[10], (N_A, D), 0.05),
        "kv_norm": 1.0 + nrm(ks[11], (D,), 0.05),
        "w_kv": nrm(ks[12], (D, KV_WIDTH), D ** -0.5),
        "b_norm_pre": 1.0 + nrm(ks[13], (N_B, D), 0.05),
        "b_w_in": nrm(ks[14], (N_B, D, Q_WIDTH + ATTN_WIDTH), D ** -0.5),
        "b_w_out": nrm(ks[15], (N_B, ATTN_WIDTH, D), ATTN_WIDTH ** -0.5),
        "b_norm_post": 1.0 + nrm(ks[16], (N_B, D), 0.05),
    }


def reference(x_prompt, x_sample, state_conv, cache_kv_g0, cache_kv_g1, cache_kv_g2,
              a_norm_pre, a_w_in, a_conv, a_w_out, a_norm_post,
              kv_norm, w_kv, b_norm_pre, b_w_in, b_w_out, b_norm_post):
    seq = x_prompt.shape[1]
    dec_seq = x_sample.shape[1]
    pos_p = jnp.arange(seq)
    pos_s = PAST_LEN + jnp.arange(dec_seq)
    caches = (cache_kv_g0, cache_kv_g1, cache_kv_g2)
    hp, hs = x_prompt, x_sample
    conv_p, conv_s = [], []
    for layer in range(DEPTH):
        if layer < N_A:
            zero_past = jnp.zeros((hp.shape[0], CONV_W - 1, D_MODEL), hp.dtype)
            hp, st_p = conv_layer(hp, zero_past, a_norm_pre[layer], a_w_in[layer],
                                  a_conv[layer], a_w_out[layer], a_norm_post[layer])
            hs, st_s = conv_layer(hs, state_conv[layer], a_norm_pre[layer], a_w_in[layer],
                                  a_conv[layer], a_w_out[layer], a_norm_post[layer])
            conv_p.append(st_p)
            conv_s.append(st_s)
            continue
        if layer == N_A:
            k_p, v_p = shared_kv(hp, kv_norm, w_kv, pos_p)
            k_s, v_s = shared_kv(hs, kv_norm, w_kv, pos_s)
            kc_s = [jnp.concatenate([caches[g][:, :, 0], k_s[:, :, g]], axis=1) for g in range(N_GROUPS)]
            vc_s = [jnp.concatenate([caches[g][:, :, 1], v_s[:, :, g]], axis=1) for g in range(N_GROUPS)]
            kv_new_p = [jnp.stack([k_p[:, :, g], v_p[:, :, g]], axis=2)[:, -min(WINDOWS[g], seq):]
                        for g in range(N_GROUPS)]
            kv_new_s = [jnp.stack([kc_s[g], vc_s[g]], axis=2)[:, -min(WINDOWS[g], caches[g].shape[1] + dec_seq):]
                        for g in range(N_GROUPS)]
        j = layer - N_A
        q, z = b_queries(rms_norm(hp, b_norm_pre[j]), b_w_in[j], pos_p)
        res = [band_attn_prompt(q[:, :, g], k_p[:, :, g], v_p[:, :, g], WINDOWS[g], DILATIONS[g])
               for g in range(N_GROUPS)]
        hp = hp + rms_norm(merge_groups(res, z, b_w_out[j]), b_norm_post[j])
        q, z = b_queries(rms_norm(hs, b_norm_pre[j]), b_w_in[j], pos_s)
        res = [window_attn_sample(q[:, :, g], kc_s[g], vc_s[g], caches[g].shape[1], WINDOWS[g], DILATIONS[g])
               for g in range(N_GROUPS)]
        hs = hs + rms_norm(merge_groups(res, z, b_w_out[j]), b_norm_post[j])
    return (hp, hs, jnp.stack(conv_p, axis=0), kv_new_p[0], kv_new_p[1], kv_new_p[2],
            jnp.stack(conv_s, axis=0), kv_new_s[0], kv_new_s[1], kv_new_s[2])
```

```python
import functools

import jax
import jax.numpy as jnp
from jax import lax
from jax.experimental import pallas as pl
from jax.experimental.pallas import tpu as pltpu

D_MODEL = 1024
HEAD_DIM = 128
N_GROUPS = 3
WINDOWS = (128, 512, 2048)
DILATIONS = (1, 4, 16)
HEADS = 8
KV_HEADS = 2
Q_PER_KV = HEADS // KV_HEADS
Q_WIDTH = N_GROUPS * HEADS * HEAD_DIM
KV_HALF = N_GROUPS * KV_HEADS * HEAD_DIM
PAST_LEN = 8192
ROPE_THETA = 10000.0
EPS = 1e-6
SCALE = HEAD_DIM ** -0.5
NEG = -1e30

BAND = 128
TM = 512
TA = 2048
CH = 256
VMEM_LIMIT = 52 * 1024 * 1024

F32 = jnp.float32
BF16 = jnp.bfloat16


def _dot(a, b):
    return jnp.dot(a, b, preferred_element_type=F32)


def _dot_nt(a, b):
    return lax.dot_general(a, b, (((1,), (1,)), ((), ())), preferred_element_type=F32)


def _rms(x, g):
    ms = jnp.mean(x * x, axis=-1, keepdims=True)
    return x * lax.rsqrt(ms + EPS) * g


def _silu(z):
    return z * jax.nn.sigmoid(z)


def _rope(x, cos, sin_signed):
    return x * cos + pltpu.roll(x, HEAD_DIM // 2, 1) * sin_signed


def _params(sem=None):
    return pltpu.CompilerParams(dimension_semantics=sem, vmem_limit_bytes=VMEM_LIMIT)


def _const_spec(shape):
    zeros = (0,) * len(shape)
    return pl.BlockSpec(shape, lambda *_: zeros, pipeline_mode=pl.Buffered(1))


def _conv_mix_chunk(h, win_ref, wconv_ref, j, u_prev1, u_prev2):
    cs = slice(j * CH, (j + 1) * CH)
    b_gate = _dot(h, win_ref[:, j * CH:(j + 1) * CH])
    c_gate = _dot(h, win_ref[:, D_MODEL + j * CH:D_MODEL + (j + 1) * CH])
    u_in = _dot(h, win_ref[:, 2 * D_MODEL + j * CH:2 * D_MODEL + (j + 1) * CH])
    z = _dot(h, win_ref[:, 3 * D_MODEL + j * CH:3 * D_MODEL + (j + 1) * CH])
    u = c_gate * u_in
    conv = (u_prev2(u, cs) * wconv_ref[0:1, cs] + u_prev1(u, cs) * wconv_ref[1:2, cs]
            + u * wconv_ref[2:3, cs])
    return u, b_gate * conv * _silu(z)


def _conv_prompt_kernel(x_ref, gpre_ref, win_ref, wconv_ref, wout_ref, gpost_ref,
                        y_ref, st_ref, tail_ref):
    @pl.when(pl.program_id(1) == 0)
    def _():
        tail_ref[...] = jnp.zeros_like(tail_ref)

    x = x_ref[0]
    h = _rms(x, gpre_ref[...]).astype(BF16)
    row = lax.broadcasted_iota(jnp.int32, (TM, CH), 0)

    def prev1(u, cs):
        return jnp.where(row == 0, tail_ref[1:2, cs], pltpu.roll(u, 1, 0))

    def prev2(u, cs):
        return jnp.where(row == 0, tail_ref[0:1, cs],
                         jnp.where(row == 1, tail_ref[1:2, cs], pltpu.roll(u, 2, 0)))

    acc = jnp.zeros((TM, D_MODEL), F32)
    for j in range(D_MODEL // CH):
        cs = slice(j * CH, (j + 1) * CH)
        u, y = _conv_mix_chunk(h, win_ref, wconv_ref, j, prev1, prev2)
        tail_ref[0:2, cs] = u[TM - 2:TM, :]
        acc = acc + _dot(y.astype(BF16), wout_ref[cs, :])
    y_ref[0] = x + _rms(acc, gpost_ref[...])
    st_ref[0] = tail_ref[0:2, :]


def _conv_layer_prompt(x, g_pre, w_in, w_conv, w_out, g_post):
    bsz, seq, _ = x.shape
    return pl.pallas_call(
        _conv_prompt_kernel,
        grid=(bsz, seq // TM),
        in_specs=[
            pl.BlockSpec((1, TM, D_MODEL), lambda b, i: (b, i, 0)),
            _const_spec((1, D_MODEL)),
            _const_spec((D_MODEL, 4 * D_MODEL)),
            _const_spec((3, D_MODEL)),
            _const_spec((D_MODEL, D_MODEL)),
            _const_spec((1, D_MODEL)),
        ],
        out_specs=[
            pl.BlockSpec((1, TM, D_MODEL), lambda b, i: (b, i, 0)),
            pl.BlockSpec((1, 2, D_MODEL), lambda b, i: (b, 0, 0)),
        ],
        out_shape=[
            jax.ShapeDtypeStruct((bsz, seq, D_MODEL), F32),
            jax.ShapeDtypeStruct((bsz, 2, D_MODEL), F32),
        ],
        scratch_shapes=[pltpu.VMEM((8, D_MODEL), F32)],
        compiler_params=_params(("arbitrary", "arbitrary")),
        name="conv_layer_prompt",
    )(x, g_pre, w_in, w_conv, w_out, g_post)


def _conv_sample_kernel(x_ref, p0_ref, p1_ref, gpre_ref, win_ref, wconv_ref, wout_ref, gpost_ref,
                        y_ref, u_ref):
    x = x_ref[...]
    h = _rms(x, gpre_ref[...]).astype(BF16)
    acc = jnp.zeros(x.shape, F32)
    for j in range(D_MODEL // CH):
        cs = slice(j * CH, (j + 1) * CH)
        u, y = _conv_mix_chunk(h, win_ref, wconv_ref, j,
                               lambda u, cs: p1_ref[:, cs], lambda u, cs: p0_ref[:, cs])
        u_ref[:, cs] = u
        acc = acc + _dot(y.astype(BF16), wout_ref[cs, :])
    y_ref[...] = x + _rms(acc, gpost_ref[...])


def _conv_layer_sample(x, past0, past1, g_pre, w_in, w_conv, w_out, g_post):
    n = x.shape[0]
    return pl.pallas_call(
        _conv_sample_kernel,
        out_shape=[jax.ShapeDtypeStruct((n, D_MODEL), F32)] * 2,
        compiler_params=_params(),
        name="conv_layer_sample",
    )(x, past0, past1, g_pre, w_in, w_conv, w_out, g_post)


def _kv_prompt_kernel(x_ref, g_ref, w_ref, cos_ref, sin_ref, kv_ref, c0_ref, c1_ref, c2_ref):
    h = _rms(x_ref[0], g_ref[...]).astype(BF16)
    cos = cos_ref[...]
    sin = sin_ref[...]
    kv = _dot(h, w_ref[...])
    caches = (c0_ref, c1_ref, c2_ref)
    for g in range(N_GROUPS):
        parts_k, parts_v = [], []
        for hh in range(KV_HEADS):
            c = (g * KV_HEADS + hh) * HEAD_DIM
            k = _rope(kv[:, c:c + HEAD_DIM], cos, sin)
            v = kv[:, KV_HALF + c:KV_HALF + c + HEAD_DIM]
            kv_ref[g, 0, hh, :, 0:HEAD_DIM] = k.astype(BF16)
            kv_ref[g, 0, hh, :, HEAD_DIM:2 * HEAD_DIM] = v.astype(BF16)
            parts_k.append(k)
            parts_v.append(v)
        rows = min(TM, WINDOWS[g])
        full = jnp.concatenate(parts_k + parts_v, axis=-1)
        caches[g][0] = full[TM - rows:, :]


def _kv_proj_prompt(x, g, w_kv, cos, sin):
    bsz, seq, _ = x.shape
    nt = seq // TM
    cache_specs, cache_shapes = [], []
    for grp in range(N_GROUPS):
        w = min(WINDOWS[grp], seq)
        rows = min(TM, w)
        first = nt - w // rows
        cache_specs.append(pl.BlockSpec(
            (1, rows, 4 * HEAD_DIM), lambda b, i, first=first: (b, jnp.maximum(i - first, 0), 0)))
        cache_shapes.append(jax.ShapeDtypeStruct((bsz, w, 4 * HEAD_DIM), F32))
    return pl.pallas_call(
        _kv_prompt_kernel,
        grid=(bsz, nt),
        in_specs=[
            pl.BlockSpec((1, TM, D_MODEL), lambda b, i: (b, i, 0)),
            _const_spec((1, D_MODEL)),
            _const_spec((D_MODEL, 2 * KV_HALF)),
            pl.BlockSpec((TM, HEAD_DIM), lambda b, i: (i, 0)),
            pl.BlockSpec((TM, HEAD_DIM), lambda b, i: (i, 0)),
        ],
        out_specs=[pl.BlockSpec((N_GROUPS, 1, KV_HEADS, TM, 2 * HEAD_DIM),
                                lambda b, i: (0, b, 0, i, 0))] + cache_specs,
        out_shape=[jax.ShapeDtypeStruct((N_GROUPS, bsz, KV_HEADS, seq, 2 * HEAD_DIM), BF16)]
        + cache_shapes,
        compiler_params=_params(("arbitrary", "arbitrary")),
        name="kv_proj_prompt",
    )(x, g, w_kv, cos, sin)


def _q_prompt_kernel(x_ref, g_ref, w_ref, cos_ref, sin_ref, q_ref, z_ref):
    h = _rms(x_ref[0], g_ref[...]).astype(BF16)
    cos = cos_ref[...]
    sin = sin_ref[...]
    width = Q_PER_KV * HEAD_DIM
    for g in range(N_GROUPS):
        for kh in range(KV_HEADS):
            c = (g * KV_HEADS + kh) * width
            qq = _dot(h, w_ref[:, c:c + width])
            for j in range(Q_PER_KV):
                head = _rope(qq[:, j * HEAD_DIM:(j + 1) * HEAD_DIM], cos, sin)
                q_ref[g, 0, kh, :, j * HEAD_DIM:(j + 1) * HEAD_DIM] = head.astype(BF16)
    z_ref[0] = _dot(h, w_ref[:, Q_WIDTH:])


def _q_proj_prompt(x, g, w_in, cos_q, sin_q):
    bsz, seq, _ = x.shape
    width = Q_PER_KV * HEAD_DIM
    return pl.pallas_call(
        _q_prompt_kernel,
        grid=(bsz, seq // TM),
        in_specs=[
            pl.BlockSpec((1, TM, D_MODEL), lambda b, i: (b, i, 0)),
            _const_spec((1, D_MODEL)),
            _const_spec((D_MODEL, Q_WIDTH + D_MODEL)),
            pl.BlockSpec((TM, HEAD_DIM), lambda b, i: (i, 0)),
            pl.BlockSpec((TM, HEAD_DIM), lambda b, i: (i, 0)),
        ],
        out_specs=[
            pl.BlockSpec((N_GROUPS, 1, KV_HEADS, TM, width), lambda b, i: (0, b, 0, i, 0)),
            pl.BlockSpec((1, TM, D_MODEL), lambda b, i: (b, i, 0)),
        ],
        out_shape=[
            jax.ShapeDtypeStruct((N_GROUPS, bsz, KV_HEADS, seq, width), BF16),
            jax.ShapeDtypeStruct((bsz, seq, D_MODEL), F32),
        ],
        compiler_params=_params(("arbitrary", "arbitrary")),
        name="q_proj_prompt",
    )(x, g, w_in, cos_q, sin_q)


def _attn_prompt_kernel(q_ref, kvc_ref, kvp_ref, o_ref, lse_ref, *, dil):
    first_tile = pl.program_id(2) == 0
    nrow = TA // dil
    nblk = nrow // BAND
    qw = Q_PER_KV * HEAD_DIM
    kw = 2 * HEAD_DIM
    m_rows = Q_PER_KV * BAND
    qi = lax.broadcasted_iota(jnp.int32, (m_rows, BAND), 0) % BAND
    jj = lax.broadcasted_iota(jnp.int32, (m_rows, BAND), 1)
    mask_cur = jj <= qi
    mask_prev = jj >= qi
    lane = lax.broadcasted_iota(jnp.int32, (BAND, HEAD_DIM), 1)
    for r in range(dil):
        for n in range(nblk):
            rows = slice(n * BAND, (n + 1) * BAND)
            q4 = jnp.concatenate(
                [q_ref[0, 0, rows, r * qw + j * HEAD_DIM:r * qw + (j + 1) * HEAD_DIM]
                 for j in range(Q_PER_KV)], axis=0)
            k_cur = kvc_ref[0, 0, rows, r * kw:r * kw + HEAD_DIM]
            v_cur = kvc_ref[0, 0, rows, r * kw + HEAD_DIM:(r + 1) * kw]
            if n == 0:
                prow = slice(nrow - BAND, nrow)
                k_prev = kvp_ref[0, 0, prow, r * kw:r * kw + HEAD_DIM]
                v_prev = kvp_ref[0, 0, prow, r * kw + HEAD_DIM:(r + 1) * kw]
            else:
                prow = slice((n - 1) * BAND, n * BAND)
                k_prev = kvc_ref[0, 0, prow, r * kw:r * kw + HEAD_DIM]
                v_prev = kvc_ref[0, 0, prow, r * kw + HEAD_DIM:(r + 1) * kw]
            s_cur = jnp.where(mask_cur, _dot_nt(q4, k_cur), NEG)
            s_prev = jnp.where(mask_prev, _dot_nt(q4, k_prev), NEG)
            if n == 0:
                s_prev = jnp.where(first_tile, NEG, s_prev)
            m = jnp.maximum(jnp.max(s_cur, axis=-1, keepdims=True),
                            jnp.max(s_prev, axis=-1, keepdims=True))
            p_cur = jnp.exp(s_cur - m)
            p_prev = jnp.exp(s_prev - m)
            l = jnp.sum(p_cur, axis=-1, keepdims=True) + jnp.sum(p_prev, axis=-1, keepdims=True)
            o = (_dot(p_cur.astype(BF16), v_cur) + _dot(p_prev.astype(BF16), v_prev)) / l
            lse = m + jnp.log(l)
            lse_tile = jnp.zeros((BAND, HEAD_DIM), F32)
            for j in range(Q_PER_KV):
                hs = slice(j * BAND, (j + 1) * BAND)
                o_ref[0, 0, rows, r * qw + j * HEAD_DIM:r * qw + (j + 1) * HEAD_DIM] = (
                    o[hs].astype(BF16))
                lse_tile = jnp.where(lane == j, lse[hs], lse_tile)
            lse_ref[0, 0, rows, r * HEAD_DIM:(r + 1) * HEAD_DIM] = lse_tile


def _attn_prompt(q_g, kv_g, dil):
    bsz, _, seq, qw = q_g.shape
    kw = kv_g.shape[-1]
    nv = seq // dil
    rows = TA // dil
    q_v = q_g.reshape(bsz, KV_HEADS, nv, dil * qw)
    kv_v = kv_g.reshape(bsz, KV_HEADS, nv, dil * kw)
    o, lse = pl.pallas_call(
        functools.partial(_attn_prompt_kernel, dil=dil),
        grid=(bsz, KV_HEADS, seq // TA),
        in_specs=[
            pl.BlockSpec((1, 1, rows, dil * qw), lambda b, h, i: (b, h, i, 0)),
            pl.BlockSpec((1, 1, rows, dil * kw), lambda b, h, i: (b, h, i, 0)),
            pl.BlockSpec((1, 1, rows, dil * kw), lambda b, h, i: (b, h, jnp.maximum(i - 1, 0), 0)),
        ],
        out_specs=[
            pl.BlockSpec((1, 1, rows, dil * qw), lambda b, h, i: (b, h, i, 0)),
            pl.BlockSpec((1, 1, rows, dil * HEAD_DIM), lambda b, h, i: (b, h, i, 0)),
        ],
        out_shape=[
            jax.ShapeDtypeStruct((bsz, KV_HEADS, nv, dil * qw), BF16),
            jax.ShapeDtypeStruct((bsz, KV_HEADS, nv, dil * HEAD_DIM), F32),
        ],
        compiler_params=_params(("arbitrary", "arbitrary", "arbitrary")),
        name=f"attn_prompt_d{dil}",
    )(q_v, kv_v, kv_v)
    return o.reshape(bsz, KV_HEADS, seq, qw), lse.reshape(bsz, KV_HEADS, seq, HEAD_DIM)


def _merge_out_kernel(o0_ref, o1_ref, o2_ref, l0_ref, l1_ref, l2_ref, z_ref, x_ref,
                      wout_ref, gpost_ref, y_ref):
    o_refs = (o0_ref, o1_ref, o2_ref)
    l_refs = (l0_ref, l1_ref, l2_ref)
    z = z_ref[0]
    pieces = []
    for kh in range(KV_HEADS):
        for j in range(Q_PER_KV):
            hcols = slice(j * HEAD_DIM, (j + 1) * HEAD_DIM)
            lses = [l_refs[g][0, kh, :, j:j + 1] for g in range(N_GROUPS)]
            m = jnp.maximum(jnp.maximum(lses[0], lses[1]), lses[2])
            es = [jnp.exp(v - m) for v in lses]
            inv = 1.0 / (es[0] + es[1] + es[2])
            o = sum((es[g] * inv) * o_refs[g][0, kh, :, hcols].astype(F32)
                    for g in range(N_GROUPS))
            head = kh * Q_PER_KV + j
            gate = _silu(z[:, head * HEAD_DIM:(head + 1) * HEAD_DIM])
            pieces.append((o * gate).astype(BF16))
    y = jnp.concatenate(pieces, axis=-1)
    out = _dot(y, wout_ref[...])
    y_ref[0] = x_ref[0] + _rms(out, gpost_ref[...])


def _merge_out_prompt(os_, lses, z, x, w_out, g_post):
    bsz, seq, _ = x.shape
    qw = Q_PER_KV * HEAD_DIM
    o_spec = pl.BlockSpec((1, KV_HEADS, TM, qw), lambda b, i: (b, 0, i, 0))
    l_spec = pl.BlockSpec((1, KV_HEADS, TM, HEAD_DIM), lambda b, i: (b, 0, i, 0))
    x_spec = pl.BlockSpec((1, TM, D_MODEL), lambda b, i: (b, i, 0))
    return pl.pallas_call(
        _merge_out_kernel,
        grid=(bsz, seq // TM),
        in_specs=[o_spec] * 3 + [l_spec] * 3 + [x_spec, x_spec,
                                                _const_spec((D_MODEL, D_MODEL)),
                                                _const_spec((1, D_MODEL))],
        out_specs=x_spec,
        out_shape=jax.ShapeDtypeStruct((bsz, seq, D_MODEL), F32),
        compiler_params=_params(("arbitrary", "arbitrary")),
        name="merge_out_prompt",
    )(*os_, *lses, z, x, w_out, g_post)


def _proj_sample_kernel(x_ref, g_ref, w_ref, cos_ref, sin_ref, y_ref, *, rope_heads):
    h = _rms(x_ref[...], g_ref[...]).astype(BF16)
    y = _dot(h, w_ref[...])
    cos = cos_ref[...]
    sin = sin_ref[...]
    for hd in range(rope_heads):
        cs = slice(hd * HEAD_DIM, (hd + 1) * HEAD_DIM)
        y_ref[:, cs] = _rope(y[:, cs], cos, sin)
    rest = rope_heads * HEAD_DIM
    y_ref[:, rest:] = y[:, rest:]


def _proj_sample(x, g, w, cos, sin, rope_heads):
    n = x.shape[0]
    return pl.pallas_call(
        functools.partial(_proj_sample_kernel, rope_heads=rope_heads),
        out_shape=jax.ShapeDtypeStruct((n, w.shape[1]), F32),
        compiler_params=_params(),
        name=f"proj_sample_{rope_heads}",
    )(x, g, w, cos, sin)


def _attn_sample_kernel(qz_ref, kvn_ref, c0_ref, c1_ref, c2_ref, y_ref, *out_cache_refs):
    c_refs = (c0_ref, c1_ref, c2_ref)
    qz = qz_ref[0]
    kvn = kvn_ref[0]
    pieces = []
    for kh in range(KV_HEADS):
        scores, s_new, vals, v_new = [], [], [], []
        for g in range(N_GROUPS):
            stride = 2 * KV_HEADS * DILATIONS[g]
            c = (g * HEADS + kh * Q_PER_KV) * HEAD_DIM
            q4 = jnp.concatenate(
                [qz[:, c + j * HEAD_DIM:c + (j + 1) * HEAD_DIM] for j in range(Q_PER_KV)], axis=0)
            q8 = jnp.concatenate([q4, q4], axis=0)
            k_rows = c_refs[g][0, pl.ds(kh, BAND, stride=stride), :]
            v_rows = c_refs[g][0, pl.ds(KV_HEADS + kh, BAND, stride=stride), :]
            kc = (g * KV_HEADS + kh) * HEAD_DIM
            k_n = kvn[:, kc:kc + HEAD_DIM]
            v_n = kvn[:, KV_HALF + kc:KV_HALF + kc + HEAD_DIM]
            scores.append(_dot_nt(q8.astype(BF16), k_rows.astype(BF16)))
            s_new.append(jnp.sum(q8 * k_n, axis=-1, keepdims=True))
            vals.append(v_rows.astype(BF16))
            v_new.append(v_n)
        m = functools.reduce(jnp.maximum,
                             [jnp.max(s, axis=-1, keepdims=True) for s in scores] + s_new)
        ps = [jnp.exp(s - m) for s in scores]
        pn = [jnp.exp(s - m) for s in s_new]
        l = sum(jnp.sum(p, axis=-1, keepdims=True) for p in ps) + sum(pn)
        o = sum(_dot(ps[g].astype(BF16), vals[g]) + pn[g] * v_new[g] for g in range(N_GROUPS)) / l
        pieces += [o[j:j + 1, :] for j in range(Q_PER_KV)]
    o_row = jnp.concatenate(pieces, axis=-1)
    y_ref[0] = o_row * _silu(qz[:, Q_WIDTH:])
    for g, oc_ref in enumerate(out_cache_refs):
        nr = 2 * KV_HEADS
        keep = (WINDOWS[g] - 1) * nr
        oc_ref[0, 0:keep, :] = c_refs[g][0, nr:nr + keep, :]
        new_rows = [kvn[:, half + (g * KV_HEADS + hh) * HEAD_DIM:half + (g * KV_HEADS + hh + 1) * HEAD_DIM]
                    for half in (0, KV_HALF) for hh in range(KV_HEADS)]
        oc_ref[0, keep:keep + nr, :] = jnp.concatenate(new_rows, axis=0)


def _attn_sample(qz, kv_new, caches, write_caches):
    n = qz.shape[0]
    cache_specs = [pl.BlockSpec((1, c.shape[1], HEAD_DIM), lambda b: (b, 0, 0)) for c in caches]
    out_specs = [pl.BlockSpec((1, 1, D_MODEL), lambda b: (b, 0, 0))]
    out_shape = [jax.ShapeDtypeStruct((n, 1, D_MODEL), F32)]
    if write_caches:
        out_specs += cache_specs
        out_shape += [jax.ShapeDtypeStruct(c.shape, F32) for c in caches]
    return pl.pallas_call(
        _attn_sample_kernel,
        grid=(n,),
        in_specs=[
            pl.BlockSpec((1, 1, qz.shape[-1]), lambda b: (b, 0, 0)),
            pl.BlockSpec((1, 1, kv_new.shape[-1]), lambda b: (b, 0, 0)),
        ] + cache_specs,
        out_specs=out_specs,
        out_shape=out_shape,
        compiler_params=_params(("arbitrary",)),
        name="attn_sample_w" if write_caches else "attn_sample",
    )(qz, kv_new, *caches)


def _out_sample_kernel(y_ref, x_ref, wout_ref, gpost_ref, o_ref):
    out = _dot(y_ref[...].astype(BF16), wout_ref[...])
    o_ref[...] = x_ref[...] + _rms(out, gpost_ref[...])


def _out_sample(y, x, w_out, g_post):
    return pl.pallas_call(
        _out_sample_kernel,
        out_shape=jax.ShapeDtypeStruct(x.shape, F32),
        compiler_params=_params(),
        name="out_sample",
    )(y, x, w_out, g_post)


def _rope_tables(pos):
    half = HEAD_DIM // 2
    inv_freq = ROPE_THETA ** (-jnp.arange(half, dtype=F32) * 2.0 / HEAD_DIM)
    ang = pos.astype(F32)[:, None] * inv_freq[None, :]
    cos = jnp.cos(ang)
    sin = jnp.sin(ang)
    return jnp.concatenate([cos, cos], axis=-1), jnp.concatenate([-sin, sin], axis=-1)


def kernel(x_prompt, x_sample, state_conv, cache_kv_g0, cache_kv_g1, cache_kv_g2,
           a_norm_pre, a_w_in, a_conv, a_w_out, a_norm_post,
           kv_norm, w_kv, b_norm_pre, b_w_in, b_w_out, b_norm_post):
    bsz, seq, _ = x_prompt.shape
    n_s = x_sample.shape[0]
    n_a = a_w_in.shape[0]
    n_b = b_w_in.shape[0]
    assert x_sample.shape[1] == 1 and seq % TA == 0

    cos_p, sin_p = _rope_tables(jnp.arange(seq))
    cos_s, sin_s = _rope_tables(PAST_LEN + jnp.arange(1))
    row = lambda v: v.reshape(1, -1)

    a_w_in16 = a_w_in.astype(BF16)
    a_w_out16 = a_w_out.astype(BF16)
    w_kv16 = w_kv.astype(BF16)
    b_w_in16 = b_w_in.astype(BF16)
    b_w_out16 = b_w_out.astype(BF16)

    hp = x_prompt
    hs = x_sample.reshape(n_s, D_MODEL)
    conv_p, conv_s = [], []
    for layer in range(n_a):
        args = (row(a_norm_pre[layer]), a_w_in16[layer], a_conv[layer], a_w_out16[layer],
                row(a_norm_post[layer]))
        hp, st_p = _conv_layer_prompt(hp, *args)
        past0 = state_conv[layer, :, 0]
        past1 = state_conv[layer, :, 1]
        hs, u_s = _conv_layer_sample(hs, past0, past1, *args)
        conv_p.append(st_p)
        conv_s.append(jnp.stack([past1, u_s], axis=1))

    kv_p, c0, c1, c2 = _kv_proj_prompt(hp, row(kv_norm), w_kv16, cos_p, sin_p)
    kv_new_p = [c.reshape(bsz, c.shape[1], 2, KV_HEADS, HEAD_DIM) for c in (c0, c1, c2)]
    kv_s = _proj_sample(hs, row(kv_norm), w_kv16, cos_s, sin_s, N_GROUPS * KV_HEADS)

    caches = [c.reshape(n_s, c.shape[1] * 2 * KV_HEADS, HEAD_DIM)
              for c in (cache_kv_g0, cache_kv_g1, cache_kv_g2)]
    kv_new_s = None
    for j in range(n_b):
        g_pre = row(b_norm_pre[j])
        g_post = row(b_norm_post[j])
        q, z = _q_proj_prompt(hp, g_pre, b_w_in16[j], cos_p * SCALE, sin_p * SCALE)
        res = [_attn_prompt(q[g], kv_p[g], DILATIONS[g]) for g in range(N_GROUPS)]
        hp = _merge_out_prompt([r[0] for r in res], [r[1] for r in res], z, hp,
                               b_w_out16[j], g_post)
        qz = _proj_sample(hs, g_pre, b_w_in16[j], cos_s * SCALE, sin_s * SCALE, N_GROUPS * HEADS)
        outs = _attn_sample(qz.reshape(n_s, 1, -1), kv_s.reshape(n_s, 1, -1), caches,
                            write_caches=(j == 0))
        if j == 0:
            kv_new_s = [o.reshape(n_s, WINDOWS[g], 2, KV_HEADS, HEAD_DIM)
                        for g, o in enumerate(outs[1:])]
        hs = _out_sample(outs[0].reshape(n_s, D_MODEL), hs, b_w_out16[j], g_post)

    return (hp, hs.reshape(n_s, 1, D_MODEL), jnp.stack(conv_p, axis=0),
            kv_new_p[0], kv_new_p[1], kv_new_p[2],
            jnp.stack(conv_s, axis=0), kv_new_s[0], kv_new_s[1], kv_new_s[2])
```

```python
import functools

import jax
import jax.numpy as jnp
from jax import lax
from jax.experimental import pallas as pl
from jax.experimental.pallas import tpu as pltpu

D_MODEL = 1024
HEAD_DIM = 128
N_GROUPS = 3
WINDOWS = (128, 512, 2048)
DILATIONS = (1, 4, 16)
HEADS = 8
KV_HEADS = 2
Q_PER_KV = HEADS // KV_HEADS
Q_WIDTH = N_GROUPS * HEADS * HEAD_DIM
KV_HALF = N_GROUPS * KV_HEADS * HEAD_DIM
PAST_LEN = 8192
ROPE_THETA = 10000.0
EPS = 1e-6
SCALE = HEAD_DIM ** -0.5
NEG = -1e30
LOG2E = 1.4426950408889634
LN2 = 0.6931471805599453

BAND = 128
TM = 512
TA = 2048
CH = 256
VMEM_LIMIT = 52 * 1024 * 1024

F32 = jnp.float32
BF16 = jnp.bfloat16


def _dot(a, b):
    return jnp.dot(a, b, preferred_element_type=F32)


def _dot_nt(a, b):
    return lax.dot_general(a, b, (((1,), (1,)), ((), ())), preferred_element_type=F32)


def _rms(x, g):
    ms = jnp.mean(x * x, axis=-1, keepdims=True)
    return x * lax.rsqrt(ms + EPS) * g


def _silu(z):
    return z * jax.nn.sigmoid(z)


def _rope(x, cos, sin_signed):
    return x * cos + pltpu.roll(x, HEAD_DIM // 2, 1) * sin_signed


def _params(sem=None):
    return pltpu.CompilerParams(dimension_semantics=sem, vmem_limit_bytes=VMEM_LIMIT)


def _const_spec(shape):
    zeros = (0,) * len(shape)
    return pl.BlockSpec(shape, lambda *_: zeros, pipeline_mode=pl.Buffered(1))


def _conv_mix_chunk(h, win_ref, wconv_ref, j, u_prev1, u_prev2):
    cs = slice(j * CH, (j + 1) * CH)
    b_gate = _dot(h, win_ref[:, j * CH:(j + 1) * CH])
    c_gate = _dot(h, win_ref[:, D_MODEL + j * CH:D_MODEL + (j + 1) * CH])
    u_in = _dot(h, win_ref[:, 2 * D_MODEL + j * CH:2 * D_MODEL + (j + 1) * CH])
    z = _dot(h, win_ref[:, 3 * D_MODEL + j * CH:3 * D_MODEL + (j + 1) * CH])
    u = c_gate * u_in
    conv = (u_prev2(u, cs) * wconv_ref[0:1, cs] + u_prev1(u, cs) * wconv_ref[1:2, cs]
            + u * wconv_ref[2:3, cs])
    return u, b_gate * conv * _silu(z)


def _conv_prompt_kernel(x_ref, gpre_ref, win_ref, wconv_ref, wout_ref, gpost_ref,
                        y_ref, st_ref, tail_ref):
    @pl.when(pl.program_id(1) == 0)
    def _():
        tail_ref[...] = jnp.zeros_like(tail_ref)

    x = x_ref[0]
    h = _rms(x, gpre_ref[...]).astype(BF16)
    row = lax.broadcasted_iota(jnp.int32, (TM, CH), 0)

    def prev1(u, cs):
        return jnp.where(row == 0, tail_ref[1:2, cs], pltpu.roll(u, 1, 0))

    def prev2(u, cs):
        return jnp.where(row == 0, tail_ref[0:1, cs],
                         jnp.where(row == 1, tail_ref[1:2, cs], pltpu.roll(u, 2, 0)))

    acc = jnp.zeros((TM, D_MODEL), F32)
    for j in range(D_MODEL // CH):
        cs = slice(j * CH, (j + 1) * CH)
        u, y = _conv_mix_chunk(h, win_ref, wconv_ref, j, prev1, prev2)
        tail_ref[0:2, cs] = u[TM - 2:TM, :]
        acc = acc + _dot(y.astype(BF16), wout_ref[cs, :])
    y_ref[0] = x + _rms(acc, gpost_ref[...])
    st_ref[0] = tail_ref[0:2, :]


def _conv_layer_prompt(x, g_pre, w_in, w_conv, w_out, g_post):
    bsz, seq, _ = x.shape
    return pl.pallas_call(
        _conv_prompt_kernel,
        grid=(bsz, seq // TM),
        in_specs=[
            pl.BlockSpec((1, TM, D_MODEL), lambda b, i: (b, i, 0)),
            _const_spec((1, D_MODEL)),
            _const_spec((D_MODEL, 4 * D_MODEL)),
            _const_spec((3, D_MODEL)),
            _const_spec((D_MODEL, D_MODEL)),
            _const_spec((1, D_MODEL)),
        ],
        out_specs=[
            pl.BlockSpec((1, TM, D_MODEL), lambda b, i: (b, i, 0)),
            pl.BlockSpec((1, 2, D_MODEL), lambda b, i: (b, 0, 0)),
        ],
        out_shape=[
            jax.ShapeDtypeStruct((bsz, seq, D_MODEL), F32),
            jax.ShapeDtypeStruct((bsz, 2, D_MODEL), F32),
        ],
        scratch_shapes=[pltpu.VMEM((8, D_MODEL), F32)],
        compiler_params=_params(("arbitrary", "arbitrary")),
        name="conv_layer_prompt",
    )(x, g_pre, w_in, w_conv, w_out, g_post)


def _conv_sample_kernel(x_ref, p0_ref, p1_ref, gpre_ref, win_ref, wconv_ref, wout_ref, gpost_ref,
                        y_ref, u_ref):
    x = x_ref[...]
    h = _rms(x, gpre_ref[...]).astype(BF16)
    acc = jnp.zeros(x.shape, F32)
    for j in range(D_MODEL // CH):
        cs = slice(j * CH, (j + 1) * CH)
        u, y = _conv_mix_chunk(h, win_ref, wconv_ref, j,
                               lambda u, cs: p1_ref[:, cs], lambda u, cs: p0_ref[:, cs])
        u_ref[:, cs] = u
        acc = acc + _dot(y.astype(BF16), wout_ref[cs, :])
    y_ref[...] = x + _rms(acc, gpost_ref[...])


def _conv_layer_sample(x, past0, past1, g_pre, w_in, w_conv, w_out, g_post):
    n = x.shape[0]
    return pl.pallas_call(
        _conv_sample_kernel,
        out_shape=[jax.ShapeDtypeStruct((n, D_MODEL), F32)] * 2,
        compiler_params=_params(),
        name="conv_layer_sample",
    )(x, past0, past1, g_pre, w_in, w_conv, w_out, g_post)


def _store_by_residue(slab_ref, val, out_ref, lead, dil, col, block_w):
    rows = val.shape[0]
    if dil == 1:
        out_ref[lead + (slice(None), slice(col, col + HEAD_DIM))] = val.astype(BF16)
        return
    slab_ref[...] = val
    for r in range(dil):
        piece = slab_ref[pl.ds(r, rows // dil, stride=dil), :]
        c = r * block_w + col
        out_ref[lead + (slice(None), slice(c, c + HEAD_DIM))] = piece.astype(BF16)


def _kv_prompt_kernel(x_ref, g_ref, w_ref, cos_ref, sin_ref, kv0_ref, kv1_ref, kv2_ref,
                      c0_ref, c1_ref, c2_ref, slab_ref):
    h = _rms(x_ref[0], g_ref[...]).astype(BF16)
    cos = cos_ref[...]
    sin = sin_ref[...]
    kv = _dot(h, w_ref[...])
    kv_refs = (kv0_ref, kv1_ref, kv2_ref)
    caches = (c0_ref, c1_ref, c2_ref)
    for g in range(N_GROUPS):
        parts_k, parts_v = [], []
        for hh in range(KV_HEADS):
            c = (g * KV_HEADS + hh) * HEAD_DIM
            k = _rope(kv[:, c:c + HEAD_DIM], cos, sin)
            v = kv[:, KV_HALF + c:KV_HALF + c + HEAD_DIM]
            slot = 2 * (g * KV_HEADS + hh)
            _store_by_residue(slab_ref.at[slot], k, kv_refs[g], (0, hh), DILATIONS[g], 0,
                              2 * HEAD_DIM)
            _store_by_residue(slab_ref.at[slot + 1], v, kv_refs[g], (0, hh), DILATIONS[g], HEAD_DIM,
                              2 * HEAD_DIM)
            parts_k.append(k)
            parts_v.append(v)
        rows = min(TM, WINDOWS[g])
        full = jnp.concatenate(parts_k + parts_v, axis=-1)
        caches[g][0] = full[TM - rows:, :]


def _kv_proj_prompt(x, g, w_kv, cos, sin):
    bsz, seq, _ = x.shape
    nt = seq // TM
    cache_specs, cache_shapes = [], []
    for grp in range(N_GROUPS):
        w = min(WINDOWS[grp], seq)
        rows = min(TM, w)
        first = nt - w // rows
        cache_specs.append(pl.BlockSpec(
            (1, rows, 4 * HEAD_DIM), lambda b, i, first=first: (b, jnp.maximum(i - first, 0), 0)))
        cache_shapes.append(jax.ShapeDtypeStruct((bsz, w, 4 * HEAD_DIM), F32))
    kw = 2 * HEAD_DIM
    kv_specs = [pl.BlockSpec((1, KV_HEADS, TM // d, d * kw), lambda b, i: (b, 0, i, 0))
                for d in DILATIONS]
    kv_shapes = [jax.ShapeDtypeStruct((bsz, KV_HEADS, seq // d, d * kw), BF16) for d in DILATIONS]
    return pl.pallas_call(
        _kv_prompt_kernel,
        grid=(bsz, nt),
        in_specs=[
            pl.BlockSpec((1, TM, D_MODEL), lambda b, i: (b, i, 0)),
            _const_spec((1, D_MODEL)),
            _const_spec((D_MODEL, 2 * KV_HALF)),
            pl.BlockSpec((TM, HEAD_DIM), lambda b, i: (i, 0)),
            pl.BlockSpec((TM, HEAD_DIM), lambda b, i: (i, 0)),
        ],
        out_specs=kv_specs + cache_specs,
        out_shape=kv_shapes + cache_shapes,
        scratch_shapes=[pltpu.VMEM((2 * N_GROUPS * KV_HEADS, TM, HEAD_DIM), F32)],
        compiler_params=_params(("arbitrary", "arbitrary")),
        name="kv_proj_prompt",
    )(x, g, w_kv, cos, sin)


def _q_prompt_kernel(x_ref, g_ref, w_ref, cos_ref, sin_ref, q0_ref, q1_ref, q2_ref, z_ref,
                     slab_ref):
    h = _rms(x_ref[0], g_ref[...]).astype(BF16)
    cos = cos_ref[...]
    sin = sin_ref[...]
    width = Q_PER_KV * HEAD_DIM
    q_refs = (q0_ref, q1_ref, q2_ref)
    for g in range(N_GROUPS):
        for kh in range(KV_HEADS):
            c = (g * KV_HEADS + kh) * width
            qq = _dot(h, w_ref[:, c:c + width])
            for j in range(Q_PER_KV):
                head = _rope(qq[:, j * HEAD_DIM:(j + 1) * HEAD_DIM], cos, sin)
                slot = (g * KV_HEADS + kh) * Q_PER_KV + j
                _store_by_residue(slab_ref.at[slot], head, q_refs[g], (0, kh), DILATIONS[g],
                                  j * HEAD_DIM, width)
    z_ref[0] = _dot(h, w_ref[:, Q_WIDTH:])


def _q_proj_prompt(x, g, w_in, cos_q, sin_q):
    bsz, seq, _ = x.shape
    width = Q_PER_KV * HEAD_DIM
    q_specs = [pl.BlockSpec((1, KV_HEADS, TM // d, d * width), lambda b, i: (b, 0, i, 0))
               for d in DILATIONS]
    q_shapes = [jax.ShapeDtypeStruct((bsz, KV_HEADS, seq // d, d * width), BF16) for d in DILATIONS]
    return pl.pallas_call(
        _q_prompt_kernel,
        grid=(bsz, seq // TM),
        in_specs=[
            pl.BlockSpec((1, TM, D_MODEL), lambda b, i: (b, i, 0)),
            _const_spec((1, D_MODEL)),
            _const_spec((D_MODEL, Q_WIDTH + D_MODEL)),
            pl.BlockSpec((TM, HEAD_DIM), lambda b, i: (i, 0)),
            pl.BlockSpec((TM, HEAD_DIM), lambda b, i: (i, 0)),
        ],
        out_specs=q_specs + [pl.BlockSpec((1, TM, D_MODEL), lambda b, i: (b, i, 0))],
        out_shape=q_shapes + [jax.ShapeDtypeStruct((bsz, seq, D_MODEL), F32)],
        scratch_shapes=[pltpu.VMEM((N_GROUPS * HEADS, TM, HEAD_DIM), F32)],
        compiler_params=_params(("arbitrary", "arbitrary")),
        name="q_proj_prompt",
    )(x, g, w_in, cos_q, sin_q)


def _attn_prompt_kernel(q_ref, kvc_ref, kvp_ref, bias_ref, o_ref, lse_ref, bias0_ref, *stage,
                        dil):
    nrow = TA // dil
    nblk = nrow // BAND
    qw = Q_PER_KV * HEAD_DIM
    kw = 2 * HEAD_DIM
    bias0_ref[...] = jnp.where(pl.program_id(2) == 0, bias_ref[1], bias_ref[0])
    ones = jnp.ones((2 * BAND, HEAD_DIM), BF16)
    lane = lax.broadcasted_iota(jnp.int32, (BAND, HEAD_DIM), 1)
    for r in range(dil):
        kcol = slice(r * kw, r * kw + HEAD_DIM)
        vcol = slice(r * kw + HEAD_DIM, (r + 1) * kw)
        for n in range(nblk):
            rows = slice(n * BAND, (n + 1) * BAND)
            q4 = jnp.concatenate(
                [q_ref[0, 0, rows, r * qw + j * HEAD_DIM:r * qw + (j + 1) * HEAD_DIM]
                 for j in range(Q_PER_KV)], axis=0)
            if n == 0:
                prow = slice(nrow - BAND, nrow)
                k_cat = jnp.concatenate([kvp_ref[0, 0, prow, kcol], kvc_ref[0, 0, rows, kcol]], axis=0)
                v_cat = jnp.concatenate([kvp_ref[0, 0, prow, vcol], kvc_ref[0, 0, rows, vcol]], axis=0)
                bias = bias0_ref[...]
            else:
                both = slice((n - 1) * BAND, (n + 1) * BAND)
                k_cat = kvc_ref[0, 0, both, kcol]
                v_cat = kvc_ref[0, 0, both, vcol]
                bias = bias_ref[0]
            s = _dot_nt(q4, k_cat) + bias
            m = jnp.max(s, axis=-1, keepdims=True)
            p = jnp.exp2(s - m).astype(BF16)
            ov = _dot(p, jnp.concatenate([v_cat, ones], axis=1))
            l = ov[:, HEAD_DIM:]
            o = ov[:, :HEAD_DIM] / l
            lse = m * LN2 + jnp.log(l)
            lse_tile = jnp.zeros((BAND, HEAD_DIM), F32)
            for j in range(Q_PER_KV):
                lse_tile = jnp.where(lane == j, lse[j * BAND:(j + 1) * BAND], lse_tile)
            if dil == 1:
                for j in range(Q_PER_KV):
                    o_ref[0, 0, rows, j * HEAD_DIM:(j + 1) * HEAD_DIM] = (
                        o[j * BAND:(j + 1) * BAND].astype(BF16))
                lse_ref[0, 0, rows, :] = lse_tile
            else:
                so_ref, sl_ref = stage
                trows = pl.ds(n * BAND * dil + r, BAND, stride=dil)
                for j in range(Q_PER_KV):
                    so_ref[j, trows, :] = o[j * BAND:(j + 1) * BAND]
                sl_ref[trows, :] = lse_tile
    if dil > 1:
        so_ref, sl_ref = stage
        for j in range(Q_PER_KV):
            o_ref[0, 0, :, j * HEAD_DIM:(j + 1) * HEAD_DIM] = so_ref[j].astype(BF16)
        lse_ref[0, 0] = sl_ref[...]


def _band_bias():
    qi = lax.broadcasted_iota(jnp.int32, (Q_PER_KV * BAND, 2 * BAND), 0) % BAND
    jj = lax.broadcasted_iota(jnp.int32, (Q_PER_KV * BAND, 2 * BAND), 1)
    ok_prev = (jj < BAND) & (jj >= qi)
    ok_cur = (jj >= BAND) & (jj - BAND <= qi)
    neg = jnp.full(qi.shape, NEG, F32)
    return jnp.stack([jnp.where(ok_prev | ok_cur, 0.0, neg), jnp.where(ok_cur, 0.0, neg)])


def _attn_prompt(q_g, kv_g, bias, dil):
    bsz, _, nv, qwd = q_g.shape
    seq = nv * dil
    qw = qwd // dil
    kw = kv_g.shape[-1] // dil
    rows = TA // dil
    scratch = [pltpu.VMEM((Q_PER_KV * BAND, 2 * BAND), F32)]
    if dil > 1:
        scratch += [pltpu.VMEM((Q_PER_KV, TA, HEAD_DIM), F32), pltpu.VMEM((TA, HEAD_DIM), F32)]
    return pl.pallas_call(
        functools.partial(_attn_prompt_kernel, dil=dil),
        grid=(bsz, KV_HEADS, seq // TA),
        in_specs=[
            pl.BlockSpec((1, 1, rows, dil * qw), lambda b, h, i: (b, h, i, 0)),
            pl.BlockSpec((1, 1, rows, dil * kw), lambda b, h, i: (b, h, i, 0)),
            pl.BlockSpec((1, 1, rows, dil * kw), lambda b, h, i: (b, h, jnp.maximum(i - 1, 0), 0)),
            _const_spec(bias.shape),
        ],
        out_specs=[
            pl.BlockSpec((1, 1, TA, qw), lambda b, h, i: (b, h, i, 0)),
            pl.BlockSpec((1, 1, TA, HEAD_DIM), lambda b, h, i: (b, h, i, 0)),
        ],
        out_shape=[
            jax.ShapeDtypeStruct((bsz, KV_HEADS, seq, qw), BF16),
            jax.ShapeDtypeStruct((bsz, KV_HEADS, seq, HEAD_DIM), F32),
        ],
        scratch_shapes=scratch,
        compiler_params=_params(("arbitrary", "arbitrary", "arbitrary")),
        name=f"attn_prompt_d{dil}",
    )(q_g, kv_g, kv_g, bias)


def _merge_out_kernel(o0_ref, o1_ref, o2_ref, l0_ref, l1_ref, l2_ref, z_ref, x_ref,
                      wout_ref, gpost_ref, y_ref):
    o_refs = (o0_ref, o1_ref, o2_ref)
    l_refs = (l0_ref, l1_ref, l2_ref)
    z = z_ref[0]
    pieces = []
    for kh in range(KV_HEADS):
        for j in range(Q_PER_KV):
            hcols = slice(j * HEAD_DIM, (j + 1) * HEAD_DIM)
            lses = [l_refs[g][0, kh, :, j:j + 1] for g in range(N_GROUPS)]
            m = jnp.maximum(jnp.maximum(lses[0], lses[1]), lses[2])
            es = [jnp.exp(v - m) for v in lses]
            inv = 1.0 / (es[0] + es[1] + es[2])
            o = sum((es[g] * inv) * o_refs[g][0, kh, :, hcols].astype(F32)
                    for g in range(N_GROUPS))
            head = kh * Q_PER_KV + j
            gate = _silu(z[:, head * HEAD_DIM:(head + 1) * HEAD_DIM])
            pieces.append((o * gate).astype(BF16))
    y = jnp.concatenate(pieces, axis=-1)
    out = _dot(y, wout_ref[...])
    y_ref[0] = x_ref[0] + _rms(out, gpost_ref[...])


def _merge_out_prompt(os_, lses, z, x, w_out, g_post):
    bsz, seq, _ = x.shape
    qw = Q_PER_KV * HEAD_DIM
    o_spec = pl.BlockSpec((1, KV_HEADS, TM, qw), lambda b, i: (b, 0, i, 0))
    l_spec = pl.BlockSpec((1, KV_HEADS, TM, HEAD_DIM), lambda b, i: (b, 0, i, 0))
    x_spec = pl.BlockSpec((1, TM, D_MODEL), lambda b, i: (b, i, 0))
    return pl.pallas_call(
        _merge_out_kernel,
        grid=(bsz, seq // TM),
        in_specs=[o_spec] * 3 + [l_spec] * 3 + [x_spec, x_spec,
                                                _const_spec((D_MODEL, D_MODEL)),
                                                _const_spec((1, D_MODEL))],
        out_specs=x_spec,
        out_shape=jax.ShapeDtypeStruct((bsz, seq, D_MODEL), F32),
        compiler_params=_params(("arbitrary", "arbitrary")),
        name="merge_out_prompt",
    )(*os_, *lses, z, x, w_out, g_post)


def _proj_sample_kernel(x_ref, g_ref, w_ref, cos_ref, sin_ref, y_ref, *, rope_heads):
    h = _rms(x_ref[...], g_ref[...]).astype(BF16)
    y = _dot(h, w_ref[...])
    cos = cos_ref[...]
    sin = sin_ref[...]
    for hd in range(rope_heads):
        cs = slice(hd * HEAD_DIM, (hd + 1) * HEAD_DIM)
        y_ref[:, cs] = _rope(y[:, cs], cos, sin)
    rest = rope_heads * HEAD_DIM
    y_ref[:, rest:] = y[:, rest:]


def _proj_sample(x, g, w, cos, sin, rope_heads):
    n = x.shape[0]
    return pl.pallas_call(
        functools.partial(_proj_sample_kernel, rope_heads=rope_heads),
        out_shape=jax.ShapeDtypeStruct((n, w.shape[1]), F32),
        compiler_params=_params(),
        name=f"proj_sample_{rope_heads}",
    )(x, g, w, cos, sin)


def _attn_sample_kernel(qz_ref, kvn_ref, c0_ref, c1_ref, c2_ref, y_ref, *out_cache_refs):
    c_refs = (c0_ref, c1_ref, c2_ref)
    qz = qz_ref[0]
    kvn = kvn_ref[0]
    pieces = []
    for kh in range(KV_HEADS):
        scores, s_new, vals, v_new = [], [], [], []
        for g in range(N_GROUPS):
            stride = 2 * KV_HEADS * DILATIONS[g]
            c = (g * HEADS + kh * Q_PER_KV) * HEAD_DIM
            q4 = jnp.concatenate(
                [qz[:, c + j * HEAD_DIM:c + (j + 1) * HEAD_DIM] for j in range(Q_PER_KV)], axis=0)
            q8 = jnp.concatenate([q4, q4], axis=0)
            k_rows = c_refs[g][0, pl.ds(kh, BAND, stride=stride), :]
            v_rows = c_refs[g][0, pl.ds(KV_HEADS + kh, BAND, stride=stride), :]
            kc = (g * KV_HEADS + kh) * HEAD_DIM
            k_n = kvn[:, kc:kc + HEAD_DIM]
            v_n = kvn[:, KV_HALF + kc:KV_HALF + kc + HEAD_DIM]
            scores.append(_dot_nt(q8.astype(BF16), k_rows.astype(BF16)))
            s_new.append(jnp.sum(q8 * k_n, axis=-1, keepdims=True))
            vals.append(v_rows.astype(BF16))
            v_new.append(v_n)
        m = functools.reduce(jnp.maximum,
                             [jnp.max(s, axis=-1, keepdims=True) for s in scores] + s_new)
        ps = [jnp.exp(s - m) for s in scores]
        pn = [jnp.exp(s - m) for s in s_new]
        l = sum(jnp.sum(p, axis=-1, keepdims=True) for p in ps) + sum(pn)
        o = sum(_dot(ps[g].astype(BF16), vals[g]) + pn[g] * v_new[g] for g in range(N_GROUPS)) / l
        pieces += [o[j:j + 1, :] for j in range(Q_PER_KV)]
    o_row = jnp.concatenate(pieces, axis=-1)
    y_ref[0] = o_row * _silu(qz[:, Q_WIDTH:])
    for g, oc_ref in enumerate(out_cache_refs):
        nr = 2 * KV_HEADS
        keep = (WINDOWS[g] - 1) * nr
        oc_ref[0, 0:keep, :] = c_refs[g][0, nr:nr + keep, :]
        new_rows = [kvn[:, half + (g * KV_HEADS + hh) * HEAD_DIM:half + (g * KV_HEADS + hh + 1) * HEAD_DIM]
                    for half in (0, KV_HALF) for hh in range(KV_HEADS)]
        oc_ref[0, keep:keep + nr, :] = jnp.concatenate(new_rows, axis=0)


def _attn_sample(qz, kv_new, caches, write_caches):
    n = qz.shape[0]
    cache_specs = [pl.BlockSpec((1, c.shape[1], HEAD_DIM), lambda b: (b, 0, 0)) for c in caches]
    out_specs = [pl.BlockSpec((1, 1, D_MODEL), lambda b: (b, 0, 0))]
    out_shape = [jax.ShapeDtypeStruct((n, 1, D_MODEL), F32)]
    if write_caches:
        out_specs += cache_specs
        out_shape += [jax.ShapeDtypeStruct(c.shape, F32) for c in caches]
    return pl.pallas_call(
        _attn_sample_kernel,
        grid=(n,),
        in_specs=[
            pl.BlockSpec((1, 1, qz.shape[-1]), lambda b: (b, 0, 0)),
            pl.BlockSpec((1, 1, kv_new.shape[-1]), lambda b: (b, 0, 0)),
        ] + cache_specs,
        out_specs=out_specs,
        out_shape=out_shape,
        compiler_params=_params(("arbitrary",)),
        name="attn_sample_w" if write_caches else "attn_sample",
    )(qz, kv_new, *caches)


def _out_sample_kernel(y_ref, x_ref, wout_ref, gpost_ref, o_ref):
    out = _dot(y_ref[...].astype(BF16), wout_ref[...])
    o_ref[...] = x_ref[...] + _rms(out, gpost_ref[...])


def _out_sample(y, x, w_out, g_post):
    return pl.pallas_call(
        _out_sample_kernel,
        out_shape=jax.ShapeDtypeStruct(x.shape, F32),
        compiler_params=_params(),
        name="out_sample",
    )(y, x, w_out, g_post)


def _rope_tables(pos):
    half = HEAD_DIM // 2
    inv_freq = ROPE_THETA ** (-jnp.arange(half, dtype=F32) * 2.0 / HEAD_DIM)
    ang = pos.astype(F32)[:, None] * inv_freq[None, :]
    cos = jnp.cos(ang)
    sin = jnp.sin(ang)
    return jnp.concatenate([cos, cos], axis=-1), jnp.concatenate([-sin, sin], axis=-1)


def kernel(x_prompt, x_sample, state_conv, cache_kv_g0, cache_kv_g1, cache_kv_g2,
           a_norm_pre, a_w_in, a_conv, a_w_out, a_norm_post,
           kv_norm, w_kv, b_norm_pre, b_w_in, b_w_out, b_norm_post):
    bsz, seq, _ = x_prompt.shape
    n_s = x_sample.shape[0]
    n_a = a_w_in.shape[0]
    n_b = b_w_in.shape[0]
    assert x_sample.shape[1] == 1 and seq % TA == 0

    cos_p, sin_p = _rope_tables(jnp.arange(seq))
    cos_s, sin_s = _rope_tables(PAST_LEN + jnp.arange(1))
    row = lambda v: v.reshape(1, -1)

    a_w_in16 = a_w_in.astype(BF16)
    a_w_out16 = a_w_out.astype(BF16)
    w_kv16 = w_kv.astype(BF16)
    b_w_in16 = b_w_in.astype(BF16)
    b_w_out16 = b_w_out.astype(BF16)

    hp = x_prompt
    hs = x_sample.reshape(n_s, D_MODEL)
    conv_p, conv_s = [], []
    for layer in range(n_a):
        args = (row(a_norm_pre[layer]), a_w_in16[layer], a_conv[layer], a_w_out16[layer],
                row(a_norm_post[layer]))
        hp, st_p = _conv_layer_prompt(hp, *args)
        past0 = state_conv[layer, :, 0]
        past1 = state_conv[layer, :, 1]
        hs, u_s = _conv_layer_sample(hs, past0, past1, *args)
        conv_p.append(st_p)
        conv_s.append(jnp.stack([past1, u_s], axis=1))

    *kv_p, c0, c1, c2 = _kv_proj_prompt(hp, row(kv_norm), w_kv16, cos_p, sin_p)
    bias = _band_bias()
    kv_new_p = [c.reshape(bsz, c.shape[1], 2, KV_HEADS, HEAD_DIM) for c in (c0, c1, c2)]
    kv_s = _proj_sample(hs, row(kv_norm), w_kv16, cos_s, sin_s, N_GROUPS * KV_HEADS)

    caches = [c.reshape(n_s, c.shape[1] * 2 * KV_HEADS, HEAD_DIM)
              for c in (cache_kv_g0, cache_kv_g1, cache_kv_g2)]
    kv_new_s = None
    for j in range(n_b):
        g_pre = row(b_norm_pre[j])
        g_post = row(b_norm_post[j])
        *q, z = _q_proj_prompt(hp, g_pre, b_w_in16[j], cos_p * (SCALE * LOG2E),
                               sin_p * (SCALE * LOG2E))
        res = [_attn_prompt(q[g], kv_p[g], bias, DILATIONS[g]) for g in range(N_GROUPS)]
        hp = _merge_out_prompt([r[0] for r in res], [r[1] for r in res], z, hp,
                               b_w_out16[j], g_post)
        qz = _proj_sample(hs, g_pre, b_w_in16[j], cos_s * SCALE, sin_s * SCALE, N_GROUPS * HEADS)
        outs = _attn_sample(qz.reshape(n_s, 1, -1), kv_s.reshape(n_s, 1, -1), caches,
                            write_caches=(j == 0))
        if j == 0:
            kv_new_s = [o.reshape(n_s, WINDOWS[g], 2, KV_HEADS, HEAD_DIM)
                        for g, o in enumerate(outs[1:])]
        hs = _out_sample(outs[0].reshape(n_s, D_MODEL), hs, b_w_out16[j], g_post)

    return (hp, hs.reshape(n_s, 1, D_MODEL), jnp.stack(conv_p, axis=0),
            kv_new_p[0], kv_new_p[1], kv_new_p[2],
            jnp.stack(conv_s, axis=0), kv_new_s[0], kv_new_s[1], kv_new_s[2])
```

```python
import functools

import jax
import jax.numpy as jnp
from jax import lax
from jax.experimental import pallas as pl
from jax.experimental.pallas import tpu as pltpu

D_MODEL = 1024
HEAD_DIM = 128
N_GROUPS = 3
WINDOWS = (128, 512, 2048)
DILATIONS = (1, 4, 16)
HEADS = 8
KV_HEADS = 2
Q_PER_KV = HEADS // KV_HEADS
Q_WIDTH = N_GROUPS * HEADS * HEAD_DIM
KV_HALF = N_GROUPS * KV_HEADS * HEAD_DIM
PAST_LEN = 8192
ROPE_THETA = 10000.0
EPS = 1e-6
SCALE = HEAD_DIM ** -0.5
NEG = -1e30
LOG2E = 1.4426950408889634
LN2 = 0.6931471805599453

BAND = 128
TM = 512
TA = 2048
CH = 256
VMEM_LIMIT = 52 * 1024 * 1024

F32 = jnp.float32
BF16 = jnp.bfloat16


def _dot(a, b):
    return jnp.dot(a, b, preferred_element_type=F32)


def _dot_nt(a, b):
    return lax.dot_general(a, b, (((1,), (1,)), ((), ())), preferred_element_type=F32)


def _rms(x, g):
    ms = jnp.mean(x * x, axis=-1, keepdims=True)
    return x * lax.rsqrt(ms + EPS) * g


def _silu(z):
    return z * jax.nn.sigmoid(z)


def _rope(x, cos, sin_signed):
    return x * cos + pltpu.roll(x, HEAD_DIM // 2, 1) * sin_signed


def _params(sem=None):
    return pltpu.CompilerParams(dimension_semantics=sem, vmem_limit_bytes=VMEM_LIMIT)


def _const_spec(shape):
    zeros = (0,) * len(shape)
    return pl.BlockSpec(shape, lambda *_: zeros, pipeline_mode=pl.Buffered(1))


def _conv_mix_chunk(h, win_ref, wconv_ref, j, u_prev1, u_prev2):
    cs = slice(j * CH, (j + 1) * CH)
    b_gate = _dot(h, win_ref[:, j * CH:(j + 1) * CH])
    c_gate = _dot(h, win_ref[:, D_MODEL + j * CH:D_MODEL + (j + 1) * CH])
    u_in = _dot(h, win_ref[:, 2 * D_MODEL + j * CH:2 * D_MODEL + (j + 1) * CH])
    z = _dot(h, win_ref[:, 3 * D_MODEL + j * CH:3 * D_MODEL + (j + 1) * CH])
    u = c_gate * u_in
    conv = (u_prev2(u, cs) * wconv_ref[0:1, cs] + u_prev1(u, cs) * wconv_ref[1:2, cs]
            + u * wconv_ref[2:3, cs])
    return u, b_gate * conv * _silu(z)


def _conv_prompt_kernel(x_ref, gpre_ref, win_ref, wconv_ref, wout_ref, gpost_ref,
                        y_ref, st_ref, tail_ref):
    @pl.when(pl.program_id(1) == 0)
    def _():
        tail_ref[...] = jnp.zeros_like(tail_ref)

    x = x_ref[0]
    h = _rms(x, gpre_ref[...]).astype(BF16)
    row = lax.broadcasted_iota(jnp.int32, (TM, CH), 0)

    def prev1(u, cs):
        return jnp.where(row == 0, tail_ref[1:2, cs], pltpu.roll(u, 1, 0))

    def prev2(u, cs):
        return jnp.where(row == 0, tail_ref[0:1, cs],
                         jnp.where(row == 1, tail_ref[1:2, cs], pltpu.roll(u, 2, 0)))

    acc = jnp.zeros((TM, D_MODEL), F32)
    for j in range(D_MODEL // CH):
        cs = slice(j * CH, (j + 1) * CH)
        u, y = _conv_mix_chunk(h, win_ref, wconv_ref, j, prev1, prev2)
        tail_ref[0:2, cs] = u[TM - 2:TM, :]
        acc = acc + _dot(y.astype(BF16), wout_ref[cs, :])
    y_ref[0] = x + _rms(acc, gpost_ref[...])
    st_ref[0] = tail_ref[0:2, :]


def _conv_layer_prompt(x, g_pre, w_in, w_conv, w_out, g_post):
    bsz, seq, _ = x.shape
    return pl.pallas_call(
        _conv_prompt_kernel,
        grid=(bsz, seq // TM),
        in_specs=[
            pl.BlockSpec((1, TM, D_MODEL), lambda b, i: (b, i, 0)),
            _const_spec((1, D_MODEL)),
            _const_spec((D_MODEL, 4 * D_MODEL)),
            _const_spec((3, D_MODEL)),
            _const_spec((D_MODEL, D_MODEL)),
            _const_spec((1, D_MODEL)),
        ],
        out_specs=[
            pl.BlockSpec((1, TM, D_MODEL), lambda b, i: (b, i, 0)),
            pl.BlockSpec((1, 2, D_MODEL), lambda b, i: (b, 0, 0)),
        ],
        out_shape=[
            jax.ShapeDtypeStruct((bsz, seq, D_MODEL), F32),
            jax.ShapeDtypeStruct((bsz, 2, D_MODEL), F32),
        ],
        scratch_shapes=[pltpu.VMEM((8, D_MODEL), F32)],
        compiler_params=_params(("arbitrary", "arbitrary")),
        name="conv_layer_prompt",
    )(x, g_pre, w_in, w_conv, w_out, g_post)


def _conv_sample_kernel(x_ref, p0_ref, p1_ref, gpre_ref, win_ref, wconv_ref, wout_ref, gpost_ref,
                        y_ref, u_ref):
    x = x_ref[...]
    h = _rms(x, gpre_ref[...]).astype(BF16)
    acc = jnp.zeros(x.shape, F32)
    for j in range(D_MODEL // CH):
        cs = slice(j * CH, (j + 1) * CH)
        u, y = _conv_mix_chunk(h, win_ref, wconv_ref, j,
                               lambda u, cs: p1_ref[:, cs], lambda u, cs: p0_ref[:, cs])
        u_ref[:, cs] = u
        acc = acc + _dot(y.astype(BF16), wout_ref[cs, :])
    y_ref[...] = x + _rms(acc, gpost_ref[...])


def _conv_layer_sample(x, past0, past1, g_pre, w_in, w_conv, w_out, g_post):
    n = x.shape[0]
    return pl.pallas_call(
        _conv_sample_kernel,
        out_shape=[jax.ShapeDtypeStruct((n, D_MODEL), F32)] * 2,
        compiler_params=_params(),
        name="conv_layer_sample",
    )(x, past0, past1, g_pre, w_in, w_conv, w_out, g_post)


def _store_by_residue(slab_ref, val, out_ref, lead, dil, col, block_w):
    rows = val.shape[0]
    if dil == 1:
        out_ref[lead + (slice(None), slice(col, col + HEAD_DIM))] = val.astype(BF16)
        return
    slab_ref[0] = val
    if dil == 16:
        q = rows // 4
        for m in range(4):
            slab_ref[1, m * q:(m + 1) * q, :] = slab_ref[0, pl.ds(m, q, stride=4), :]
        pieces = {4 * m2 + m: slab_ref[1, pl.ds(m * q + m2, q // 4, stride=4), :]
                  for m in range(4) for m2 in range(4)}
    else:
        pieces = {r: slab_ref[0, pl.ds(r, rows // dil, stride=dil), :] for r in range(dil)}
    for r in range(dil):
        c = r * block_w + col
        out_ref[lead + (slice(None), slice(c, c + HEAD_DIM))] = pieces[r].astype(BF16)


def _kv_prompt_kernel(x_ref, g_ref, w_ref, cos_ref, sin_ref, kv0_ref, kv1_ref, kv2_ref,
                      c0_ref, c1_ref, c2_ref, slab_ref):
    h = _rms(x_ref[0], g_ref[...]).astype(BF16)
    cos = cos_ref[...]
    sin = sin_ref[...]
    kv = _dot(h, w_ref[...])
    kv_refs = (kv0_ref, kv1_ref, kv2_ref)
    caches = (c0_ref, c1_ref, c2_ref)
    for g in range(N_GROUPS):
        parts_k, parts_v = [], []
        for hh in range(KV_HEADS):
            c = (g * KV_HEADS + hh) * HEAD_DIM
            k = _rope(kv[:, c:c + HEAD_DIM], cos, sin)
            v = kv[:, KV_HALF + c:KV_HALF + c + HEAD_DIM]
            slot = max(2 * ((g - 1) * KV_HEADS + hh), 0)
            _store_by_residue(slab_ref.at[slot], k, kv_refs[g], (0, hh), DILATIONS[g], 0,
                              2 * HEAD_DIM)
            _store_by_residue(slab_ref.at[slot + 1], v, kv_refs[g], (0, hh), DILATIONS[g], HEAD_DIM,
                              2 * HEAD_DIM)
            parts_k.append(k)
            parts_v.append(v)
        rows = min(TM, WINDOWS[g])
        full = jnp.concatenate(parts_k + parts_v, axis=-1)
        caches[g][0] = full[TM - rows:, :]


def _kv_proj_prompt(x, g, w_kv, cos, sin):
    bsz, seq, _ = x.shape
    nt = seq // TM
    cache_specs, cache_shapes = [], []
    for grp in range(N_GROUPS):
        w = min(WINDOWS[grp], seq)
        rows = min(TM, w)
        first = nt - w // rows
        cache_specs.append(pl.BlockSpec(
            (1, rows, 4 * HEAD_DIM), lambda b, i, first=first: (b, jnp.maximum(i - first, 0), 0)))
        cache_shapes.append(jax.ShapeDtypeStruct((bsz, w, 4 * HEAD_DIM), F32))
    kw = 2 * HEAD_DIM
    kv_specs = [pl.BlockSpec((1, KV_HEADS, TM // d, d * kw), lambda b, i: (b, 0, i, 0))
                for d in DILATIONS]
    kv_shapes = [jax.ShapeDtypeStruct((bsz, KV_HEADS, seq // d, d * kw), BF16) for d in DILATIONS]
    return pl.pallas_call(
        _kv_prompt_kernel,
        grid=(bsz, nt),
        in_specs=[
            pl.BlockSpec((1, TM, D_MODEL), lambda b, i: (b, i, 0)),
            _const_spec((1, D_MODEL)),
            _const_spec((D_MODEL, 2 * KV_HALF)),
            pl.BlockSpec((TM, HEAD_DIM), lambda b, i: (i, 0)),
            pl.BlockSpec((TM, HEAD_DIM), lambda b, i: (i, 0)),
        ],
        out_specs=kv_specs + cache_specs,
        out_shape=kv_shapes + cache_shapes,
        scratch_shapes=[pltpu.VMEM((2 * (N_GROUPS - 1) * KV_HEADS, 2, TM, HEAD_DIM), F32)],
        compiler_params=_params(("arbitrary", "arbitrary")),
        name="kv_proj_prompt",
    )(x, g, w_kv, cos, sin)


def _q_prompt_kernel(x_ref, g_ref, w_ref, cos_ref, sin_ref, q0_ref, q1_ref, q2_ref, z_ref,
                     slab_ref):
    h = _rms(x_ref[0], g_ref[...]).astype(BF16)
    cos = cos_ref[...]
    sin = sin_ref[...]
    width = Q_PER_KV * HEAD_DIM
    q_refs = (q0_ref, q1_ref, q2_ref)
    for g in range(N_GROUPS):
        for kh in range(KV_HEADS):
            c = (g * KV_HEADS + kh) * width
            qq = _dot(h, w_ref[:, c:c + width])
            for j in range(Q_PER_KV):
                head = _rope(qq[:, j * HEAD_DIM:(j + 1) * HEAD_DIM], cos, sin)
                slot = max(((g - 1) * KV_HEADS + kh) * Q_PER_KV + j, 0)
                _store_by_residue(slab_ref.at[slot], head, q_refs[g], (0, kh), DILATIONS[g],
                                  j * HEAD_DIM, width)
    z_ref[0] = _dot(h, w_ref[:, Q_WIDTH:])


def _q_proj_prompt(x, g, w_in, cos_q, sin_q):
    bsz, seq, _ = x.shape
    width = Q_PER_KV * HEAD_DIM
    q_specs = [pl.BlockSpec((1, KV_HEADS, TM // d, d * width), lambda b, i: (b, 0, i, 0))
               for d in DILATIONS]
    q_shapes = [jax.ShapeDtypeStruct((bsz, KV_HEADS, seq // d, d * width), BF16) for d in DILATIONS]
    return pl.pallas_call(
        _q_prompt_kernel,
        grid=(bsz, seq // TM),
        in_specs=[
            pl.BlockSpec((1, TM, D_MODEL), lambda b, i: (b, i, 0)),
            _const_spec((1, D_MODEL)),
            _const_spec((D_MODEL, Q_WIDTH + D_MODEL)),
            pl.BlockSpec((TM, HEAD_DIM), lambda b, i: (i, 0)),
            pl.BlockSpec((TM, HEAD_DIM), lambda b, i: (i, 0)),
        ],
        out_specs=q_specs + [pl.BlockSpec((1, TM, D_MODEL), lambda b, i: (b, i, 0))],
        out_shape=q_shapes + [jax.ShapeDtypeStruct((bsz, seq, D_MODEL), F32)],
        scratch_shapes=[pltpu.VMEM(((N_GROUPS - 1) * HEADS, 2, TM, HEAD_DIM), F32)],
        compiler_params=_params(("arbitrary", "arbitrary")),
        name="q_proj_prompt",
    )(x, g, w_in, cos_q, sin_q)


def _attn_prompt_kernel(q_ref, kvc_ref, kvp_ref, bias_ref, o_ref, lse_ref, bias0_ref, *stage,
                        dil):
    nrow = TA // dil
    nblk = nrow // BAND
    qw = Q_PER_KV * HEAD_DIM
    kw = 2 * HEAD_DIM
    bias0_ref[...] = jnp.where(pl.program_id(2) == 0, bias_ref[1], bias_ref[0])
    ones = jnp.ones((2 * BAND, HEAD_DIM), BF16)
    lane = lax.broadcasted_iota(jnp.int32, (BAND, HEAD_DIM), 1)
    for r in range(dil):
        kcol = slice(r * kw, r * kw + HEAD_DIM)
        vcol = slice(r * kw + HEAD_DIM, (r + 1) * kw)
        for n in range(nblk):
            rows = slice(n * BAND, (n + 1) * BAND)
            q4 = jnp.concatenate(
                [q_ref[0, 0, rows, r * qw + j * HEAD_DIM:r * qw + (j + 1) * HEAD_DIM]
                 for j in range(Q_PER_KV)], axis=0)
            if n == 0:
                prow = slice(nrow - BAND, nrow)
                k_cat = jnp.concatenate([kvp_ref[0, 0, prow, kcol], kvc_ref[0, 0, rows, kcol]], axis=0)
                v_cat = jnp.concatenate([kvp_ref[0, 0, prow, vcol], kvc_ref[0, 0, rows, vcol]], axis=0)
                bias = bias0_ref[...]
            else:
                both = slice((n - 1) * BAND, (n + 1) * BAND)
                k_cat = kvc_ref[0, 0, both, kcol]
                v_cat = kvc_ref[0, 0, both, vcol]
                bias = bias_ref[0]
            s = _dot_nt(q4, k_cat) + bias
            m = jnp.max(s, axis=-1, keepdims=True)
            p = jnp.exp2(s - m).astype(BF16)
            ov = _dot(p, jnp.concatenate([v_cat, ones], axis=1))
            l = ov[:, HEAD_DIM:]
            o = ov[:, :HEAD_DIM] / l
            lse = m * LN2 + jnp.log(l)
            lse_tile = jnp.zeros((BAND, HEAD_DIM), F32)
            for j in range(Q_PER_KV):
                lse_tile = jnp.where(lane == j, lse[j * BAND:(j + 1) * BAND], lse_tile)
            if dil == 1:
                for j in range(Q_PER_KV):
                    o_ref[0, 0, rows, j * HEAD_DIM:(j + 1) * HEAD_DIM] = (
                        o[j * BAND:(j + 1) * BAND].astype(BF16))
                lse_ref[0, 0, rows, :] = lse_tile
            else:
                so_ref, sl_ref = stage
                trows = pl.ds(n * BAND * dil + r, BAND, stride=dil)
                for j in range(Q_PER_KV):
                    so_ref[j, trows, :] = o[j * BAND:(j + 1) * BAND]
                sl_ref[trows, :] = lse_tile
    if dil > 1:
        so_ref, sl_ref = stage
        for j in range(Q_PER_KV):
            o_ref[0, 0, :, j * HEAD_DIM:(j + 1) * HEAD_DIM] = so_ref[j].astype(BF16)
        lse_ref[0, 0] = sl_ref[...]


def _band_bias():
    qi = lax.broadcasted_iota(jnp.int32, (Q_PER_KV * BAND, 2 * BAND), 0) % BAND
    jj = lax.broadcasted_iota(jnp.int32, (Q_PER_KV * BAND, 2 * BAND), 1)
    ok_prev = (jj < BAND) & (jj >= qi)
    ok_cur = (jj >= BAND) & (jj - BAND <= qi)
    neg = jnp.full(qi.shape, NEG, F32)
    return jnp.stack([jnp.where(ok_prev | ok_cur, 0.0, neg), jnp.where(ok_cur, 0.0, neg)])


def _attn_prompt(q_g, kv_g, bias, dil):
    bsz, _, nv, qwd = q_g.shape
    seq = nv * dil
    qw = qwd // dil
    kw = kv_g.shape[-1] // dil
    rows = TA // dil
    scratch = [pltpu.VMEM((Q_PER_KV * BAND, 2 * BAND), F32)]
    if dil > 1:
        scratch += [pltpu.VMEM((Q_PER_KV, TA, HEAD_DIM), F32), pltpu.VMEM((TA, HEAD_DIM), F32)]
    return pl.pallas_call(
        functools.partial(_attn_prompt_kernel, dil=dil),
        grid=(bsz, KV_HEADS, seq // TA),
        in_specs=[
            pl.BlockSpec((1, 1, rows, dil * qw), lambda b, h, i: (b, h, i, 0)),
            pl.BlockSpec((1, 1, rows, dil * kw), lambda b, h, i: (b, h, i, 0)),
            pl.BlockSpec((1, 1, rows, dil * kw), lambda b, h, i: (b, h, jnp.maximum(i - 1, 0), 0)),
            _const_spec(bias.shape),
        ],
        out_specs=[
            pl.BlockSpec((1, 1, TA, qw), lambda b, h, i: (b, h, i, 0)),
            pl.BlockSpec((1, 1, TA, HEAD_DIM), lambda b, h, i: (b, h, i, 0)),
        ],
        out_shape=[
            jax.ShapeDtypeStruct((bsz, KV_HEADS, seq, qw), BF16),
            jax.ShapeDtypeStruct((bsz, KV_HEADS, seq, HEAD_DIM), F32),
        ],
        scratch_shapes=scratch,
        compiler_params=_params(("arbitrary", "arbitrary", "arbitrary")),
        name=f"attn_prompt_d{dil}",
    )(q_g, kv_g, kv_g, bias)


def _merge_out_kernel(o0_ref, o1_ref, o2_ref, l0_ref, l1_ref, l2_ref, z_ref, x_ref,
                      wout_ref, gpost_ref, y_ref):
    o_refs = (o0_ref, o1_ref, o2_ref)
    l_refs = (l0_ref, l1_ref, l2_ref)
    z = z_ref[0]
    pieces = []
    for kh in range(KV_HEADS):
        for j in range(Q_PER_KV):
            hcols = slice(j * HEAD_DIM, (j + 1) * HEAD_DIM)
            lses = [l_refs[g][0, kh, :, j:j + 1] for g in range(N_GROUPS)]
            m = jnp.maximum(jnp.maximum(lses[0], lses[1]), lses[2])
            es = [jnp.exp(v - m) for v in lses]
            inv = 1.0 / (es[0] + es[1] + es[2])
            o = sum((es[g] * inv) * o_refs[g][0, kh, :, hcols].astype(F32)
                    for g in range(N_GROUPS))
            head = kh * Q_PER_KV + j
            gate = _silu(z[:, head * HEAD_DIM:(head + 1) * HEAD_DIM])
            pieces.append((o * gate).astype(BF16))
    y = jnp.concatenate(pieces, axis=-1)
    out = _dot(y, wout_ref[...])
    y_ref[0] = x_ref[0] + _rms(out, gpost_ref[...])


def _merge_out_prompt(os_, lses, z, x, w_out, g_post):
    bsz, seq, _ = x.shape
    qw = Q_PER_KV * HEAD_DIM
    o_spec = pl.BlockSpec((1, KV_HEADS, TM, qw), lambda b, i: (b, 0, i, 0))
    l_spec = pl.BlockSpec((1, KV_HEADS, TM, HEAD_DIM), lambda b, i: (b, 0, i, 0))
    x_spec = pl.BlockSpec((1, TM, D_MODEL), lambda b, i: (b, i, 0))
    return pl.pallas_call(
        _merge_out_kernel,
        grid=(bsz, seq // TM),
        in_specs=[o_spec] * 3 + [l_spec] * 3 + [x_spec, x_spec,
                                                _const_spec((D_MODEL, D_MODEL)),
                                                _const_spec((1, D_MODEL))],
        out_specs=x_spec,
        out_shape=jax.ShapeDtypeStruct((bsz, seq, D_MODEL), F32),
        compiler_params=_params(("arbitrary", "arbitrary")),
        name="merge_out_prompt",
    )(*os_, *lses, z, x, w_out, g_post)


def _proj_sample_kernel(x_ref, g_ref, w_ref, cos_ref, sin_ref, y_ref, *, rope_heads):
    h = _rms(x_ref[...], g_ref[...]).astype(BF16)
    y = _dot(h, w_ref[...])
    cos = cos_ref[...]
    sin = sin_ref[...]
    for hd in range(rope_heads):
        cs = slice(hd * HEAD_DIM, (hd + 1) * HEAD_DIM)
        y_ref[:, cs] = _rope(y[:, cs], cos, sin)
    rest = rope_heads * HEAD_DIM
    y_ref[:, rest:] = y[:, rest:]


def _proj_sample(x, g, w, cos, sin, rope_heads):
    n = x.shape[0]
    return pl.pallas_call(
        functools.partial(_proj_sample_kernel, rope_heads=rope_heads),
        out_shape=jax.ShapeDtypeStruct((n, w.shape[1]), F32),
        compiler_params=_params(),
        name=f"proj_sample_{rope_heads}",
    )(x, g, w, cos, sin)


def _attn_sample_body(qz, kvn, window_kv, y_ref):
    pieces = []
    for kh in range(KV_HEADS):
        scores, s_new, vals, v_new = [], [], [], []
        for g in range(N_GROUPS):
            c = (g * HEADS + kh * Q_PER_KV) * HEAD_DIM
            q4 = jnp.concatenate(
                [qz[:, c + j * HEAD_DIM:c + (j + 1) * HEAD_DIM] for j in range(Q_PER_KV)], axis=0)
            q8 = jnp.concatenate([q4, q4], axis=0)
            k_rows, v_rows = window_kv(g, kh)
            kc = (g * KV_HEADS + kh) * HEAD_DIM
            k_n = kvn[:, kc:kc + HEAD_DIM]
            v_n = kvn[:, KV_HALF + kc:KV_HALF + kc + HEAD_DIM]
            scores.append(_dot_nt(q8.astype(BF16), k_rows))
            s_new.append(jnp.sum(q8 * k_n, axis=-1, keepdims=True))
            vals.append(v_rows)
            v_new.append(v_n)
        m = functools.reduce(jnp.maximum,
                             [jnp.max(s, axis=-1, keepdims=True) for s in scores] + s_new)
        ps = [jnp.exp(s - m) for s in scores]
        pn = [jnp.exp(s - m) for s in s_new]
        l = sum(jnp.sum(p, axis=-1, keepdims=True) for p in ps) + sum(pn)
        o = sum(_dot(ps[g].astype(BF16), vals[g]) + pn[g] * v_new[g] for g in range(N_GROUPS)) / l
        pieces += [o[j:j + 1, :] for j in range(Q_PER_KV)]
    o_row = jnp.concatenate(pieces, axis=-1)
    y_ref[0] = o_row * _silu(qz[:, Q_WIDTH:])


def _attn_sample_first_kernel(qz_ref, kvn_ref, c0_ref, c1_ref, c2_ref,
                              y_ref, oc0_ref, oc1_ref, oc2_ref, win_ref):
    c_refs = (c0_ref, c1_ref, c2_ref)
    kvn = kvn_ref[0]
    nr = 2 * KV_HEADS

    def window_kv(g, kh):
        stride = nr * DILATIONS[g]
        k_rows = c_refs[g][0, pl.ds(kh, BAND, stride=stride), :].astype(BF16)
        v_rows = c_refs[g][0, pl.ds(KV_HEADS + kh, BAND, stride=stride), :].astype(BF16)
        slot = 2 * (g * KV_HEADS + kh)
        win_ref[0, slot] = k_rows
        win_ref[0, slot + 1] = v_rows
        return k_rows, v_rows

    _attn_sample_body(qz_ref[0], kvn, window_kv, y_ref)
    for g, oc_ref in enumerate((oc0_ref, oc1_ref, oc2_ref)):
        keep = (WINDOWS[g] - 1) * nr
        oc_ref[0, 0:keep, :] = c_refs[g][0, nr:nr + keep, :]
        new_rows = [kvn[:, half + (g * KV_HEADS + hh) * HEAD_DIM:half + (g * KV_HEADS + hh + 1) * HEAD_DIM]
                    for half in (0, KV_HALF) for hh in range(KV_HEADS)]
        oc_ref[0, keep:keep + nr, :] = jnp.concatenate(new_rows, axis=0)


def _attn_sample_next_kernel(qz_ref, kvn_ref, win_ref, y_ref):
    def window_kv(g, kh):
        slot = 2 * (g * KV_HEADS + kh)
        return win_ref[0, slot], win_ref[0, slot + 1]

    _attn_sample_body(qz_ref[0], kvn_ref[0], window_kv, y_ref)


def _attn_sample_first(qz, kv_new, caches):
    n = qz.shape[0]
    row_spec = lambda a: pl.BlockSpec((1, 1, a.shape[-1]), lambda b: (b, 0, 0))
    cache_specs = [pl.BlockSpec((1, c.shape[1], HEAD_DIM), lambda b: (b, 0, 0)) for c in caches]
    win_shape = (n, 2 * N_GROUPS * KV_HEADS, BAND, HEAD_DIM)
    return pl.pallas_call(
        _attn_sample_first_kernel,
        grid=(n,),
        in_specs=[row_spec(qz), row_spec(kv_new)] + cache_specs,
        out_specs=[pl.BlockSpec((1, 1, D_MODEL), lambda b: (b, 0, 0))] + cache_specs
        + [pl.BlockSpec((1,) + win_shape[1:], lambda b: (b, 0, 0, 0))],
        out_shape=[jax.ShapeDtypeStruct((n, 1, D_MODEL), F32)]
        + [jax.ShapeDtypeStruct(c.shape, F32) for c in caches]
        + [jax.ShapeDtypeStruct(win_shape, BF16)],
        compiler_params=_params(("arbitrary",)),
        name="attn_sample_first",
    )(qz, kv_new, *caches)


def _attn_sample_next(qz, kv_new, win):
    n = qz.shape[0]
    row_spec = lambda a: pl.BlockSpec((1, 1, a.shape[-1]), lambda b: (b, 0, 0))
    return pl.pallas_call(
        _attn_sample_next_kernel,
        grid=(n,),
        in_specs=[row_spec(qz), row_spec(kv_new),
                  pl.BlockSpec((1,) + win.shape[1:], lambda b: (b, 0, 0, 0))],
        out_specs=pl.BlockSpec((1, 1, D_MODEL), lambda b: (b, 0, 0)),
        out_shape=jax.ShapeDtypeStruct((n, 1, D_MODEL), F32),
        compiler_params=_params(("arbitrary",)),
        name="attn_sample_next",
    )(qz, kv_new, win)


def _out_sample_kernel(y_ref, x_ref, wout_ref, gpost_ref, o_ref):
    out = _dot(y_ref[...].astype(BF16), wout_ref[...])
    o_ref[...] = x_ref[...] + _rms(out, gpost_ref[...])


def _out_sample(y, x, w_out, g_post):
    return pl.pallas_call(
        _out_sample_kernel,
        out_shape=jax.ShapeDtypeStruct(x.shape, F32),
        compiler_params=_params(),
        name="out_sample",
    )(y, x, w_out, g_post)


def _rope_tables(pos):
    half = HEAD_DIM // 2
    inv_freq = ROPE_THETA ** (-jnp.arange(half, dtype=F32) * 2.0 / HEAD_DIM)
    ang = pos.astype(F32)[:, None] * inv_freq[None, :]
    cos = jnp.cos(ang)
    sin = jnp.sin(ang)
    return jnp.concatenate([cos, cos], axis=-1), jnp.concatenate([-sin, sin], axis=-1)


def kernel(x_prompt, x_sample, state_conv, cache_kv_g0, cache_kv_g1, cache_kv_g2,
           a_norm_pre, a_w_in, a_conv, a_w_out, a_norm_post,
           kv_norm, w_kv, b_norm_pre, b_w_in, b_w_out, b_norm_post):
    bsz, seq, _ = x_prompt.shape
    n_s = x_sample.shape[0]
    n_a = a_w_in.shape[0]
    n_b = b_w_in.shape[0]
    assert x_sample.shape[1] == 1 and seq % TA == 0

    cos_p, sin_p = _rope_tables(jnp.arange(seq))
    cos_s, sin_s = _rope_tables(PAST_LEN + jnp.arange(1))
    row = lambda v: v.reshape(1, -1)

    a_w_in16 = a_w_in.astype(BF16)
    a_w_out16 = a_w_out.astype(BF16)
    w_kv16 = w_kv.astype(BF16)
    b_w_in16 = b_w_in.astype(BF16)
    b_w_out16 = b_w_out.astype(BF16)

    hp = x_prompt
    hs = x_sample.reshape(n_s, D_MODEL)
    conv_p, conv_s = [], []
    for layer in range(n_a):
        args = (row(a_norm_pre[layer]), a_w_in16[layer], a_conv[layer], a_w_out16[layer],
                row(a_norm_post[layer]))
        hp, st_p = _conv_layer_prompt(hp, *args)
        past0 = state_conv[layer, :, 0]
        past1 = state_conv[layer, :, 1]
        hs, u_s = _conv_layer_sample(hs, past0, past1, *args)
        conv_p.append(st_p)
        conv_s.append(jnp.stack([past1, u_s], axis=1))

    *kv_p, c0, c1, c2 = _kv_proj_prompt(hp, row(kv_norm), w_kv16, cos_p, sin_p)
    bias = _band_bias()
    kv_new_p = [c.reshape(bsz, c.shape[1], 2, KV_HEADS, HEAD_DIM) for c in (c0, c1, c2)]
    kv_s = _proj_sample(hs, row(kv_norm), w_kv16, cos_s, sin_s, N_GROUPS * KV_HEADS)

    caches = [c.reshape(n_s, c.shape[1] * 2 * KV_HEADS, HEAD_DIM)
              for c in (cache_kv_g0, cache_kv_g1, cache_kv_g2)]
    kv_new_s = None
    for j in range(n_b):
        g_pre = row(b_norm_pre[j])
        g_post = row(b_norm_post[j])
        *q, z = _q_proj_prompt(hp, g_pre, b_w_in16[j], cos_p * (SCALE * LOG2E),
                               sin_p * (SCALE * LOG2E))
        res = [_attn_prompt(q[g], kv_p[g], bias, DILATIONS[g]) for g in range(N_GROUPS)]
        hp = _merge_out_prompt([r[0] for r in res], [r[1] for r in res], z, hp,
                               b_w_out16[j], g_post)
        qz = _proj_sample(hs, g_pre, b_w_in16[j], cos_s * SCALE, sin_s * SCALE, N_GROUPS * HEADS)
        qz3 = qz.reshape(n_s, 1, -1)
        if j == 0:
            y_s, *shifted, win = _attn_sample_first(qz3, kv_s.reshape(n_s, 1, -1), caches)
            kv_new_s = [o.reshape(n_s, WINDOWS[g], 2, KV_HEADS, HEAD_DIM)
                        for g, o in enumerate(shifted)]
        else:
            y_s = _attn_sample_next(qz3, kv_s.reshape(n_s, 1, -1), win)
        hs = _out_sample(y_s.reshape(n_s, D_MODEL), hs, b_w_out16[j], g_post)

    return (hp, hs.reshape(n_s, 1, D_MODEL), jnp.stack(conv_p, axis=0),
            kv_new_p[0], kv_new_p[1], kv_new_p[2],
            jnp.stack(conv_s, axis=0), kv_new_s[0], kv_new_s[1], kv_new_s[2])
```

```python
import functools

import jax
import jax.numpy as jnp
from jax import lax
from jax.experimental import pallas as pl
from jax.experimental.pallas import tpu as pltpu

D_MODEL = 1024
HEAD_DIM = 128
N_GROUPS = 3
WINDOWS = (128, 512, 2048)
DILATIONS = (1, 4, 16)
HEADS = 8
KV_HEADS = 2
Q_PER_KV = HEADS // KV_HEADS
Q_WIDTH = N_GROUPS * HEADS * HEAD_DIM
KV_HALF = N_GROUPS * KV_HEADS * HEAD_DIM
PAST_LEN = 8192
ROPE_THETA = 10000.0
EPS = 1e-6
SCALE = HEAD_DIM ** -0.5
NEG = -1e30
LOG2E = 1.4426950408889634

BAND = 128
NRES = 16
TA = NRES * BAND
TM = 512
SUB = TM // NRES
CH = 256
QW = Q_PER_KV * HEAD_DIM
KW = 2 * HEAD_DIM
VMEM_LIMIT = 54 * 1024 * 1024

F32 = jnp.float32
BF16 = jnp.bfloat16
GROUP_DTYPES = (F32, BF16, BF16)


def _dot(a, b):
    return jnp.dot(a, b, preferred_element_type=F32)


def _dot_nt(a, b):
    return lax.dot_general(a, b, (((1,), (1,)), ((), ())), preferred_element_type=F32)


def _rms(x, g):
    ms = jnp.mean(x * x, axis=-1, keepdims=True)
    return x * lax.rsqrt(ms + EPS) * g


def _silu(z):
    return z * jax.nn.sigmoid(z)


def _rope(x, cos, sin_signed):
    return x * cos + pltpu.roll(x, HEAD_DIM // 2, 1) * sin_signed


def _params(sem=None):
    return pltpu.CompilerParams(dimension_semantics=sem, vmem_limit_bytes=VMEM_LIMIT)


def _const_spec(shape):
    zeros = (0,) * len(shape)
    return pl.BlockSpec(shape, lambda *_: zeros, pipeline_mode=pl.Buffered(1))


def _split_residues(slab_ref, val):
    q = TM // 4
    slab_ref[0] = val
    for m in range(4):
        slab_ref[1, m * q:(m + 1) * q, :] = slab_ref[0, pl.ds(m, q, stride=4), :]
    return {4 * m2 + m: slab_ref[1, pl.ds(m * q + m2, SUB, stride=4), :]
            for m in range(4) for m2 in range(4)}


def _merge_residues(slab_ref, pieces):
    q = TM // 4
    for m in range(4):
        for m2 in range(4):
            slab_ref[1, pl.ds(m * q + m2, SUB, stride=4), :] = pieces[4 * m2 + m]
    for m in range(4):
        slab_ref[0, pl.ds(m, q, stride=4), :] = slab_ref[1, m * q:(m + 1) * q, :]
    return slab_ref[0]


def _to_residue_major(slab_ref, val):
    ncol = val.shape[1] // HEAD_DIM
    cols = [_split_residues(slab_ref.at[c], val[:, c * HEAD_DIM:(c + 1) * HEAD_DIM])
            for c in range(ncol)]
    return [jnp.concatenate([cols[c][r] for c in range(ncol)], axis=-1) for r in range(NRES)]


def _to_token_order(slab_ref, val):
    ncol = val.shape[1] // HEAD_DIM
    cols = [_merge_residues(slab_ref.at[c],
                            {r: val[r * SUB:(r + 1) * SUB, c * HEAD_DIM:(c + 1) * HEAD_DIM]
                             for r in range(NRES)})
            for c in range(ncol)]
    return jnp.concatenate(cols, axis=-1)


def _set_spec(width):
    per = TA // TM
    return pl.BlockSpec((1, 1, NRES, SUB, width), lambda b, i: (b, i // per, 0, i % per, 0))


def _load_set(ref):
    return jnp.concatenate([ref[0, 0, r] for r in range(NRES)], axis=0)


def _conv_mix_chunk(h, win_ref, wconv_ref, j, u_prev1, u_prev2):
    cs = slice(j * CH, (j + 1) * CH)
    b_gate = _dot(h, win_ref[:, j * CH:(j + 1) * CH])
    c_gate = _dot(h, win_ref[:, D_MODEL + j * CH:D_MODEL + (j + 1) * CH])
    u_in = _dot(h, win_ref[:, 2 * D_MODEL + j * CH:2 * D_MODEL + (j + 1) * CH])
    z = _dot(h, win_ref[:, 3 * D_MODEL + j * CH:3 * D_MODEL + (j + 1) * CH])
    u = c_gate * u_in
    conv = (u_prev2(u, cs) * wconv_ref[0:1, cs] + u_prev1(u, cs) * wconv_ref[1:2, cs]
            + u * wconv_ref[2:3, cs])
    return u, b_gate * conv * _silu(z)


def _conv_prompt_kernel(x_ref, gpre_ref, win_ref, wconv_ref, wout_ref, gpost_ref,
                        y_ref, st_ref, tail_ref, *slab, residue_major_out):
    @pl.when(pl.program_id(1) == 0)
    def _():
        tail_ref[...] = jnp.zeros_like(tail_ref)

    x = x_ref[0]
    h = _rms(x, gpre_ref[...]).astype(BF16)
    row = lax.broadcasted_iota(jnp.int32, (TM, CH), 0)

    def prev1(u, cs):
        return jnp.where(row == 0, tail_ref[1:2, cs], pltpu.roll(u, 1, 0))

    def prev2(u, cs):
        return jnp.where(row == 0, tail_ref[0:1, cs],
                         jnp.where(row == 1, tail_ref[1:2, cs], pltpu.roll(u, 2, 0)))

    acc = jnp.zeros((TM, D_MODEL), F32)
    for j in range(D_MODEL // CH):
        cs = slice(j * CH, (j + 1) * CH)
        u, y = _conv_mix_chunk(h, win_ref, wconv_ref, j, prev1, prev2)
        tail_ref[0:2, cs] = u[TM - 2:TM, :]
        acc = acc + _dot(y.astype(BF16), wout_ref[cs, :])
    out = x + _rms(acc, gpost_ref[...])
    if residue_major_out:
        for r, piece in enumerate(_to_residue_major(slab[0], out)):
            y_ref[0, 0, r] = piece
    else:
        y_ref[0] = out
    st_ref[0] = tail_ref[0:2, :]


def _conv_layer_prompt(x, g_pre, w_in, w_conv, w_out, g_post, residue_major_out):
    bsz, seq, _ = x.shape
    scratch = [pltpu.VMEM((8, D_MODEL), F32)]
    if residue_major_out:
        y_spec = _set_spec(D_MODEL)
        y_shape = jax.ShapeDtypeStruct((bsz, seq // TA, NRES, BAND, D_MODEL), F32)
        scratch.append(pltpu.VMEM((D_MODEL // HEAD_DIM, 2, TM, HEAD_DIM), F32))
    else:
        y_spec = pl.BlockSpec((1, TM, D_MODEL), lambda b, i: (b, i, 0))
        y_shape = jax.ShapeDtypeStruct((bsz, seq, D_MODEL), F32)
    return pl.pallas_call(
        functools.partial(_conv_prompt_kernel, residue_major_out=residue_major_out),
        grid=(bsz, seq // TM),
        in_specs=[
            pl.BlockSpec((1, TM, D_MODEL), lambda b, i: (b, i, 0)),
            _const_spec((1, D_MODEL)),
            _const_spec((D_MODEL, 4 * D_MODEL)),
            _const_spec((3, D_MODEL)),
            _const_spec((D_MODEL, D_MODEL)),
            _const_spec((1, D_MODEL)),
        ],
        out_specs=[y_spec, pl.BlockSpec((1, 2, D_MODEL), lambda b, i: (b, 0, 0))],
        out_shape=[y_shape, jax.ShapeDtypeStruct((bsz, 2, D_MODEL), F32)],
        scratch_shapes=scratch,
        compiler_params=_params(("arbitrary", "arbitrary")),
        name="conv_layer_prompt",
    )(x, g_pre, w_in, w_conv, w_out, g_post)


def _conv_sample_kernel(x_ref, p0_ref, p1_ref, gpre_ref, win_ref, wconv_ref, wout_ref, gpost_ref,
                        y_ref, u_ref):
    x = x_ref[...]
    h = _rms(x, gpre_ref[...]).astype(BF16)
    acc = jnp.zeros(x.shape, F32)
    for j in range(D_MODEL // CH):
        cs = slice(j * CH, (j + 1) * CH)
        u, y = _conv_mix_chunk(h, win_ref, wconv_ref, j,
                               lambda u, cs: p1_ref[:, cs], lambda u, cs: p0_ref[:, cs])
        u_ref[:, cs] = u
        acc = acc + _dot(y.astype(BF16), wout_ref[cs, :])
    y_ref[...] = x + _rms(acc, gpost_ref[...])


def _conv_layer_sample(x, past0, past1, g_pre, w_in, w_conv, w_out, g_post):
    n = x.shape[0]
    return pl.pallas_call(
        _conv_sample_kernel,
        out_shape=[jax.ShapeDtypeStruct((n, D_MODEL), F32)] * 2,
        compiler_params=_params(),
        name="conv_layer_sample",
    )(x, past0, past1, g_pre, w_in, w_conv, w_out, g_post)


def _kv_prompt_kernel(x_ref, g_ref, w_ref, cos_ref, sin_ref, kv0_ref, kv1_ref, kv2_ref,
                      c0_ref, c1_ref, c2_ref, slab_ref, *, first_cache_tile):
    h = _rms(_load_set(x_ref), g_ref[...]).astype(BF16)
    cos = jnp.concatenate([cos_ref[0, r] for r in range(NRES)], axis=0)
    sin = jnp.concatenate([sin_ref[0, r] for r in range(NRES)], axis=0)
    kv = _dot(h, w_ref[...])
    kv_refs = (kv0_ref, kv1_ref, kv2_ref)
    fulls = []
    for g in range(N_GROUPS):
        ks, vs = [], []
        for hh in range(KV_HEADS):
            c = (g * KV_HEADS + hh) * HEAD_DIM
            k = _rope(kv[:, c:c + HEAD_DIM], cos, sin)
            v = kv[:, KV_HALF + c:KV_HALF + c + HEAD_DIM]
            both = jnp.concatenate([k, v], axis=-1).astype(GROUP_DTYPES[g])
            for r in range(NRES):
                kv_refs[g][0, hh, 0, r] = both[r * SUB:(r + 1) * SUB]
            ks.append(k)
            vs.append(v)
        fulls.append(jnp.concatenate(ks + vs, axis=-1))

    @pl.when(pl.program_id(1) >= first_cache_tile)
    def _():
        for g, c_ref in enumerate((c0_ref, c1_ref, c2_ref)):
            rows = min(TM, WINDOWS[g])
            c_ref[0] = _to_token_order(slab_ref, fulls[g])[TM - rows:, :]


def _kv_proj_prompt(x5, g, w_kv, cos5, sin5):
    bsz, nta = x5.shape[:2]
    seq = nta * TA
    nt = seq // TM
    per = TA // TM
    cache_specs, cache_shapes, firsts = [], [], []
    for grp in range(N_GROUPS):
        w = min(WINDOWS[grp], seq)
        rows = min(TM, w)
        first = nt - w // rows
        firsts.append(first)
        cache_specs.append(pl.BlockSpec(
            (1, rows, 4 * HEAD_DIM), lambda b, i, first=first: (b, jnp.maximum(i - first, 0), 0)))
        cache_shapes.append(jax.ShapeDtypeStruct((bsz, w, 4 * HEAD_DIM), F32))
    kv_spec = pl.BlockSpec((1, KV_HEADS, 1, NRES, SUB, KW),
                           lambda b, i: (b, 0, i // per, 0, i % per, 0))
    kv_shapes = [jax.ShapeDtypeStruct((bsz, KV_HEADS, nta, NRES, BAND, KW), dt)
                 for dt in GROUP_DTYPES]
    tab_spec = pl.BlockSpec((1, NRES, SUB, HEAD_DIM), lambda b, i: (i // per, 0, i % per, 0))
    outs = pl.pallas_call(
        functools.partial(_kv_prompt_kernel, first_cache_tile=min(firsts)),
        grid=(bsz, nt),
        in_specs=[_set_spec(D_MODEL), _const_spec((1, D_MODEL)),
                  _const_spec((D_MODEL, 2 * KV_HALF)), tab_spec, tab_spec],
        out_specs=[kv_spec] * N_GROUPS + cache_specs,
        out_shape=kv_shapes + cache_shapes,
        scratch_shapes=[pltpu.VMEM((4, 2, TM, HEAD_DIM), F32)],
        compiler_params=_params(("arbitrary", "arbitrary")),
        name="kv_proj_prompt",
    )(x5, g, w_kv, cos5, sin5)
    kvs = [o.reshape(bsz, KV_HEADS, seq, KW) for o in outs[:N_GROUPS]]
    return kvs, outs[N_GROUPS:]


def _q_prompt_kernel(x_ref, g_ref, w_ref, cos_ref, sin_ref, q0_ref, q1_ref, q2_ref, z_ref):
    h = _rms(x_ref[0], g_ref[...]).astype(BF16)
    cos = cos_ref[...]
    sin = sin_ref[...]
    q_refs = (q0_ref, q1_ref, q2_ref)
    for g in range(N_GROUPS):
        for kh in range(KV_HEADS):
            c = (g * KV_HEADS + kh) * QW
            qq = _dot(h, w_ref[:, c:c + QW])
            heads = [_rope(qq[:, j * HEAD_DIM:(j + 1) * HEAD_DIM], cos, sin)
                     for j in range(Q_PER_KV)]
            q_refs[g][0, kh] = jnp.concatenate(heads, axis=-1).astype(GROUP_DTYPES[g])
    z_ref[0] = _dot(h, w_ref[:, Q_WIDTH:])


def _q_proj_prompt(x, g, w_in, cos_q, sin_q):
    bsz, seq, _ = x.shape
    x_spec = pl.BlockSpec((1, TM, D_MODEL), lambda b, i: (b, i, 0))
    tab_spec = pl.BlockSpec((TM, HEAD_DIM), lambda b, i: (i, 0))
    return pl.pallas_call(
        _q_prompt_kernel,
        grid=(bsz, seq // TM),
        in_specs=[x_spec, _const_spec((1, D_MODEL)), _const_spec((D_MODEL, Q_WIDTH + D_MODEL)),
                  tab_spec, tab_spec],
        out_specs=[pl.BlockSpec((1, KV_HEADS, TM, QW), lambda b, i: (b, 0, i, 0))] * N_GROUPS
        + [x_spec],
        out_shape=[jax.ShapeDtypeStruct((bsz, KV_HEADS, seq, QW), dt) for dt in GROUP_DTYPES]
        + [jax.ShapeDtypeStruct((bsz, seq, D_MODEL), F32)],
        compiler_params=_params(("arbitrary", "arbitrary")),
        name="q_proj_prompt",
    )(x, g, w_in, cos_q, sin_q)


def _block_runs(g, blk):
    if g == 2:
        return [(blk * BAND, BAND)]
    if g == 1:
        r4, c = blk
        return [((4 * mm + r4) * BAND + 32 * c, 32) for mm in range(4)]
    return [(r * BAND + 8 * blk, 8) for r in range(NRES)]


def _block_pos(g):
    a = jnp.arange(BAND)
    if g == 2:
        return a
    if g == 1:
        return 4 * (a % 32) + a // 32
    return NRES * (a % 8) + a // 8


def _band_bias():
    out = []
    for g in range(N_GROUPS):
        pos = _block_pos(g)
        qpos = jnp.tile(pos, Q_PER_KV)[:, None]
        ok_prev = pos[None, :] >= qpos
        ok_cur = pos[None, :] <= qpos
        neg = jnp.full(ok_cur.shape, NEG, F32)
        cur = jnp.where(ok_cur, 0.0, neg)
        out.append(jnp.concatenate([jnp.where(ok_prev, 0.0, neg), cur], axis=1))
        out.append(jnp.concatenate([neg, cur], axis=1))
    return jnp.stack(out)


def _attn_prompt_kernel(q0_ref, q1_ref, q2_ref, kv0_ref, kv1_ref, kv2_ref, bias_ref, o_ref,
                        acc_ref, m_ref, l_ref, biasf_ref, p0_ref, p1_ref, p2_ref):
    q_refs = (q0_ref, q1_ref, q2_ref)
    kv_refs = (kv0_ref, kv1_ref, kv2_ref)
    prev_refs = (p0_ref, p1_ref, p2_ref)
    prev_rows = (8, 32, BAND)
    first = pl.program_id(2) == 0

    @pl.when(first)
    def _():
        for p_ref in prev_refs:
            p_ref[...] = jnp.zeros_like(p_ref)

    for g in range(N_GROUPS):
        biasf_ref[g] = jnp.where(first, bias_ref[2 * g + 1], bias_ref[2 * g])
    ones = jnp.ones((2 * BAND, HEAD_DIM), BF16)
    m_rows = Q_PER_KV * BAND

    def gather(read, runs):
        return jnp.concatenate([read(rs, n) for rs, n in runs], axis=0)

    def heads_state(ref, runs):
        return jnp.concatenate(
            [gather(lambda rs, n, j=j: ref[j, rs:rs + n, :], runs) for j in range(Q_PER_KV)], axis=0)

    def scatter_state(ref, val, runs):
        for j in range(Q_PER_KV):
            off = j * BAND
            for rs, n in runs:
                ref[j, rs:rs + n, :] = val[off:off + n]
                off += n

    for g in range(N_GROUPS):
        q_ref, kv_ref, p_ref = q_refs[g], kv_refs[g], prev_refs[g]
        if g == 2:
            blocks = [(r, None) for r in range(NRES)]
        elif g == 1:
            blocks = [((r4, c), (r4, c - 1) if c else None) for r4 in range(4) for c in range(4)]
        else:
            blocks = [(c, c - 1 if c else None) for c in range(NRES)]
        for blk, prev_blk in blocks:
            runs = _block_runs(g, blk)
            q4 = jnp.concatenate(
                [gather(lambda rs, n, j=j: q_ref[0, 0, rs:rs + n, j * HEAD_DIM:(j + 1) * HEAD_DIM],
                        runs) for j in range(Q_PER_KV)], axis=0).astype(BF16)
            if prev_blk is None:
                pr = prev_rows[g]
                if g == 2:
                    pruns = [(blk * BAND, BAND)]
                elif g == 1:
                    pruns = [((4 * mm + blk[0]) * pr, pr) for mm in range(4)]
                else:
                    pruns = [(r * pr, pr) for r in range(NRES)]
                read_prev = lambda cols: gather(lambda rs, n: p_ref[rs:rs + n, cols], pruns)
                bias = biasf_ref[g]
            else:
                pruns = _block_runs(g, prev_blk)
                read_prev = lambda cols: gather(lambda rs, n: kv_ref[0, 0, rs:rs + n, cols], pruns)
                bias = bias_ref[2 * g]
            read_cur = lambda cols: gather(lambda rs, n: kv_ref[0, 0, rs:rs + n, cols], runs)
            kcols, vcols = slice(0, HEAD_DIM), slice(HEAD_DIM, KW)
            k_cat = jnp.concatenate([read_prev(kcols), read_cur(kcols)], axis=0).astype(BF16)
            v_cat = jnp.concatenate([read_prev(vcols), read_cur(vcols)], axis=0).astype(BF16)

            s = _dot_nt(q4, k_cat) + bias
            m_new = jnp.broadcast_to(jnp.max(s, axis=-1, keepdims=True), (m_rows, HEAD_DIM))
            if g > 0:
                m_old = heads_state(m_ref, runs)
                m_new = jnp.maximum(m_old, m_new)
                alpha = jnp.exp2(m_old - m_new)
            p = jnp.exp2(jnp.concatenate([s[:, :BAND] - m_new, s[:, BAND:] - m_new], axis=1))
            ov = _dot(p.astype(BF16), jnp.concatenate([v_cat, ones], axis=1))
            acc_new, l_new = ov[:, :HEAD_DIM], ov[:, HEAD_DIM:]
            if g > 0:
                acc_new = alpha * heads_state(acc_ref, runs) + acc_new
                l_new = alpha * heads_state(l_ref, runs) + l_new
            if g < N_GROUPS - 1:
                scatter_state(acc_ref, acc_new, runs)
                scatter_state(m_ref, m_new, runs)
                scatter_state(l_ref, l_new, runs)
            else:
                o = (acc_new / l_new).astype(BF16)
                (rs, n), = runs
                for j in range(Q_PER_KV):
                    o_ref[0, rs:rs + n, j * HEAD_DIM:(j + 1) * HEAD_DIM] = o[j * BAND:(j + 1) * BAND]

    for g in range(N_GROUPS):
        pr = prev_rows[g]
        for r in range(NRES):
            prev_refs[g][r * pr:(r + 1) * pr, :] = kv_refs[g][0, 0, (r + 1) * BAND - pr:(r + 1) * BAND, :]


def _attn_prompt(qs, kvs, bias):
    bsz, _, seq, _ = qs[0].shape
    blk = lambda width: pl.BlockSpec((1, 1, TA, width), lambda b, h, i: (b, h, i, 0))
    return pl.pallas_call(
        _attn_prompt_kernel,
        grid=(bsz, KV_HEADS, seq // TA),
        in_specs=[blk(QW)] * N_GROUPS + [blk(KW)] * N_GROUPS + [_const_spec(bias.shape)],
        out_specs=pl.BlockSpec((1, TA, QW), lambda b, h, i: (b, i, h)),
        out_shape=jax.ShapeDtypeStruct((bsz, seq, KV_HEADS * QW), BF16),
        scratch_shapes=[
            pltpu.VMEM((Q_PER_KV, TA, HEAD_DIM), F32),
            pltpu.VMEM((Q_PER_KV, TA, HEAD_DIM), F32),
            pltpu.VMEM((Q_PER_KV, TA, HEAD_DIM), F32),
            pltpu.VMEM((N_GROUPS, Q_PER_KV * BAND, 2 * BAND), F32),
            pltpu.VMEM((NRES * 8, KW), GROUP_DTYPES[0]),
            pltpu.VMEM((NRES * 32, KW), GROUP_DTYPES[1]),
            pltpu.VMEM((NRES * BAND, KW), GROUP_DTYPES[2]),
        ],
        compiler_params=_params(("arbitrary", "arbitrary", "arbitrary")),
        name="attn_prompt",
    )(*qs, *kvs, bias)


def _out_prompt_kernel(o_ref, z_ref, x_ref, wout_ref, gpost_ref, y_ref, *slab, token_order_out):
    if token_order_out:
        o, z, x = _load_set(o_ref), _load_set(z_ref), _load_set(x_ref)
    else:
        o, z, x = o_ref[0], z_ref[0], x_ref[0]
    y = (o.astype(F32) * _silu(z)).astype(BF16)
    res = x + _rms(_dot(y, wout_ref[...]), gpost_ref[...])
    y_ref[0] = _to_token_order(slab[0], res) if token_order_out else res


def _out_proj_prompt(o, z, x, w_out, g_post, token_order_out):
    bsz, seq, _ = x.shape
    x_spec = pl.BlockSpec((1, TM, D_MODEL), lambda b, i: (b, i, 0))
    scratch = []
    if token_order_out:
        view = lambda a: a.reshape(bsz, seq // TA, NRES, BAND, D_MODEL)
        o, z, x = view(o), view(z), view(x)
        in_spec = _set_spec(D_MODEL)
        scratch.append(pltpu.VMEM((D_MODEL // HEAD_DIM, 2, TM, HEAD_DIM), F32))
    else:
        in_spec = x_spec
    return pl.pallas_call(
        functools.partial(_out_prompt_kernel, token_order_out=token_order_out),
        grid=(bsz, seq // TM),
        in_specs=[in_spec] * 3 + [_const_spec((D_MODEL, D_MODEL)), _const_spec((1, D_MODEL))],
        out_specs=x_spec,
        out_shape=jax.ShapeDtypeStruct((bsz, seq, D_MODEL), F32),
        scratch_shapes=scratch,
        compiler_params=_params(("arbitrary", "arbitrary")),
        name="out_proj_prompt",
    )(o, z, x, w_out, g_post)


def _proj_sample_kernel(x_ref, g_ref, w_ref, cos_ref, sin_ref, y_ref, *, rope_heads):
    h = _rms(x_ref[...], g_ref[...]).astype(BF16)
    y = _dot(h, w_ref[...])
    cos = cos_ref[...]
    sin = sin_ref[...]
    for hd in range(rope_heads):
        cs = slice(hd * HEAD_DIM, (hd + 1) * HEAD_DIM)
        y_ref[:, cs] = _rope(y[:, cs], cos, sin)
    rest = rope_heads * HEAD_DIM
    y_ref[:, rest:] = y[:, rest:]


def _proj_sample(x, g, w, cos, sin, rope_heads):
    n = x.shape[0]
    return pl.pallas_call(
        functools.partial(_proj_sample_kernel, rope_heads=rope_heads),
        out_shape=jax.ShapeDtypeStruct((n, w.shape[1]), F32),
        compiler_params=_params(),
        name=f"proj_sample_{rope_heads}",
    )(x, g, w, cos, sin)


def _attn_sample_body(qz, kvn, window_kv, y_ref):
    pieces = []
    for kh in range(KV_HEADS):
        scores, s_new, vals, v_new = [], [], [], []
        for g in range(N_GROUPS):
            c = (g * HEADS + kh * Q_PER_KV) * HEAD_DIM
            q4 = jnp.concatenate(
                [qz[:, c + j * HEAD_DIM:c + (j + 1) * HEAD_DIM] for j in range(Q_PER_KV)], axis=0)
            q8 = jnp.concatenate([q4, q4], axis=0)
            k_rows, v_rows = window_kv(g, kh)
            kc = (g * KV_HEADS + kh) * HEAD_DIM
            k_n = kvn[:, kc:kc + HEAD_DIM]
            v_n = kvn[:, KV_HALF + kc:KV_HALF + kc + HEAD_DIM]
            scores.append(_dot_nt(q8.astype(BF16), k_rows))
            s_new.append(jnp.sum(q8 * k_n, axis=-1, keepdims=True))
            vals.append(v_rows)
            v_new.append(v_n)
        m = functools.reduce(jnp.maximum,
                             [jnp.max(s, axis=-1, keepdims=True) for s in scores] + s_new)
        ps = [jnp.exp(s - m) for s in scores]
        pn = [jnp.exp(s - m) for s in s_new]
        l = sum(jnp.sum(p, axis=-1, keepdims=True) for p in ps) + sum(pn)
        o = sum(_dot(ps[g].astype(BF16), vals[g]) + pn[g] * v_new[g] for g in range(N_GROUPS)) / l
        pieces += [o[j:j + 1, :] for j in range(Q_PER_KV)]
    o_row = jnp.concatenate(pieces, axis=-1)
    y_ref[0] = o_row * _silu(qz[:, Q_WIDTH:])


def _attn_sample_first_kernel(qz_ref, kvn_ref, c0_ref, c1_ref, c2_ref,
                              y_ref, oc0_ref, oc1_ref, oc2_ref, win_ref):
    c_refs = (c0_ref, c1_ref, c2_ref)
    kvn = kvn_ref[0]
    nr = 2 * KV_HEADS

    def window_kv(g, kh):
        stride = nr * DILATIONS[g]
        k_rows = c_refs[g][0, pl.ds(kh, BAND, stride=stride), :].astype(BF16)
        v_rows = c_refs[g][0, pl.ds(KV_HEADS + kh, BAND, stride=stride), :].astype(BF16)
        slot = 2 * (g * KV_HEADS + kh)
        win_ref[0, slot] = k_rows
        win_ref[0, slot + 1] = v_rows
        return k_rows, v_rows

    _attn_sample_body(qz_ref[0], kvn, window_kv, y_ref)
    for g, oc_ref in enumerate((oc0_ref, oc1_ref, oc2_ref)):
        keep = (WINDOWS[g] - 1) * nr
        oc_ref[0, 0:keep, :] = c_refs[g][0, nr:nr + keep, :]
        new_rows = [kvn[:, half + (g * KV_HEADS + hh) * HEAD_DIM:half + (g * KV_HEADS + hh + 1) * HEAD_DIM]
                    for half in (0, KV_HALF) for hh in range(KV_HEADS)]
        oc_ref[0, keep:keep + nr, :] = jnp.concatenate(new_rows, axis=0)


def _attn_sample_next_kernel(qz_ref, kvn_ref, win_ref, y_ref):
    def window_kv(g, kh):
        slot = 2 * (g * KV_HEADS + kh)
        return win_ref[0, slot], win_ref[0, slot + 1]

    _attn_sample_body(qz_ref[0], kvn_ref[0], window_kv, y_ref)


def _attn_sample_first(qz, kv_new, caches):
    n = qz.shape[0]
    row_spec = lambda a: pl.BlockSpec((1, 1, a.shape[-1]), lambda b: (b, 0, 0))
    cache_specs = [pl.BlockSpec((1, c.shape[1], HEAD_DIM), lambda b: (b, 0, 0)) for c in caches]
    win_shape = (n, 2 * N_GROUPS * KV_HEADS, BAND, HEAD_DIM)
    return pl.pallas_call(
        _attn_sample_first_kernel,
        grid=(n,),
        in_specs=[row_spec(qz), row_spec(kv_new)] + cache_specs,
        out_specs=[pl.BlockSpec((1, 1, D_MODEL), lambda b: (b, 0, 0))] + cache_specs
        + [pl.BlockSpec((1,) + win_shape[1:], lambda b: (b, 0, 0, 0))],
        out_shape=[jax.ShapeDtypeStruct((n, 1, D_MODEL), F32)]
        + [jax.ShapeDtypeStruct(c.shape, F32) for c in caches]
        + [jax.ShapeDtypeStruct(win_shape, BF16)],
        compiler_params=_params(("arbitrary",)),
        name="attn_sample_first",
    )(qz, kv_new, *caches)


def _attn_sample_next(qz, kv_new, win):
    n = qz.shape[0]
    row_spec = lambda a: pl.BlockSpec((1, 1, a.shape[-1]), lambda b: (b, 0, 0))
    return pl.pallas_call(
        _attn_sample_next_kernel,
        grid=(n,),
        in_specs=[row_spec(qz), row_spec(kv_new),
                  pl.BlockSpec((1,) + win.shape[1:], lambda b: (b, 0, 0, 0))],
        out_specs=pl.BlockSpec((1, 1, D_MODEL), lambda b: (b, 0, 0)),
        out_shape=jax.ShapeDtypeStruct((n, 1, D_MODEL), F32),
        compiler_params=_params(("arbitrary",)),
        name="attn_sample_next",
    )(qz, kv_new, win)


def _out_sample_kernel(y_ref, x_ref, wout_ref, gpost_ref, o_ref):
    out = _dot(y_ref[...].astype(BF16), wout_ref[...])
    o_ref[...] = x_ref[...] + _rms(out, gpost_ref[...])


def _out_sample(y, x, w_out, g_post):
    return pl.pallas_call(
        _out_sample_kernel,
        out_shape=jax.ShapeDtypeStruct(x.shape, F32),
        compiler_params=_params(),
        name="out_sample",
    )(y, x, w_out, g_post)


def _rope_tables(pos):
    half = HEAD_DIM // 2
    inv_freq = ROPE_THETA ** (-jnp.arange(half, dtype=F32) * 2.0 / HEAD_DIM)
    ang = pos.astype(F32)[:, None] * inv_freq[None, :]
    cos = jnp.cos(ang)
    sin = jnp.sin(ang)
    return jnp.concatenate([cos, cos], axis=-1), jnp.concatenate([-sin, sin], axis=-1)


def kernel(x_prompt, x_sample, state_conv, cache_kv_g0, cache_kv_g1, cache_kv_g2,
           a_norm_pre, a_w_in, a_conv, a_w_out, a_norm_post,
           kv_norm, w_kv, b_norm_pre, b_w_in, b_w_out, b_norm_post):
    bsz, seq, _ = x_prompt.shape
    n_s = x_sample.shape[0]
    n_a = a_w_in.shape[0]
    n_b = b_w_in.shape[0]
    assert x_sample.shape[1] == 1 and seq % TA == 0 and n_a >= 1 and n_b >= 1

    pos_rm = jnp.arange(seq).reshape(seq // TA, BAND, NRES).transpose(0, 2, 1).reshape(seq)
    cos_p, sin_p = _rope_tables(pos_rm)
    cos_s, sin_s = _rope_tables(PAST_LEN + jnp.arange(1))
    row = lambda v: v.reshape(1, -1)

    a_w_in16 = a_w_in.astype(BF16)
    a_w_out16 = a_w_out.astype(BF16)
    w_kv16 = w_kv.astype(BF16)
    b_w_in16 = b_w_in.astype(BF16)
    b_w_out16 = b_w_out.astype(BF16)

    hp = x_prompt
    hs = x_sample.reshape(n_s, D_MODEL)
    conv_p, conv_s = [], []
    for layer in range(n_a):
        args = (row(a_norm_pre[layer]), a_w_in16[layer], a_conv[layer], a_w_out16[layer],
                row(a_norm_post[layer]))
        hp, st_p = _conv_layer_prompt(hp, *args, residue_major_out=(layer == n_a - 1))
        past0 = state_conv[layer, :, 0]
        past1 = state_conv[layer, :, 1]
        hs, u_s = _conv_layer_sample(hs, past0, past1, *args)
        conv_p.append(st_p)
        conv_s.append(jnp.stack([past1, u_s], axis=1))

    tab5 = lambda t: t.reshape(seq // TA, NRES, BAND, HEAD_DIM)
    kv_p, cs = _kv_proj_prompt(hp, row(kv_norm), w_kv16, tab5(cos_p), tab5(sin_p))
    kv_new_p = [c.reshape(bsz, c.shape[1], 2, KV_HEADS, HEAD_DIM) for c in cs]
    hp = hp.reshape(bsz, seq, D_MODEL)
    bias = _band_bias()
    kv_s = _proj_sample(hs, row(kv_norm), w_kv16, cos_s, sin_s, N_GROUPS * KV_HEADS)

    caches = [c.reshape(n_s, c.shape[1] * 2 * KV_HEADS, HEAD_DIM)
              for c in (cache_kv_g0, cache_kv_g1, cache_kv_g2)]
    kv_new_s = None
    for j in range(n_b):
        g_pre = row(b_norm_pre[j])
        g_post = row(b_norm_post[j])
        *q, z = _q_proj_prompt(hp, g_pre, b_w_in16[j], cos_p * (SCALE * LOG2E),
                               sin_p * (SCALE * LOG2E))
        o = _attn_prompt(q, kv_p, bias)
        hp = _out_proj_prompt(o, z, hp, b_w_out16[j], g_post, token_order_out=(j == n_b - 1))
        qz = _proj_sample(hs, g_pre, b_w_in16[j], cos_s * SCALE, sin_s * SCALE, N_GROUPS * HEADS)
        qz3 = qz.reshape(n_s, 1, -1)
        if j == 0:
            y_s, *shifted, win = _attn_sample_first(qz3, kv_s.reshape(n_s, 1, -1), caches)
            kv_new_s = [c.reshape(n_s, WINDOWS[g], 2, KV_HEADS, HEAD_DIM)
                        for g, c in enumerate(shifted)]
        else:
            y_s = _attn_sample_next(qz3, kv_s.reshape(n_s, 1, -1), win)
        hs = _out_sample(y_s.reshape(n_s, D_MODEL), hs, b_w_out16[j], g_post)

    return (hp, hs.reshape(n_s, 1, D_MODEL), jnp.stack(conv_p, axis=0),
            kv_new_p[0], kv_new_p[1], kv_new_p[2],
            jnp.stack(conv_s, axis=0), kv_new_s[0], kv_new_s[1], kv_new_s[2])
```

```python
import functools

import jax
import jax.numpy as jnp
from jax import lax
from jax.experimental import pallas as pl
from jax.experimental.pallas import tpu as pltpu

D_MODEL = 1024
HEAD_DIM = 128
N_GROUPS = 3
WINDOWS = (128, 512, 2048)
DILATIONS = (1, 4, 16)
HEADS = 8
KV_HEADS = 2
Q_PER_KV = HEADS // KV_HEADS
Q_WIDTH = N_GROUPS * HEADS * HEAD_DIM
KV_HALF = N_GROUPS * KV_HEADS * HEAD_DIM
PAST_LEN = 8192
ROPE_THETA = 10000.0
EPS = 1e-6
SCALE = HEAD_DIM ** -0.5
NEG = -1e30
LOG2E = 1.4426950408889634

BAND = 128
NRES = 16
TA = NRES * BAND
TM = 512
SUB = TM // NRES
CH = 256
QW = Q_PER_KV * HEAD_DIM
KW = 2 * HEAD_DIM
VMEM_LIMIT = 54 * 1024 * 1024

F32 = jnp.float32
BF16 = jnp.bfloat16
GROUP_DTYPES = (F32, BF16, BF16)


def _dot(a, b):
    return jnp.dot(a, b, preferred_element_type=F32)


def _dot_nt(a, b):
    return lax.dot_general(a, b, (((1,), (1,)), ((), ())), preferred_element_type=F32)


def _rms(x, g):
    ms = jnp.mean(x * x, axis=-1, keepdims=True)
    return x * lax.rsqrt(ms + EPS) * g


def _silu(z):
    return z * jax.nn.sigmoid(z)


def _rope(x, cos, sin_signed):
    return x * cos + pltpu.roll(x, HEAD_DIM // 2, 1) * sin_signed


def _params(sem=None, flags=None):
    return pltpu.CompilerParams(dimension_semantics=sem, vmem_limit_bytes=VMEM_LIMIT, flags=flags)


def _const_spec(shape):
    zeros = (0,) * len(shape)
    return pl.BlockSpec(shape, lambda *_: zeros, pipeline_mode=pl.Buffered(1))


def _split_residues(slab_ref, val):
    q = TM // 4
    slab_ref[0] = val
    for m in range(4):
        slab_ref[1, m * q:(m + 1) * q, :] = slab_ref[0, pl.ds(m, q, stride=4), :]
    return {4 * m2 + m: slab_ref[1, pl.ds(m * q + m2, SUB, stride=4), :]
            for m in range(4) for m2 in range(4)}


def _merge_residues(slab_ref, pieces):
    q = TM // 4
    for m in range(4):
        for m2 in range(4):
            slab_ref[1, pl.ds(m * q + m2, SUB, stride=4), :] = pieces[4 * m2 + m]
    for m in range(4):
        slab_ref[0, pl.ds(m, q, stride=4), :] = slab_ref[1, m * q:(m + 1) * q, :]
    return slab_ref[0]


def _to_residue_major(slab_ref, val):
    ncol = val.shape[1] // HEAD_DIM
    cols = [_split_residues(slab_ref.at[c], val[:, c * HEAD_DIM:(c + 1) * HEAD_DIM])
            for c in range(ncol)]
    return [jnp.concatenate([cols[c][r] for c in range(ncol)], axis=-1) for r in range(NRES)]


def _to_token_order(slab_ref, val):
    ncol = val.shape[1] // HEAD_DIM
    cols = [_merge_residues(slab_ref.at[c],
                            {r: val[r * SUB:(r + 1) * SUB, c * HEAD_DIM:(c + 1) * HEAD_DIM]
                             for r in range(NRES)})
            for c in range(ncol)]
    return jnp.concatenate(cols, axis=-1)


def _set_spec(width):
    per = TA // TM
    return pl.BlockSpec((1, 1, NRES, SUB, width), lambda b, i: (b, i // per, 0, i % per, 0))


def _load_set(ref):
    return jnp.concatenate([ref[0, 0, r] for r in range(NRES)], axis=0)


def _conv_project_chunk(h, win_ref, j):
    return [_dot(h, win_ref[:, k * D_MODEL + j * CH:k * D_MODEL + (j + 1) * CH]) for k in range(4)]


def _conv_mix_chunk(proj, wconv_ref, j, u_prev1, u_prev2):
    cs = slice(j * CH, (j + 1) * CH)
    b_gate, c_gate, u_in, z = proj
    u = c_gate * u_in
    conv = (u_prev2(u, cs) * wconv_ref[0:1, cs] + u_prev1(u, cs) * wconv_ref[1:2, cs]
            + u * wconv_ref[2:3, cs])
    return u, b_gate * conv * _silu(z)


def _conv_chunks(h, win_ref, wconv_ref, wout_ref, u_prev1, u_prev2, keep_u):
    nch = D_MODEL // CH
    acc = jnp.zeros((h.shape[0], D_MODEL), F32)
    proj = _conv_project_chunk(h, win_ref, 0)
    for j in range(nch):
        nxt = _conv_project_chunk(h, win_ref, j + 1) if j + 1 < nch else None
        u, y = _conv_mix_chunk(proj, wconv_ref, j, u_prev1, u_prev2)
        keep_u(u, slice(j * CH, (j + 1) * CH))
        acc = acc + _dot(y.astype(BF16), wout_ref[j * CH:(j + 1) * CH, :])
        proj = nxt
    return acc


def _conv_prompt_kernel(x_ref, gpre_ref, win_ref, wconv_ref, wout_ref, gpost_ref,
                        y_ref, st_ref, tail_ref, *slab, residue_major_out):
    @pl.when(pl.program_id(1) == 0)
    def _():
        tail_ref[...] = jnp.zeros_like(tail_ref)

    x = x_ref[0]
    h = _rms(x, gpre_ref[...]).astype(BF16)
    row = lax.broadcasted_iota(jnp.int32, (TM, CH), 0)

    def prev1(u, cs):
        return jnp.where(row == 0, tail_ref[1:2, cs], pltpu.roll(u, 1, 0))

    def prev2(u, cs):
        return jnp.where(row == 0, tail_ref[0:1, cs],
                         jnp.where(row == 1, tail_ref[1:2, cs], pltpu.roll(u, 2, 0)))

    def keep_tail(u, cs):
        tail_ref[0:2, cs] = u[TM - 2:TM, :]

    acc = _conv_chunks(h, win_ref, wconv_ref, wout_ref, prev1, prev2, keep_tail)
    out = x + _rms(acc, gpost_ref[...])
    if residue_major_out:
        for r, piece in enumerate(_to_residue_major(slab[0], out)):
            y_ref[0, 0, r] = piece
    else:
        y_ref[0] = out
    st_ref[0] = tail_ref[0:2, :]


def _conv_layer_prompt(x, g_pre, w_in, w_conv, w_out, g_post, residue_major_out):
    bsz, seq, _ = x.shape
    scratch = [pltpu.VMEM((8, D_MODEL), F32)]
    if residue_major_out:
        y_spec = _set_spec(D_MODEL)
        y_shape = jax.ShapeDtypeStruct((bsz, seq // TA, NRES, BAND, D_MODEL), F32)
        scratch.append(pltpu.VMEM((D_MODEL // HEAD_DIM, 2, TM, HEAD_DIM), F32))
    else:
        y_spec = pl.BlockSpec((1, TM, D_MODEL), lambda b, i: (b, i, 0))
        y_shape = jax.ShapeDtypeStruct((bsz, seq, D_MODEL), F32)
    return pl.pallas_call(
        functools.partial(_conv_prompt_kernel, residue_major_out=residue_major_out),
        grid=(bsz, seq // TM),
        in_specs=[
            pl.BlockSpec((1, TM, D_MODEL), lambda b, i: (b, i, 0)),
            _const_spec((1, D_MODEL)),
            _const_spec((D_MODEL, 4 * D_MODEL)),
            _const_spec((3, D_MODEL)),
            _const_spec((D_MODEL, D_MODEL)),
            _const_spec((1, D_MODEL)),
        ],
        out_specs=[y_spec, pl.BlockSpec((1, 2, D_MODEL), lambda b, i: (b, 0, 0))],
        out_shape=[y_shape, jax.ShapeDtypeStruct((bsz, 2, D_MODEL), F32)],
        scratch_shapes=scratch,
        compiler_params=_params(("arbitrary", "arbitrary")),
        name="conv_layer_prompt",
    )(x, g_pre, w_in, w_conv, w_out, g_post)


def _conv_sample_kernel(x_ref, p0_ref, p1_ref, gpre_ref, win_ref, wconv_ref, wout_ref, gpost_ref,
                        y_ref, u_ref):
    x = x_ref[...]
    h = _rms(x, gpre_ref[...]).astype(BF16)
    def keep_u(u, cs):
        u_ref[:, cs] = u

    acc = _conv_chunks(h, win_ref, wconv_ref, wout_ref,
                       lambda u, cs: p1_ref[:, cs], lambda u, cs: p0_ref[:, cs], keep_u)
    y_ref[...] = x + _rms(acc, gpost_ref[...])


def _conv_layer_sample(x, past0, past1, g_pre, w_in, w_conv, w_out, g_post):
    n = x.shape[0]
    return pl.pallas_call(
        _conv_sample_kernel,
        out_shape=[jax.ShapeDtypeStruct((n, D_MODEL), F32)] * 2,
        compiler_params=_params(),
        name="conv_layer_sample",
    )(x, past0, past1, g_pre, w_in, w_conv, w_out, g_post)


def _kv_prompt_kernel(x_ref, g_ref, w_ref, cos_ref, sin_ref, kv0_ref, kv1_ref, kv2_ref,
                      c0_ref, c1_ref, c2_ref, slab_ref, *, first_cache_tile):
    h = _rms(_load_set(x_ref), g_ref[...]).astype(BF16)
    cos = jnp.concatenate([cos_ref[0, r] for r in range(NRES)], axis=0)
    sin = jnp.concatenate([sin_ref[0, r] for r in range(NRES)], axis=0)
    kv_refs = (kv0_ref, kv1_ref, kv2_ref)
    c_refs = (c0_ref, c1_ref, c2_ref)
    gw = KV_HEADS * HEAD_DIM

    def project(g):
        return (_dot(h, w_ref[:, g * gw:(g + 1) * gw]),
                _dot(h, w_ref[:, KV_HALF + g * gw:KV_HALF + (g + 1) * gw]))

    cur = project(0)
    for g in range(N_GROUPS):
        nxt = project(g + 1) if g + 1 < N_GROUPS else None
        ks = [_rope(cur[0][:, hh * HEAD_DIM:(hh + 1) * HEAD_DIM], cos, sin) for hh in range(KV_HEADS)]
        vs = [cur[1][:, hh * HEAD_DIM:(hh + 1) * HEAD_DIM] for hh in range(KV_HEADS)]
        for hh in range(KV_HEADS):
            both = jnp.concatenate([ks[hh], vs[hh]], axis=-1).astype(GROUP_DTYPES[g])
            for r in range(NRES):
                kv_refs[g][0, hh, 0, r] = both[r * SUB:(r + 1) * SUB]

        @pl.when(pl.program_id(1) >= first_cache_tile)
        def _(g=g, ks=ks, vs=vs):
            rows = min(TM, WINDOWS[g])
            full = jnp.concatenate(ks + vs, axis=-1)
            c_refs[g][0] = _to_token_order(slab_ref, full)[TM - rows:, :]

        cur = nxt


def _kv_proj_prompt(x5, g, w_kv, cos5, sin5):
    bsz, nta = x5.shape[:2]
    seq = nta * TA
    nt = seq // TM
    per = TA // TM
    cache_specs, cache_shapes, firsts = [], [], []
    for grp in range(N_GROUPS):
        w = min(WINDOWS[grp], seq)
        rows = min(TM, w)
        first = nt - w // rows
        firsts.append(first)
        cache_specs.append(pl.BlockSpec(
            (1, rows, 4 * HEAD_DIM), lambda b, i, first=first: (b, jnp.maximum(i - first, 0), 0)))
        cache_shapes.append(jax.ShapeDtypeStruct((bsz, w, 4 * HEAD_DIM), F32))
    kv_spec = pl.BlockSpec((1, KV_HEADS, 1, NRES, SUB, KW),
                           lambda b, i: (b, 0, i // per, 0, i % per, 0))
    kv_shapes = [jax.ShapeDtypeStruct((bsz, KV_HEADS, nta, NRES, BAND, KW), dt)
                 for dt in GROUP_DTYPES]
    tab_spec = pl.BlockSpec((1, NRES, SUB, HEAD_DIM), lambda b, i: (i // per, 0, i % per, 0))
    outs = pl.pallas_call(
        functools.partial(_kv_prompt_kernel, first_cache_tile=min(firsts)),
        grid=(bsz, nt),
        in_specs=[_set_spec(D_MODEL), _const_spec((1, D_MODEL)),
                  _const_spec((D_MODEL, 2 * KV_HALF)), tab_spec, tab_spec],
        out_specs=[kv_spec] * N_GROUPS + cache_specs,
        out_shape=kv_shapes + cache_shapes,
        scratch_shapes=[pltpu.VMEM((4, 2, TM, HEAD_DIM), F32)],
        compiler_params=_params(("arbitrary", "arbitrary")),
        name="kv_proj_prompt",
    )(x5, g, w_kv, cos5, sin5)
    kvs = [o.reshape(bsz, KV_HEADS, seq, KW) for o in outs[:N_GROUPS]]
    return kvs, outs[N_GROUPS:]


def _q_project(x, g_ref, w_ref, cos_ref, sin_ref, q_refs, z_ref):
    h = _rms(x, g_ref[...]).astype(BF16)
    cos = cos_ref[...]
    sin = sin_ref[...]
    units = [(g, kh) for g in range(N_GROUPS) for kh in range(KV_HEADS)]
    project = lambda u: _dot(h, w_ref[:, (u[0] * KV_HEADS + u[1]) * QW:(u[0] * KV_HEADS + u[1] + 1) * QW])
    cur = project(units[0])
    for n, (g, kh) in enumerate(units):
        nxt = project(units[n + 1]) if n + 1 < len(units) else _dot(h, w_ref[:, Q_WIDTH:])
        heads = [_rope(cur[:, j * HEAD_DIM:(j + 1) * HEAD_DIM], cos, sin) for j in range(Q_PER_KV)]
        q_refs[g][0, kh] = jnp.concatenate(heads, axis=-1).astype(GROUP_DTYPES[g])
        cur = nxt
    z_ref[0] = cur


def _q_prompt_kernel(x_ref, g_ref, w_ref, cos_ref, sin_ref, q0_ref, q1_ref, q2_ref, z_ref):
    _q_project(x_ref[0], g_ref, w_ref, cos_ref, sin_ref, (q0_ref, q1_ref, q2_ref), z_ref)


def _q_specs(bsz, seq):
    x_spec = pl.BlockSpec((1, TM, D_MODEL), lambda b, i: (b, i, 0))
    tab_spec = pl.BlockSpec((TM, HEAD_DIM), lambda b, i: (i, 0))
    in_specs = [_const_spec((1, D_MODEL)), _const_spec((D_MODEL, Q_WIDTH + D_MODEL)),
                tab_spec, tab_spec]
    out_specs = [pl.BlockSpec((1, KV_HEADS, TM, QW), lambda b, i: (b, 0, i, 0))] * N_GROUPS + [x_spec]
    out_shape = ([jax.ShapeDtypeStruct((bsz, KV_HEADS, seq, QW), dt) for dt in GROUP_DTYPES]
                 + [jax.ShapeDtypeStruct((bsz, seq, D_MODEL), F32)])
    return x_spec, in_specs, out_specs, out_shape


def _q_proj_prompt(x, g, w_in, cos_q, sin_q):
    bsz, seq, _ = x.shape
    x_spec, in_specs, out_specs, out_shape = _q_specs(bsz, seq)
    return pl.pallas_call(
        _q_prompt_kernel,
        grid=(bsz, seq // TM),
        in_specs=[x_spec] + in_specs,
        out_specs=out_specs,
        out_shape=out_shape,
        compiler_params=_params(("arbitrary", "arbitrary")),
        name="q_proj_prompt",
    )(x, g, w_in, cos_q, sin_q)


def _block_runs(g, blk):
    if g == 2:
        return [(blk * BAND, BAND)]
    if g == 1:
        r4, c = blk
        return [((4 * mm + r4) * BAND + 32 * c, 32) for mm in range(4)]
    return [(r * BAND + 8 * blk, 8) for r in range(NRES)]


def _block_pos(g):
    a = jnp.arange(BAND)
    if g == 2:
        return a
    if g == 1:
        return 4 * (a % 32) + a // 32
    return NRES * (a % 8) + a // 8


def _band_bias():
    out = []
    for g in range(N_GROUPS):
        pos = _block_pos(g)
        qpos = jnp.tile(pos, Q_PER_KV)[:, None]
        ok_prev = pos[None, :] >= qpos
        ok_cur = pos[None, :] <= qpos
        neg = jnp.full(ok_cur.shape, NEG, F32)
        cur = jnp.where(ok_cur, 0.0, neg)
        out.append(jnp.concatenate([jnp.where(ok_prev, 0.0, neg), cur], axis=1))
        out.append(jnp.concatenate([neg, cur], axis=1))
    return jnp.stack(out)


def _attn_prompt_kernel(q0_ref, q1_ref, q2_ref, kv0_ref, kv1_ref, kv2_ref, bias_ref, o_ref,
                        acc_ref, m_ref, l_ref, biasf_ref, p0_ref, p1_ref, p2_ref):
    q_refs = (q0_ref, q1_ref, q2_ref)
    kv_refs = (kv0_ref, kv1_ref, kv2_ref)
    prev_refs = (p0_ref, p1_ref, p2_ref)
    prev_rows = (8, 32, BAND)
    first = pl.program_id(2) == 0

    @pl.when(first)
    def _():
        for p_ref in prev_refs:
            p_ref[...] = jnp.zeros_like(p_ref)

    for g in range(N_GROUPS):
        biasf_ref[g] = jnp.where(first, bias_ref[2 * g + 1], bias_ref[2 * g])
    ones = jnp.ones((2 * BAND, HEAD_DIM), BF16)
    m_rows = Q_PER_KV * BAND

    def gather(read, runs):
        return jnp.concatenate([read(rs, n) for rs, n in runs], axis=0)

    def heads_state(ref, runs):
        return jnp.concatenate(
            [gather(lambda rs, n, j=j: ref[j, rs:rs + n, :], runs) for j in range(Q_PER_KV)], axis=0)

    def scatter_state(ref, val, runs):
        for j in range(Q_PER_KV):
            off = j * BAND
            for rs, n in runs:
                ref[j, rs:rs + n, :] = val[off:off + n]
                off += n

    for g in range(N_GROUPS):
        q_ref, kv_ref, p_ref = q_refs[g], kv_refs[g], prev_refs[g]
        if g == 2:
            blocks = [(r, None) for r in range(NRES)]
        elif g == 1:
            blocks = [((r4, c), (r4, c - 1) if c else None) for r4 in range(4) for c in range(4)]
        else:
            blocks = [(c, c - 1 if c else None) for c in range(NRES)]
        for blk, prev_blk in blocks:
            runs = _block_runs(g, blk)
            q4 = jnp.concatenate(
                [gather(lambda rs, n, j=j: q_ref[0, 0, rs:rs + n, j * HEAD_DIM:(j + 1) * HEAD_DIM],
                        runs) for j in range(Q_PER_KV)], axis=0).astype(BF16)
            if prev_blk is None:
                pr = prev_rows[g]
                if g == 2:
                    pruns = [(blk * BAND, BAND)]
                elif g == 1:
                    pruns = [((4 * mm + blk[0]) * pr, pr) for mm in range(4)]
                else:
                    pruns = [(r * pr, pr) for r in range(NRES)]
                read_prev = lambda cols: gather(lambda rs, n: p_ref[rs:rs + n, cols], pruns)
                bias = biasf_ref[g]
            else:
                pruns = _block_runs(g, prev_blk)
                read_prev = lambda cols: gather(lambda rs, n: kv_ref[0, 0, rs:rs + n, cols], pruns)
                bias = bias_ref[2 * g]
            read_cur = lambda cols: gather(lambda rs, n: kv_ref[0, 0, rs:rs + n, cols], runs)
            kcols, vcols = slice(0, HEAD_DIM), slice(HEAD_DIM, KW)
            k_cat = jnp.concatenate([read_prev(kcols), read_cur(kcols)], axis=0).astype(BF16)
            v_cat = jnp.concatenate([read_prev(vcols), read_cur(vcols)], axis=0).astype(BF16)

            s = _dot_nt(q4, k_cat) + bias
            m_new = jnp.broadcast_to(jnp.max(s, axis=-1, keepdims=True), (m_rows, HEAD_DIM))
            if g > 0:
                m_old = heads_state(m_ref, runs)
                m_new = jnp.maximum(m_old, m_new)
                alpha = jnp.exp2(m_old - m_new)
            p = jnp.exp2(jnp.concatenate([s[:, :BAND] - m_new, s[:, BAND:] - m_new], axis=1))
            ov = _dot(p.astype(BF16), jnp.concatenate([v_cat, ones], axis=1))
            acc_new, l_new = ov[:, :HEAD_DIM], ov[:, HEAD_DIM:]
            if g > 0:
                acc_new = alpha * heads_state(acc_ref, runs) + acc_new
                l_new = alpha * heads_state(l_ref, runs) + l_new
            if g < N_GROUPS - 1:
                scatter_state(acc_ref, acc_new, runs)
                scatter_state(m_ref, m_new, runs)
                scatter_state(l_ref, l_new, runs)
            else:
                o = (acc_new / l_new).astype(BF16)
                (rs, n), = runs
                for j in range(Q_PER_KV):
                    o_ref[0, rs:rs + n, j * HEAD_DIM:(j + 1) * HEAD_DIM] = o[j * BAND:(j + 1) * BAND]

    for g in range(N_GROUPS):
        pr = prev_rows[g]
        for r in range(NRES):
            prev_refs[g][r * pr:(r + 1) * pr, :] = kv_refs[g][0, 0, (r + 1) * BAND - pr:(r + 1) * BAND, :]


def _attn_prompt(qs, kvs, bias):
    bsz, _, seq, _ = qs[0].shape
    blk = lambda width: pl.BlockSpec((1, 1, TA, width), lambda b, h, i: (b, h, i, 0))
    return pl.pallas_call(
        _attn_prompt_kernel,
        grid=(bsz, KV_HEADS, seq // TA),
        in_specs=[blk(QW)] * N_GROUPS + [blk(KW)] * N_GROUPS + [_const_spec(bias.shape)],
        out_specs=pl.BlockSpec((1, TA, QW), lambda b, h, i: (b, i, h)),
        out_shape=jax.ShapeDtypeStruct((bsz, seq, KV_HEADS * QW), BF16),
        scratch_shapes=[
            pltpu.VMEM((Q_PER_KV, TA, HEAD_DIM), F32),
            pltpu.VMEM((Q_PER_KV, TA, HEAD_DIM), F32),
            pltpu.VMEM((Q_PER_KV, TA, HEAD_DIM), F32),
            pltpu.VMEM((N_GROUPS, Q_PER_KV * BAND, 2 * BAND), F32),
            pltpu.VMEM((NRES * 8, KW), GROUP_DTYPES[0]),
            pltpu.VMEM((NRES * 32, KW), GROUP_DTYPES[1]),
            pltpu.VMEM((NRES * BAND, KW), GROUP_DTYPES[2]),
        ],
        compiler_params=_params(("arbitrary", "arbitrary", "arbitrary")),
        name="attn_prompt",
    )(*qs, *kvs, bias)


def _out_prompt_kernel(o_ref, z_ref, x_ref, wout_ref, gpost_ref, y_ref, slab_ref):
    y = (_load_set(o_ref).astype(F32) * _silu(_load_set(z_ref))).astype(BF16)
    res = _load_set(x_ref) + _rms(_dot(y, wout_ref[...]), gpost_ref[...])
    y_ref[0] = _to_token_order(slab_ref, res)


def _out_q_prompt_kernel(o_ref, z_ref, x_ref, wout_ref, gpost_ref, gpre_ref, win_ref, cos_ref, sin_ref,
                         y_ref, q0_ref, q1_ref, q2_ref, zn_ref):
    y = (o_ref[0].astype(F32) * _silu(z_ref[0])).astype(BF16)
    x_new = x_ref[0] + _rms(_dot(y, wout_ref[...]), gpost_ref[...])
    y_ref[0] = x_new
    _q_project(x_new, gpre_ref, win_ref, cos_ref, sin_ref, (q0_ref, q1_ref, q2_ref), zn_ref)


def _out_q_proj_prompt(o, z, x, w_out, g_post, g_pre_next, w_in_next, cos_q, sin_q):
    bsz, seq, _ = x.shape
    x_spec, q_in_specs, q_out_specs, q_out_shape = _q_specs(bsz, seq)
    return pl.pallas_call(
        _out_q_prompt_kernel,
        grid=(bsz, seq // TM),
        in_specs=[x_spec] * 3 + [_const_spec((D_MODEL, D_MODEL)), _const_spec((1, D_MODEL))]
        + q_in_specs,
        out_specs=[x_spec] + q_out_specs,
        out_shape=[jax.ShapeDtypeStruct((bsz, seq, D_MODEL), F32)] + q_out_shape,
        compiler_params=_params(("arbitrary", "arbitrary")),
        name="out_q_proj_prompt",
    )(o, z, x, w_out, g_post, g_pre_next, w_in_next, cos_q, sin_q)


def _out_proj_prompt(o, z, x, w_out, g_post):
    bsz, seq, _ = x.shape
    view = lambda a: a.reshape(bsz, seq // TA, NRES, BAND, D_MODEL)
    return pl.pallas_call(
        _out_prompt_kernel,
        grid=(bsz, seq // TM),
        in_specs=[_set_spec(D_MODEL)] * 3
        + [_const_spec((D_MODEL, D_MODEL)), _const_spec((1, D_MODEL))],
        out_specs=pl.BlockSpec((1, TM, D_MODEL), lambda b, i: (b, i, 0)),
        out_shape=jax.ShapeDtypeStruct((bsz, seq, D_MODEL), F32),
        scratch_shapes=[pltpu.VMEM((D_MODEL // HEAD_DIM, 2, TM, HEAD_DIM), F32)],
        compiler_params=_params(("arbitrary", "arbitrary")),
        name="out_proj_prompt",
    )(view(o), view(z), view(x), w_out, g_post)


def _proj_sample_kernel(x_ref, g_ref, w_ref, cos_ref, sin_ref, y_ref, *, rope_heads):
    h = _rms(x_ref[...], g_ref[...]).astype(BF16)
    y = _dot(h, w_ref[...])
    cos = cos_ref[...]
    sin = sin_ref[...]
    for hd in range(rope_heads):
        cs = slice(hd * HEAD_DIM, (hd + 1) * HEAD_DIM)
        y_ref[:, cs] = _rope(y[:, cs], cos, sin)
    rest = rope_heads * HEAD_DIM
    y_ref[:, rest:] = y[:, rest:]


def _proj_sample(x, g, w, cos, sin, rope_heads):
    n = x.shape[0]
    return pl.pallas_call(
        functools.partial(_proj_sample_kernel, rope_heads=rope_heads),
        out_shape=jax.ShapeDtypeStruct((n, w.shape[1]), F32),
        compiler_params=_params(),
        name=f"proj_sample_{rope_heads}",
    )(x, g, w, cos, sin)


def _attn_sample_body(qz, kvn, window_kv, y_ref):
    pieces = []
    for kh in range(KV_HEADS):
        scores, s_new, vals, v_new = [], [], [], []
        for g in range(N_GROUPS):
            c = (g * HEADS + kh * Q_PER_KV) * HEAD_DIM
            q4 = jnp.concatenate(
                [qz[:, c + j * HEAD_DIM:c + (j + 1) * HEAD_DIM] for j in range(Q_PER_KV)], axis=0)
            q8 = jnp.concatenate([q4, q4], axis=0)
            k_rows, v_rows = window_kv(g, kh)
            kc = (g * KV_HEADS + kh) * HEAD_DIM
            k_n = kvn[:, kc:kc + HEAD_DIM]
            v_n = kvn[:, KV_HALF + kc:KV_HALF + kc + HEAD_DIM]
            scores.append(_dot_nt(q8.astype(BF16), k_rows))
            s_new.append(jnp.sum(q8 * k_n, axis=-1, keepdims=True))
            vals.append(v_rows)
            v_new.append(v_n)
        m = functools.reduce(jnp.maximum,
                             [jnp.max(s, axis=-1, keepdims=True) for s in scores] + s_new)
        ps = [jnp.exp(s - m) for s in scores]
        pn = [jnp.exp(s - m) for s in s_new]
        l = sum(jnp.sum(p, axis=-1, keepdims=True) for p in ps) + sum(pn)
        o = sum(_dot(ps[g].astype(BF16), vals[g]) + pn[g] * v_new[g] for g in range(N_GROUPS)) / l
        pieces += [o[j:j + 1, :] for j in range(Q_PER_KV)]
    o_row = jnp.concatenate(pieces, axis=-1)
    y_ref[0] = o_row * _silu(qz[:, Q_WIDTH:])


def _attn_sample_first_kernel(qz_ref, kvn_ref, c0_ref, c1_ref, c2_ref,
                              y_ref, oc0_ref, oc1_ref, oc2_ref, win_ref):
    c_refs = (c0_ref, c1_ref, c2_ref)
    kvn = kvn_ref[0]
    nr = 2 * KV_HEADS

    def window_kv(g, kh):
        stride = nr * DILATIONS[g]
        k_rows = c_refs[g][0, pl.ds(kh, BAND, stride=stride), :].astype(BF16)
        v_rows = c_refs[g][0, pl.ds(KV_HEADS + kh, BAND, stride=stride), :].astype(BF16)
        slot = 2 * (g * KV_HEADS + kh)
        win_ref[0, slot] = k_rows
        win_ref[0, slot + 1] = v_rows
        return k_rows, v_rows

    _attn_sample_body(qz_ref[0], kvn, window_kv, y_ref)
    for g, oc_ref in enumerate((oc0_ref, oc1_ref, oc2_ref)):
        keep = (WINDOWS[g] - 1) * nr
        oc_ref[0, 0:keep, :] = c_refs[g][0, nr:nr + keep, :]
        new_rows = [kvn[:, half + (g * KV_HEADS + hh) * HEAD_DIM:half + (g * KV_HEADS + hh + 1) * HEAD_DIM]
                    for half in (0, KV_HALF) for hh in range(KV_HEADS)]
        oc_ref[0, keep:keep + nr, :] = jnp.concatenate(new_rows, axis=0)


def _attn_sample_next_kernel(qz_ref, kvn_ref, win_ref, y_ref):
    def window_kv(g, kh):
        slot = 2 * (g * KV_HEADS + kh)
        return win_ref[0, slot], win_ref[0, slot + 1]

    _attn_sample_body(qz_ref[0], kvn_ref[0], window_kv, y_ref)


def _attn_sample_first(qz, kv_new, caches):
    n = qz.shape[0]
    row_spec = lambda a: pl.BlockSpec((1, 1, a.shape[-1]), lambda b: (b, 0, 0))
    cache_specs = [pl.BlockSpec((1, c.shape[1], HEAD_DIM), lambda b: (b, 0, 0)) for c in caches]
    win_shape = (n, 2 * N_GROUPS * KV_HEADS, BAND, HEAD_DIM)
    return pl.pallas_call(
        _attn_sample_first_kernel,
        grid=(n,),
        in_specs=[row_spec(qz), row_spec(kv_new)] + cache_specs,
        out_specs=[pl.BlockSpec((1, 1, D_MODEL), lambda b: (b, 0, 0))] + cache_specs
        + [pl.BlockSpec((1,) + win_shape[1:], lambda b: (b, 0, 0, 0))],
        out_shape=[jax.ShapeDtypeStruct((n, 1, D_MODEL), F32)]
        + [jax.ShapeDtypeStruct(c.shape, F32) for c in caches]
        + [jax.ShapeDtypeStruct(win_shape, BF16)],
        compiler_params=_params(("arbitrary",)),
        name="attn_sample_first",
    )(qz, kv_new, *caches)


def _attn_sample_next(qz, kv_new, win):
    n = qz.shape[0]
    row_spec = lambda a: pl.BlockSpec((1, 1, a.shape[-1]), lambda b: (b, 0, 0))
    return pl.pallas_call(
        _attn_sample_next_kernel,
        grid=(n,),
        in_specs=[row_spec(qz), row_spec(kv_new),
                  pl.BlockSpec((1,) + win.shape[1:], lambda b: (b, 0, 0, 0))],
        out_specs=pl.BlockSpec((1, 1, D_MODEL), lambda b: (b, 0, 0)),
        out_shape=jax.ShapeDtypeStruct((n, 1, D_MODEL), F32),
        compiler_params=_params(("arbitrary",)),
        name="attn_sample_next",
    )(qz, kv_new, win)


def _out_sample_kernel(y_ref, x_ref, wout_ref, gpost_ref, o_ref):
    out = _dot(y_ref[...].astype(BF16), wout_ref[...])
    o_ref[...] = x_ref[...] + _rms(out, gpost_ref[...])


def _out_sample(y, x, w_out, g_post):
    return pl.pallas_call(
        _out_sample_kernel,
        out_shape=jax.ShapeDtypeStruct(x.shape, F32),
        compiler_params=_params(),
        name="out_sample",
    )(y, x, w_out, g_post)


def _rope_tables(pos):
    half = HEAD_DIM // 2
    inv_freq = ROPE_THETA ** (-jnp.arange(half, dtype=F32) * 2.0 / HEAD_DIM)
    ang = pos.astype(F32)[:, None] * inv_freq[None, :]
    cos = jnp.cos(ang)
    sin = jnp.sin(ang)
    return jnp.concatenate([cos, cos], axis=-1), jnp.concatenate([-sin, sin], axis=-1)


def kernel(x_prompt, x_sample, state_conv, cache_kv_g0, cache_kv_g1, cache_kv_g2,
           a_norm_pre, a_w_in, a_conv, a_w_out, a_norm_post,
           kv_norm, w_kv, b_norm_pre, b_w_in, b_w_out, b_norm_post):
    bsz, seq, _ = x_prompt.shape
    n_s = x_sample.shape[0]
    n_a = a_w_in.shape[0]
    n_b = b_w_in.shape[0]
    assert x_sample.shape[1] == 1 and seq % TA == 0 and n_a >= 1 and n_b >= 1

    pos_rm = jnp.arange(seq).reshape(seq // TA, BAND, NRES).transpose(0, 2, 1).reshape(seq)
    cos_p, sin_p = _rope_tables(pos_rm)
    cos_s, sin_s = _rope_tables(PAST_LEN + jnp.arange(1))
    row = lambda v: v.reshape(1, -1)

    a_w_in16 = a_w_in.astype(BF16)
    a_w_out16 = a_w_out.astype(BF16)
    w_kv16 = w_kv.astype(BF16)
    b_w_in16 = b_w_in.astype(BF16)
    b_w_out16 = b_w_out.astype(BF16)

    hp = x_prompt
    hs = x_sample.reshape(n_s, D_MODEL)
    conv_p, conv_s = [], []
    for layer in range(n_a):
        args = (row(a_norm_pre[layer]), a_w_in16[layer], a_conv[layer], a_w_out16[layer],
                row(a_norm_post[layer]))
        hp, st_p = _conv_layer_prompt(hp, *args, residue_major_out=(layer == n_a - 1))
        past0 = state_conv[layer, :, 0]
        past1 = state_conv[layer, :, 1]
        hs, u_s = _conv_layer_sample(hs, past0, past1, *args)
        conv_p.append(st_p)
        conv_s.append(jnp.stack([past1, u_s], axis=1))

    tab5 = lambda t: t.reshape(seq // TA, NRES, BAND, HEAD_DIM)
    kv_p, cs = _kv_proj_prompt(hp, row(kv_norm), w_kv16, tab5(cos_p), tab5(sin_p))
    kv_new_p = [c.reshape(bsz, c.shape[1], 2, KV_HEADS, HEAD_DIM) for c in cs]
    hp = hp.reshape(bsz, seq, D_MODEL)
    bias = _band_bias()
    kv_s = _proj_sample(hs, row(kv_norm), w_kv16, cos_s, sin_s, N_GROUPS * KV_HEADS)

    caches = [c.reshape(n_s, c.shape[1] * 2 * KV_HEADS, HEAD_DIM)
              for c in (cache_kv_g0, cache_kv_g1, cache_kv_g2)]
    kv_new_s = None
    cos_q = cos_p * (SCALE * LOG2E)
    sin_q = sin_p * (SCALE * LOG2E)
    *q, z = _q_proj_prompt(hp, row(b_norm_pre[0]), b_w_in16[0], cos_q, sin_q)
    for j in range(n_b):
        g_pre = row(b_norm_pre[j])
        g_post = row(b_norm_post[j])
        o = _attn_prompt(q, kv_p, bias)
        if j + 1 < n_b:
            hp, *q, z = _out_q_proj_prompt(o, z, hp, b_w_out16[j], g_post, row(b_norm_pre[j + 1]),
                                           b_w_in16[j + 1], cos_q, sin_q)
        else:
            hp = _out_proj_prompt(o, z, hp, b_w_out16[j], g_post)
        qz = _proj_sample(hs, g_pre, b_w_in16[j], cos_s * SCALE, sin_s * SCALE, N_GROUPS * HEADS)
        qz3 = qz.reshape(n_s, 1, -1)
        if j == 0:
            y_s, *shifted, win = _attn_sample_first(qz3, kv_s.reshape(n_s, 1, -1), caches)
            kv_new_s = [c.reshape(n_s, WINDOWS[g], 2, KV_HEADS, HEAD_DIM)
                        for g, c in enumerate(shifted)]
        else:
            y_s = _attn_sample_next(qz3, kv_s.reshape(n_s, 1, -1), win)
        hs = _out_sample(y_s.reshape(n_s, D_MODEL), hs, b_w_out16[j], g_post)

    return (hp, hs.reshape(n_s, 1, D_MODEL), jnp.stack(conv_p, axis=0),
            kv_new_p[0], kv_new_p[1], kv_new_p[2],
            jnp.stack(conv_s, axis=0), kv_new_s[0], kv_new_s[1], kv_new_s[2])
```

```python
import functools

import jax
import jax.numpy as jnp
from jax import lax
from jax.experimental import pallas as pl
from jax.experimental.pallas import tpu as pltpu

D_MODEL = 1024
HEAD_DIM = 128
N_GROUPS = 3
WINDOWS = (128, 512, 2048)
DILATIONS = (1, 4, 16)
HEADS = 8
KV_HEADS = 2
Q_PER_KV = HEADS // KV_HEADS
Q_WIDTH = N_GROUPS * HEADS * HEAD_DIM
KV_HALF = N_GROUPS * KV_HEADS * HEAD_DIM
PAST_LEN = 8192
ROPE_THETA = 10000.0
EPS = 1e-6
SCALE = HEAD_DIM ** -0.5
NEG = -1e30
LOG2E = 1.4426950408889634

BAND = 128
NRES = 16
TA = NRES * BAND
TM = 512
SUB = TM // NRES
CH = 256
QW = Q_PER_KV * HEAD_DIM
KW = 2 * HEAD_DIM
VMEM_LIMIT = 54 * 1024 * 1024

F32 = jnp.float32
BF16 = jnp.bfloat16
GROUP_DTYPES = (F32, BF16, BF16)


def _dot(a, b):
    return jnp.dot(a, b, preferred_element_type=F32)


def _dot_nt(a, b):
    return lax.dot_general(a, b, (((1,), (1,)), ((), ())), preferred_element_type=F32)


def _rms(x, g):
    ms = jnp.mean(x * x, axis=-1, keepdims=True)
    return x * lax.rsqrt(ms + EPS) * g


def _silu(z):
    return z * jax.nn.sigmoid(z)


def _rope(x, cos, sin_signed):
    return x * cos + pltpu.roll(x, HEAD_DIM // 2, 1) * sin_signed


def _params(sem=None, flags=None):
    return pltpu.CompilerParams(dimension_semantics=sem, vmem_limit_bytes=VMEM_LIMIT, flags=flags)


def _const_spec(shape):
    zeros = (0,) * len(shape)
    return pl.BlockSpec(shape, lambda *_: zeros, pipeline_mode=pl.Buffered(1))


def _split_residues(slab_ref, val):
    q = TM // 4
    slab_ref[0] = val
    for m in range(4):
        slab_ref[1, m * q:(m + 1) * q, :] = slab_ref[0, pl.ds(m, q, stride=4), :]
    return {4 * m2 + m: slab_ref[1, pl.ds(m * q + m2, SUB, stride=4), :]
            for m in range(4) for m2 in range(4)}


def _merge_residues(slab_ref, pieces):
    q = TM // 4
    for m in range(4):
        for m2 in range(4):
            slab_ref[1, pl.ds(m * q + m2, SUB, stride=4), :] = pieces[4 * m2 + m]
    for m in range(4):
        slab_ref[0, pl.ds(m, q, stride=4), :] = slab_ref[1, m * q:(m + 1) * q, :]
    return slab_ref[0]


def _to_residue_major(slab_ref, val):
    ncol = val.shape[1] // HEAD_DIM
    cols = [_split_residues(slab_ref.at[c], val[:, c * HEAD_DIM:(c + 1) * HEAD_DIM])
            for c in range(ncol)]
    return [jnp.concatenate([cols[c][r] for c in range(ncol)], axis=-1) for r in range(NRES)]


def _to_token_order(slab_ref, val):
    ncol = val.shape[1] // HEAD_DIM
    cols = [_merge_residues(slab_ref.at[c],
                            {r: val[r * SUB:(r + 1) * SUB, c * HEAD_DIM:(c + 1) * HEAD_DIM]
                             for r in range(NRES)})
            for c in range(ncol)]
    return jnp.concatenate(cols, axis=-1)


def _set_spec(width):
    per = TA // TM
    return pl.BlockSpec((1, 1, NRES, SUB, width), lambda b, i: (b, i // per, 0, i % per, 0))


def _load_set(ref):
    return jnp.concatenate([ref[0, 0, r] for r in range(NRES)], axis=0)


def _conv_project_chunk(h, win_ref, j):
    return [_dot(h, win_ref[:, k * D_MODEL + j * CH:k * D_MODEL + (j + 1) * CH]) for k in range(4)]


def _conv_mix_chunk(proj, wconv_ref, j, u_prev1, u_prev2):
    cs = slice(j * CH, (j + 1) * CH)
    b_gate, c_gate, u_in, z = proj
    u = c_gate * u_in
    conv = (u_prev2(u, cs) * wconv_ref[0:1, cs] + u_prev1(u, cs) * wconv_ref[1:2, cs]
            + u * wconv_ref[2:3, cs])
    return u, b_gate * conv * _silu(z)


def _conv_chunks(h, win_ref, wconv_ref, wout_ref, u_prev1, u_prev2, keep_u):
    nch = D_MODEL // CH
    acc = jnp.zeros((h.shape[0], D_MODEL), F32)
    proj = _conv_project_chunk(h, win_ref, 0)
    for j in range(nch):
        nxt = _conv_project_chunk(h, win_ref, j + 1) if j + 1 < nch else None
        u, y = _conv_mix_chunk(proj, wconv_ref, j, u_prev1, u_prev2)
        keep_u(u, slice(j * CH, (j + 1) * CH))
        acc = acc + _dot(y.astype(BF16), wout_ref[j * CH:(j + 1) * CH, :])
        proj = nxt
    return acc


def _cache_shift_copies(step, per, cache_refs, new_refs, out_refs, sem_ref):
    rows = pl.ds(step * per, per)
    copies = []
    for g in range(N_GROUPS):
        w = WINDOWS[g]
        copies.append(pltpu.make_async_copy(cache_refs[g].at[rows, pl.ds(1, w - 1)],
                                            out_refs[g].at[rows, pl.ds(0, w - 1)], sem_ref.at[g]))
        copies.append(pltpu.make_async_copy(new_refs[g].at[rows],
                                            out_refs[g].at[rows, pl.ds(w - 1, 1)],
                                            sem_ref.at[N_GROUPS + g]))
    return copies


def _conv_prompt_kernel(*refs, residue_major_out, shift_rows):
    x_ref, gpre_ref, win_ref, wconv_ref, wout_ref, gpost_ref = refs[:6]
    refs = refs[6:]
    if shift_rows:
        cache_refs, new_refs, refs = refs[:3], refs[3:6], refs[6:]
    (y_ref, st_ref), refs = refs[:2], refs[2:]
    if shift_rows:
        out_cache_refs, refs = refs[:3], refs[3:]
    tail_ref, refs = refs[0], refs[1:]
    if residue_major_out:
        slab_ref, refs = refs[0], refs[1:]
    if shift_rows:
        sem_ref, = refs
        step = pl.program_id(0) * pl.num_programs(1) + pl.program_id(1)
        n_copy_steps = cache_refs[0].shape[0] // shift_rows
        copies = _cache_shift_copies(step, shift_rows, cache_refs, new_refs, out_cache_refs, sem_ref)

        @pl.when(step < n_copy_steps)
        def _():
            for cp in copies:
                cp.start()

    @pl.when(pl.program_id(1) == 0)
    def _():
        tail_ref[...] = jnp.zeros_like(tail_ref)

    x = x_ref[0]
    h = _rms(x, gpre_ref[...]).astype(BF16)
    row = lax.broadcasted_iota(jnp.int32, (TM, CH), 0)

    def prev1(u, cs):
        return jnp.where(row == 0, tail_ref[1:2, cs], pltpu.roll(u, 1, 0))

    def prev2(u, cs):
        return jnp.where(row == 0, tail_ref[0:1, cs],
                         jnp.where(row == 1, tail_ref[1:2, cs], pltpu.roll(u, 2, 0)))

    def keep_tail(u, cs):
        tail_ref[0:2, cs] = u[TM - 2:TM, :]

    acc = _conv_chunks(h, win_ref, wconv_ref, wout_ref, prev1, prev2, keep_tail)
    out = x + _rms(acc, gpost_ref[...])
    if residue_major_out:
        for r, piece in enumerate(_to_residue_major(slab_ref, out)):
            y_ref[0, 0, r] = piece
    else:
        y_ref[0] = out
    st_ref[0] = tail_ref[0:2, :]

    if shift_rows:
        @pl.when(step < n_copy_steps)
        def _():
            for cp in copies:
                cp.wait()


def _conv_layer_prompt(x, g_pre, w_in, w_conv, w_out, g_post, residue_major_out, shift=None):
    bsz, seq, _ = x.shape
    n_steps = bsz * (seq // TM)
    scratch = [pltpu.VMEM((8, D_MODEL), F32)]
    if residue_major_out:
        y_spec = _set_spec(D_MODEL)
        y_shape = jax.ShapeDtypeStruct((bsz, seq // TA, NRES, BAND, D_MODEL), F32)
        scratch.append(pltpu.VMEM((D_MODEL // HEAD_DIM, 2, TM, HEAD_DIM), F32))
    else:
        y_spec = pl.BlockSpec((1, TM, D_MODEL), lambda b, i: (b, i, 0))
        y_shape = jax.ShapeDtypeStruct((bsz, seq, D_MODEL), F32)
    any_spec = pl.BlockSpec(memory_space=pl.ANY)
    extra_in, extra_specs, extra_out, shift_rows = [], [], [], 0
    if shift is not None:
        caches, new_rows = shift
        n_s = caches[0].shape[0]
        shift_rows = -(-n_s // n_steps)
        assert n_s % shift_rows == 0
        extra_in = list(caches) + list(new_rows)
        extra_specs = [any_spec] * len(extra_in)
        extra_out = [jax.ShapeDtypeStruct(c.shape, c.dtype) for c in caches]
        scratch.append(pltpu.SemaphoreType.DMA((2 * N_GROUPS,)))
    return pl.pallas_call(
        functools.partial(_conv_prompt_kernel, residue_major_out=residue_major_out,
                          shift_rows=shift_rows),
        grid=(bsz, seq // TM),
        in_specs=[
            pl.BlockSpec((1, TM, D_MODEL), lambda b, i: (b, i, 0)),
            _const_spec((1, D_MODEL)),
            _const_spec((D_MODEL, 4 * D_MODEL)),
            _const_spec((3, D_MODEL)),
            _const_spec((D_MODEL, D_MODEL)),
            _const_spec((1, D_MODEL)),
        ] + extra_specs,
        out_specs=[y_spec, pl.BlockSpec((1, 2, D_MODEL), lambda b, i: (b, 0, 0))]
        + [any_spec] * len(extra_out),
        out_shape=[y_shape, jax.ShapeDtypeStruct((bsz, 2, D_MODEL), F32)] + extra_out,
        scratch_shapes=scratch,
        compiler_params=_params(("arbitrary", "arbitrary")),
        name="conv_layer_prompt",
    )(x, g_pre, w_in, w_conv, w_out, g_post, *extra_in)


def _conv_sample_kernel(x_ref, p0_ref, p1_ref, gpre_ref, win_ref, wconv_ref, wout_ref, gpost_ref,
                        y_ref, u_ref):
    x = x_ref[...]
    h = _rms(x, gpre_ref[...]).astype(BF16)
    def keep_u(u, cs):
        u_ref[:, cs] = u

    acc = _conv_chunks(h, win_ref, wconv_ref, wout_ref,
                       lambda u, cs: p1_ref[:, cs], lambda u, cs: p0_ref[:, cs], keep_u)
    y_ref[...] = x + _rms(acc, gpost_ref[...])


def _conv_layer_sample(x, past0, past1, g_pre, w_in, w_conv, w_out, g_post):
    n = x.shape[0]
    return pl.pallas_call(
        _conv_sample_kernel,
        out_shape=[jax.ShapeDtypeStruct((n, D_MODEL), F32)] * 2,
        compiler_params=_params(),
        name="conv_layer_sample",
    )(x, past0, past1, g_pre, w_in, w_conv, w_out, g_post)


def _kv_prompt_kernel(x_ref, g_ref, w_ref, cos_ref, sin_ref, kv0_ref, kv1_ref, kv2_ref,
                      c0_ref, c1_ref, c2_ref, slab_ref, *, first_cache_tile):
    h = _rms(_load_set(x_ref), g_ref[...]).astype(BF16)
    cos = jnp.concatenate([cos_ref[0, r] for r in range(NRES)], axis=0)
    sin = jnp.concatenate([sin_ref[0, r] for r in range(NRES)], axis=0)
    kv = _dot(h, w_ref[...])
    kv_refs = (kv0_ref, kv1_ref, kv2_ref)
    fulls = []
    for g in range(N_GROUPS):
        ks, vs = [], []
        for hh in range(KV_HEADS):
            c = (g * KV_HEADS + hh) * HEAD_DIM
            k = _rope(kv[:, c:c + HEAD_DIM], cos, sin)
            v = kv[:, KV_HALF + c:KV_HALF + c + HEAD_DIM]
            both = jnp.concatenate([k, v], axis=-1).astype(GROUP_DTYPES[g])
            for r in range(NRES):
                kv_refs[g][0, hh, 0, r] = both[r * SUB:(r + 1) * SUB]
            ks.append(k)
            vs.append(v)
        fulls.append(jnp.concatenate(ks + vs, axis=-1))

    @pl.when(pl.program_id(1) >= first_cache_tile)
    def _():
        for g, c_ref in enumerate((c0_ref, c1_ref, c2_ref)):
            rows = min(TM, WINDOWS[g])
            c_ref[0] = _to_token_order(slab_ref, fulls[g])[TM - rows:, :]


def _kv_proj_prompt(x5, g, w_kv, cos5, sin5):
    bsz, nta = x5.shape[:2]
    seq = nta * TA
    nt = seq // TM
    per = TA // TM
    cache_specs, cache_shapes, firsts = [], [], []
    for grp in range(N_GROUPS):
        w = min(WINDOWS[grp], seq)
        rows = min(TM, w)
        first = nt - w // rows
        firsts.append(first)
        cache_specs.append(pl.BlockSpec(
            (1, rows, 4 * HEAD_DIM), lambda b, i, first=first: (b, jnp.maximum(i - first, 0), 0)))
        cache_shapes.append(jax.ShapeDtypeStruct((bsz, w, 4 * HEAD_DIM), F32))
    kv_spec = pl.BlockSpec((1, KV_HEADS, 1, NRES, SUB, KW),
                           lambda b, i: (b, 0, i // per, 0, i % per, 0))
    kv_shapes = [jax.ShapeDtypeStruct((bsz, KV_HEADS, nta, NRES, BAND, KW), dt)
                 for dt in GROUP_DTYPES]
    tab_spec = pl.BlockSpec((1, NRES, SUB, HEAD_DIM), lambda b, i: (i // per, 0, i % per, 0))
    outs = pl.pallas_call(
        functools.partial(_kv_prompt_kernel, first_cache_tile=min(firsts)),
        grid=(bsz, nt),
        in_specs=[_set_spec(D_MODEL), _const_spec((1, D_MODEL)),
                  _const_spec((D_MODEL, 2 * KV_HALF)), tab_spec, tab_spec],
        out_specs=[kv_spec] * N_GROUPS + cache_specs,
        out_shape=kv_shapes + cache_shapes,
        scratch_shapes=[pltpu.VMEM((4, 2, TM, HEAD_DIM), F32)],
        compiler_params=_params(("arbitrary", "arbitrary")),
        name="kv_proj_prompt",
    )(x5, g, w_kv, cos5, sin5)
    kvs = [o.reshape(bsz, KV_HEADS, seq, KW) for o in outs[:N_GROUPS]]
    return kvs, outs[N_GROUPS:]


def _q_project(x, g_ref, w_ref, cos_ref, sin_ref, q_refs, z_ref):
    h = _rms(x, g_ref[...]).astype(BF16)
    cos = cos_ref[...]
    sin = sin_ref[...]
    units = [(g, kh) for g in range(N_GROUPS) for kh in range(KV_HEADS)]
    project = lambda u: _dot(h, w_ref[:, (u[0] * KV_HEADS + u[1]) * QW:(u[0] * KV_HEADS + u[1] + 1) * QW])
    cur = project(units[0])
    for n, (g, kh) in enumerate(units):
        nxt = project(units[n + 1]) if n + 1 < len(units) else _dot(h, w_ref[:, Q_WIDTH:])
        heads = [_rope(cur[:, j * HEAD_DIM:(j + 1) * HEAD_DIM], cos, sin) for j in range(Q_PER_KV)]
        q_refs[g][0, kh] = jnp.concatenate(heads, axis=-1).astype(GROUP_DTYPES[g])
        cur = nxt
    z_ref[0] = cur


def _q_prompt_kernel(x_ref, g_ref, w_ref, cos_ref, sin_ref, q0_ref, q1_ref, q2_ref, z_ref):
    _q_project(x_ref[0], g_ref, w_ref, cos_ref, sin_ref, (q0_ref, q1_ref, q2_ref), z_ref)


def _q_specs(bsz, seq):
    x_spec = pl.BlockSpec((1, TM, D_MODEL), lambda b, i: (b, i, 0))
    tab_spec = pl.BlockSpec((TM, HEAD_DIM), lambda b, i: (i, 0))
    in_specs = [_const_spec((1, D_MODEL)), _const_spec((D_MODEL, Q_WIDTH + D_MODEL)),
                tab_spec, tab_spec]
    out_specs = [pl.BlockSpec((1, KV_HEADS, TM, QW), lambda b, i: (b, 0, i, 0))] * N_GROUPS + [x_spec]
    out_shape = ([jax.ShapeDtypeStruct((bsz, KV_HEADS, seq, QW), dt) for dt in GROUP_DTYPES]
                 + [jax.ShapeDtypeStruct((bsz, seq, D_MODEL), F32)])
    return x_spec, in_specs, out_specs, out_shape


def _q_proj_prompt(x, g, w_in, cos_q, sin_q):
    bsz, seq, _ = x.shape
    x_spec, in_specs, out_specs, out_shape = _q_specs(bsz, seq)
    return pl.pallas_call(
        _q_prompt_kernel,
        grid=(bsz, seq // TM),
        in_specs=[x_spec] + in_specs,
        out_specs=out_specs,
        out_shape=out_shape,
        compiler_params=_params(("arbitrary", "arbitrary")),
        name="q_proj_prompt",
    )(x, g, w_in, cos_q, sin_q)


def _block_runs(g, blk):
    if g == 2:
        return [(blk * BAND, BAND)]
    if g == 1:
        r4, c = blk
        return [((4 * mm + r4) * BAND + 32 * c, 32) for mm in range(4)]
    return [(r * BAND + 8 * blk, 8) for r in range(NRES)]


def _block_pos(g):
    a = jnp.arange(BAND)
    if g == 2:
        return a
    if g == 1:
        return 4 * (a % 32) + a // 32
    return NRES * (a % 8) + a // 8


def _band_bias():
    out = []
    for g in range(N_GROUPS):
        pos = _block_pos(g)
        qpos = jnp.tile(pos, Q_PER_KV)[:, None]
        ok_prev = pos[None, :] >= qpos
        ok_cur = pos[None, :] <= qpos
        neg = jnp.full(ok_cur.shape, NEG, F32)
        cur = jnp.where(ok_cur, 0.0, neg)
        out.append(jnp.concatenate([jnp.where(ok_prev, 0.0, neg), cur], axis=1))
        out.append(jnp.concatenate([neg, cur], axis=1))
    return jnp.stack(out)


def _attn_prompt_kernel(q0_ref, q1_ref, q2_ref, kv0_ref, kv1_ref, kv2_ref, bias_ref, o_ref,
                        acc_ref, m_ref, l_ref, biasf_ref, p0_ref, p1_ref, p2_ref):
    q_refs = (q0_ref, q1_ref, q2_ref)
    kv_refs = (kv0_ref, kv1_ref, kv2_ref)
    prev_refs = (p0_ref, p1_ref, p2_ref)
    prev_rows = (8, 32, BAND)
    first = pl.program_id(2) == 0

    @pl.when(first)
    def _():
        for p_ref in prev_refs:
            p_ref[...] = jnp.zeros_like(p_ref)

    for g in range(N_GROUPS):
        biasf_ref[g] = jnp.where(first, bias_ref[2 * g + 1], bias_ref[2 * g])
    ones = jnp.ones((2 * BAND, HEAD_DIM), BF16)
    m_rows = Q_PER_KV * BAND

    def gather(read, runs):
        return jnp.concatenate([read(rs, n) for rs, n in runs], axis=0)

    def heads_state(ref, runs):
        return jnp.concatenate(
            [gather(lambda rs, n, j=j: ref[j, rs:rs + n, :], runs) for j in range(Q_PER_KV)], axis=0)

    def scatter_state(ref, val, runs):
        for j in range(Q_PER_KV):
            off = j * BAND
            for rs, n in runs:
                ref[j, rs:rs + n, :] = val[off:off + n]
                off += n

    for g in range(N_GROUPS):
        q_ref, kv_ref, p_ref = q_refs[g], kv_refs[g], prev_refs[g]
        if g == 2:
            blocks = [(r, None) for r in range(NRES)]
        elif g == 1:
            blocks = [((r4, c), (r4, c - 1) if c else None) for r4 in range(4) for c in range(4)]
        else:
            blocks = [(c, c - 1 if c else None) for c in range(NRES)]
        for blk, prev_blk in blocks:
            runs = _block_runs(g, blk)
            q4 = jnp.concatenate(
                [gather(lambda rs, n, j=j: q_ref[0, 0, rs:rs + n, j * HEAD_DIM:(j + 1) * HEAD_DIM],
                        runs) for j in range(Q_PER_KV)], axis=0).astype(BF16)
            if prev_blk is None:
                pr = prev_rows[g]
                if g == 2:
                    pruns = [(blk * BAND, BAND)]
                elif g == 1:
                    pruns = [((4 * mm + blk[0]) * pr, pr) for mm in range(4)]
                else:
                    pruns = [(r * pr, pr) for r in range(NRES)]
                read_prev = lambda cols: gather(lambda rs, n: p_ref[rs:rs + n, cols], pruns)
                bias = biasf_ref[g]
            else:
                pruns = _block_runs(g, prev_blk)
                read_prev = lambda cols: gather(lambda rs, n: kv_ref[0, 0, rs:rs + n, cols], pruns)
                bias = bias_ref[2 * g]
            read_cur = lambda cols: gather(lambda rs, n: kv_ref[0, 0, rs:rs + n, cols], runs)
            kcols, vcols = slice(0, HEAD_DIM), slice(HEAD_DIM, KW)
            k_cat = jnp.concatenate([read_prev(kcols), read_cur(kcols)], axis=0).astype(BF16)
            v_cat = jnp.concatenate([read_prev(vcols), read_cur(vcols)], axis=0).astype(BF16)

            s = _dot_nt(q4, k_cat) + bias
            m_new = jnp.broadcast_to(jnp.max(s, axis=-1, keepdims=True), (m_rows, HEAD_DIM))
            if g > 0:
                m_old = heads_state(m_ref, runs)
                m_new = jnp.maximum(m_old, m_new)
                alpha = jnp.exp2(m_old - m_new)
            p = jnp.exp2(jnp.concatenate([s[:, :BAND] - m_new, s[:, BAND:] - m_new], axis=1))
            ov = _dot(p.astype(BF16), jnp.concatenate([v_cat, ones], axis=1))
            acc_new, l_new = ov[:, :HEAD_DIM], ov[:, HEAD_DIM:]
            if g > 0:
                acc_new = alpha * heads_state(acc_ref, runs) + acc_new
                l_new = alpha * heads_state(l_ref, runs) + l_new
            if g < N_GROUPS - 1:
                scatter_state(acc_ref, acc_new, runs)
                scatter_state(m_ref, m_new, runs)
                scatter_state(l_ref, l_new, runs)
            else:
                o = (acc_new / l_new).astype(BF16)
                (rs, n), = runs
                for j in range(Q_PER_KV):
                    o_ref[0, rs:rs + n, j * HEAD_DIM:(j + 1) * HEAD_DIM] = o[j * BAND:(j + 1) * BAND]

    for g in range(N_GROUPS):
        pr = prev_rows[g]
        for r in range(NRES):
            prev_refs[g][r * pr:(r + 1) * pr, :] = kv_refs[g][0, 0, (r + 1) * BAND - pr:(r + 1) * BAND, :]


def _attn_prompt(qs, kvs, bias):
    bsz, _, seq, _ = qs[0].shape
    blk = lambda width: pl.BlockSpec((1, 1, TA, width), lambda b, h, i: (b, h, i, 0))
    return pl.pallas_call(
        _attn_prompt_kernel,
        grid=(bsz, KV_HEADS, seq // TA),
        in_specs=[blk(QW)] * N_GROUPS + [blk(KW)] * N_GROUPS + [_const_spec(bias.shape)],
        out_specs=pl.BlockSpec((1, TA, QW), lambda b, h, i: (b, i, h)),
        out_shape=jax.ShapeDtypeStruct((bsz, seq, KV_HEADS * QW), BF16),
        scratch_shapes=[
            pltpu.VMEM((Q_PER_KV, TA, HEAD_DIM), F32),
            pltpu.VMEM((Q_PER_KV, TA, HEAD_DIM), F32),
            pltpu.VMEM((Q_PER_KV, TA, HEAD_DIM), F32),
            pltpu.VMEM((N_GROUPS, Q_PER_KV * BAND, 2 * BAND), F32),
            pltpu.VMEM((NRES * 8, KW), GROUP_DTYPES[0]),
            pltpu.VMEM((NRES * 32, KW), GROUP_DTYPES[1]),
            pltpu.VMEM((NRES * BAND, KW), GROUP_DTYPES[2]),
        ],
        compiler_params=_params(("arbitrary", "arbitrary", "arbitrary")),
        name="attn_prompt",
    )(*qs, *kvs, bias)


def _out_prompt_kernel(o_ref, z_ref, x_ref, wout_ref, gpost_ref, y_ref, slab_ref):
    y = (_load_set(o_ref).astype(F32) * _silu(_load_set(z_ref))).astype(BF16)
    res = _load_set(x_ref) + _rms(_dot(y, wout_ref[...]), gpost_ref[...])
    y_ref[0] = _to_token_order(slab_ref, res)


def _out_q_prompt_kernel(o_ref, z_ref, x_ref, wout_ref, gpost_ref, gpre_ref, win_ref, cos_ref, sin_ref,
                         y_ref, q0_ref, q1_ref, q2_ref, zn_ref):
    y = (o_ref[0].astype(F32) * _silu(z_ref[0])).astype(BF16)
    x_new = x_ref[0] + _rms(_dot(y, wout_ref[...]), gpost_ref[...])
    y_ref[0] = x_new
    _q_project(x_new, gpre_ref, win_ref, cos_ref, sin_ref, (q0_ref, q1_ref, q2_ref), zn_ref)


def _out_q_proj_prompt(o, z, x, w_out, g_post, g_pre_next, w_in_next, cos_q, sin_q):
    bsz, seq, _ = x.shape
    x_spec, q_in_specs, q_out_specs, q_out_shape = _q_specs(bsz, seq)
    return pl.pallas_call(
        _out_q_prompt_kernel,
        grid=(bsz, seq // TM),
        in_specs=[x_spec] * 3 + [_const_spec((D_MODEL, D_MODEL)), _const_spec((1, D_MODEL))]
        + q_in_specs,
        out_specs=[x_spec] + q_out_specs,
        out_shape=[jax.ShapeDtypeStruct((bsz, seq, D_MODEL), F32)] + q_out_shape,
        compiler_params=_params(("arbitrary", "arbitrary")),
        name="out_q_proj_prompt",
    )(o, z, x, w_out, g_post, g_pre_next, w_in_next, cos_q, sin_q)


def _out_proj_prompt(o, z, x, w_out, g_post):
    bsz, seq, _ = x.shape
    view = lambda a: a.reshape(bsz, seq // TA, NRES, BAND, D_MODEL)
    return pl.pallas_call(
        _out_prompt_kernel,
        grid=(bsz, seq // TM),
        in_specs=[_set_spec(D_MODEL)] * 3
        + [_const_spec((D_MODEL, D_MODEL)), _const_spec((1, D_MODEL))],
        out_specs=pl.BlockSpec((1, TM, D_MODEL), lambda b, i: (b, i, 0)),
        out_shape=jax.ShapeDtypeStruct((bsz, seq, D_MODEL), F32),
        scratch_shapes=[pltpu.VMEM((D_MODEL // HEAD_DIM, 2, TM, HEAD_DIM), F32)],
        compiler_params=_params(("arbitrary", "arbitrary")),
        name="out_proj_prompt",
    )(view(o), view(z), view(x), w_out, g_post)


def _proj_sample_kernel(x_ref, g_ref, w_ref, cos_ref, sin_ref, y_ref, *, rope_heads):
    h = _rms(x_ref[...], g_ref[...]).astype(BF16)
    y = _dot(h, w_ref[...])
    cos = cos_ref[...]
    sin = sin_ref[...]
    for hd in range(rope_heads):
        cs = slice(hd * HEAD_DIM, (hd + 1) * HEAD_DIM)
        y_ref[:, cs] = _rope(y[:, cs], cos, sin)
    rest = rope_heads * HEAD_DIM
    y_ref[:, rest:] = y[:, rest:]


def _proj_sample(x, g, w, cos, sin, rope_heads):
    n = x.shape[0]
    return pl.pallas_call(
        functools.partial(_proj_sample_kernel, rope_heads=rope_heads),
        out_shape=jax.ShapeDtypeStruct((n, w.shape[1]), F32),
        compiler_params=_params(),
        name=f"proj_sample_{rope_heads}",
    )(x, g, w, cos, sin)


def _attn_sample_body(qz, kvn, window_kv, y_ref):
    pieces = []
    for kh in range(KV_HEADS):
        scores, s_new, vals, v_new = [], [], [], []
        for g in range(N_GROUPS):
            c = (g * HEADS + kh * Q_PER_KV) * HEAD_DIM
            q4 = jnp.concatenate(
                [qz[:, c + j * HEAD_DIM:c + (j + 1) * HEAD_DIM] for j in range(Q_PER_KV)], axis=0)
            q8 = jnp.concatenate([q4, q4], axis=0)
            k_rows, v_rows = window_kv(g, kh)
            kc = (g * KV_HEADS + kh) * HEAD_DIM
            k_n = kvn[:, kc:kc + HEAD_DIM]
            v_n = kvn[:, KV_HALF + kc:KV_HALF + kc + HEAD_DIM]
            scores.append(_dot_nt(q8.astype(BF16), k_rows))
            s_new.append(jnp.sum(q8 * k_n, axis=-1, keepdims=True))
            vals.append(v_rows)
            v_new.append(v_n)
        m = functools.reduce(jnp.maximum,
                             [jnp.max(s, axis=-1, keepdims=True) for s in scores] + s_new)
        ps = [jnp.exp(s - m) for s in scores]
        pn = [jnp.exp(s - m) for s in s_new]
        l = sum(jnp.sum(p, axis=-1, keepdims=True) for p in ps) + sum(pn)
        o = sum(_dot(ps[g].astype(BF16), vals[g]) + pn[g] * v_new[g] for g in range(N_GROUPS)) / l
        pieces += [o[j:j + 1, :] for j in range(Q_PER_KV)]
    o_row = jnp.concatenate(pieces, axis=-1)
    y_ref[0] = o_row * _silu(qz[:, Q_WIDTH:])


def _attn_sample_first_kernel(qz_ref, kvn_ref, c0_ref, c1_ref, c2_ref, y_ref, win_ref):
    c_refs = (c0_ref, c1_ref, c2_ref)
    nr = 2 * KV_HEADS

    def window_kv(g, kh):
        stride = nr * DILATIONS[g]
        k_rows = c_refs[g][0, pl.ds(kh, BAND, stride=stride), :].astype(BF16)
        v_rows = c_refs[g][0, pl.ds(KV_HEADS + kh, BAND, stride=stride), :].astype(BF16)
        slot = 2 * (g * KV_HEADS + kh)
        win_ref[0, slot] = k_rows
        win_ref[0, slot + 1] = v_rows
        return k_rows, v_rows

    _attn_sample_body(qz_ref[0], kvn_ref[0], window_kv, y_ref)


def _attn_sample_next_kernel(qz_ref, kvn_ref, win_ref, y_ref):
    def window_kv(g, kh):
        slot = 2 * (g * KV_HEADS + kh)
        return win_ref[0, slot], win_ref[0, slot + 1]

    _attn_sample_body(qz_ref[0], kvn_ref[0], window_kv, y_ref)


def _attn_sample_first(qz, kv_new, caches):
    n = qz.shape[0]
    row_spec = lambda a: pl.BlockSpec((1, 1, a.shape[-1]), lambda b: (b, 0, 0))
    cache_specs = [pl.BlockSpec((1, c.shape[1], HEAD_DIM), lambda b: (b, 0, 0)) for c in caches]
    win_shape = (n, 2 * N_GROUPS * KV_HEADS, BAND, HEAD_DIM)
    return pl.pallas_call(
        _attn_sample_first_kernel,
        grid=(n,),
        in_specs=[row_spec(qz), row_spec(kv_new)] + cache_specs,
        out_specs=[pl.BlockSpec((1, 1, D_MODEL), lambda b: (b, 0, 0)),
                   pl.BlockSpec((1,) + win_shape[1:], lambda b: (b, 0, 0, 0))],
        out_shape=[jax.ShapeDtypeStruct((n, 1, D_MODEL), F32),
                   jax.ShapeDtypeStruct(win_shape, BF16)],
        compiler_params=_params(("arbitrary",)),
        name="attn_sample_first",
    )(qz, kv_new, *caches)


def _attn_sample_next(qz, kv_new, win):
    n = qz.shape[0]
    row_spec = lambda a: pl.BlockSpec((1, 1, a.shape[-1]), lambda b: (b, 0, 0))
    return pl.pallas_call(
        _attn_sample_next_kernel,
        grid=(n,),
        in_specs=[row_spec(qz), row_spec(kv_new),
                  pl.BlockSpec((1,) + win.shape[1:], lambda b: (b, 0, 0, 0))],
        out_specs=pl.BlockSpec((1, 1, D_MODEL), lambda b: (b, 0, 0)),
        out_shape=jax.ShapeDtypeStruct((n, 1, D_MODEL), F32),
        compiler_params=_params(("arbitrary",)),
        name="attn_sample_next",
    )(qz, kv_new, win)


def _out_sample_kernel(y_ref, x_ref, wout_ref, gpost_ref, o_ref):
    out = _dot(y_ref[...].astype(BF16), wout_ref[...])
    o_ref[...] = x_ref[...] + _rms(out, gpost_ref[...])


def _out_sample(y, x, w_out, g_post):
    return pl.pallas_call(
        _out_sample_kernel,
        out_shape=jax.ShapeDtypeStruct(x.shape, F32),
        compiler_params=_params(),
        name="out_sample",
    )(y, x, w_out, g_post)


def _rope_tables(pos):
    half = HEAD_DIM // 2
    inv_freq = ROPE_THETA ** (-jnp.arange(half, dtype=F32) * 2.0 / HEAD_DIM)
    ang = pos.astype(F32)[:, None] * inv_freq[None, :]
    cos = jnp.cos(ang)
    sin = jnp.sin(ang)
    return jnp.concatenate([cos, cos], axis=-1), jnp.concatenate([-sin, sin], axis=-1)


def kernel(x_prompt, x_sample, state_conv, cache_kv_g0, cache_kv_g1, cache_kv_g2,
           a_norm_pre, a_w_in, a_conv, a_w_out, a_norm_post,
           kv_norm, w_kv, b_norm_pre, b_w_in, b_w_out, b_norm_post):
    bsz, seq, _ = x_prompt.shape
    n_s = x_sample.shape[0]
    n_a = a_w_in.shape[0]
    n_b = b_w_in.shape[0]
    assert x_sample.shape[1] == 1 and seq % TA == 0 and n_a >= 1 and n_b >= 1

    pos_rm = jnp.arange(seq).reshape(seq // TA, BAND, NRES).transpose(0, 2, 1).reshape(seq)
    cos_p, sin_p = _rope_tables(pos_rm)
    cos_s, sin_s = _rope_tables(PAST_LEN + jnp.arange(1))
    row = lambda v: v.reshape(1, -1)

    a_w_in16 = a_w_in.astype(BF16)
    a_w_out16 = a_w_out.astype(BF16)
    w_kv16 = w_kv.astype(BF16)
    b_w_in16 = b_w_in.astype(BF16)
    b_w_out16 = b_w_out.astype(BF16)

    conv_args = [(row(a_norm_pre[layer]), a_w_in16[layer], a_conv[layer], a_w_out16[layer],
                  row(a_norm_post[layer])) for layer in range(n_a)]

    hs = x_sample.reshape(n_s, D_MODEL)
    conv_s = []
    for layer in range(n_a):
        past0 = state_conv[layer, :, 0]
        past1 = state_conv[layer, :, 1]
        hs, u_s = _conv_layer_sample(hs, past0, past1, *conv_args[layer])
        conv_s.append(jnp.stack([past1, u_s], axis=1))
    kv_s = _proj_sample(hs, row(kv_norm), w_kv16, cos_s, sin_s, N_GROUPS * KV_HEADS)
    caches = (cache_kv_g0, cache_kv_g1, cache_kv_g2)
    gw = KV_HEADS * HEAD_DIM
    new_rows = [jnp.stack([kv_s[:, g * gw:(g + 1) * gw], kv_s[:, KV_HALF + g * gw:KV_HALF + (g + 1) * gw]],
                          axis=1).reshape(n_s, 1, 2, KV_HEADS, HEAD_DIM) for g in range(N_GROUPS)]
    cache_rows = [c.reshape(n_s, c.shape[1] * 2 * KV_HEADS, HEAD_DIM) for c in caches]
    kv_s3 = kv_s.reshape(n_s, 1, -1)
    for j in range(n_b):
        qz = _proj_sample(hs, row(b_norm_pre[j]), b_w_in16[j], cos_s * SCALE, sin_s * SCALE,
                          N_GROUPS * HEADS)
        qz3 = qz.reshape(n_s, 1, -1)
        if j == 0:
            y_s, win = _attn_sample_first(qz3, kv_s3, cache_rows)
        else:
            y_s = _attn_sample_next(qz3, kv_s3, win)
        hs = _out_sample(y_s.reshape(n_s, D_MODEL), hs, b_w_out16[j], row(b_norm_post[j]))

    hp = x_prompt
    conv_p = []
    for layer in range(n_a):
        hp, st_p, *shifted = _conv_layer_prompt(
            hp, *conv_args[layer], residue_major_out=(layer == n_a - 1),
            shift=(caches, new_rows) if layer == 0 else None)
        conv_p.append(st_p)
        if layer == 0:
            kv_new_s = shifted

    tab5 = lambda t: t.reshape(seq // TA, NRES, BAND, HEAD_DIM)
    kv_p, cs = _kv_proj_prompt(hp, row(kv_norm), w_kv16, tab5(cos_p), tab5(sin_p))
    kv_new_p = [c.reshape(bsz, c.shape[1], 2, KV_HEADS, HEAD_DIM) for c in cs]
    hp = hp.reshape(bsz, seq, D_MODEL)
    bias = _band_bias()
    cos_q = cos_p * (SCALE * LOG2E)
    sin_q = sin_p * (SCALE * LOG2E)
    *q, z = _q_proj_prompt(hp, row(b_norm_pre[0]), b_w_in16[0], cos_q, sin_q)
    for j in range(n_b):
        g_post = row(b_norm_post[j])
        o = _attn_prompt(q, kv_p, bias)
        if j + 1 < n_b:
            hp, *q, z = _out_q_proj_prompt(o, z, hp, b_w_out16[j], g_post, row(b_norm_pre[j + 1]),
                                           b_w_in16[j + 1], cos_q, sin_q)
        else:
            hp = _out_proj_prompt(o, z, hp, b_w_out16[j], g_post)

    return (hp, hs.reshape(n_s, 1, D_MODEL), jnp.stack(conv_p, axis=0),
            kv_new_p[0], kv_new_p[1], kv_new_p[2],
            jnp.stack(conv_s, axis=0), kv_new_s[0], kv_new_s[1], kv_new_s[2])
```

```python
import functools

import jax
import jax.numpy as jnp
from jax import lax
from jax.experimental import pallas as pl
from jax.experimental.pallas import tpu as pltpu

D_MODEL = 1024
HEAD_DIM = 128
N_GROUPS = 3
WINDOWS = (128, 512, 2048)
DILATIONS = (1, 4, 16)
HEADS = 8
KV_HEADS = 2
Q_PER_KV = HEADS // KV_HEADS
Q_WIDTH = N_GROUPS * HEADS * HEAD_DIM
KV_HALF = N_GROUPS * KV_HEADS * HEAD_DIM
PAST_LEN = 8192
ROPE_THETA = 10000.0
EPS = 1e-6
SCALE = HEAD_DIM ** -0.5
NEG = -1e30
LOG2E = 1.4426950408889634

BAND = 128
NRES = 16
TA = NRES * BAND
TM = 512
SUB = TM // NRES
CH = 256
QW = Q_PER_KV * HEAD_DIM
KW = 2 * HEAD_DIM
VMEM_LIMIT = 54 * 1024 * 1024

F32 = jnp.float32
BF16 = jnp.bfloat16
GROUP_DTYPES = (F32, BF16, BF16)


def _dot(a, b):
    return jnp.dot(a, b, preferred_element_type=F32)


def _dot_nt(a, b):
    return lax.dot_general(a, b, (((1,), (1,)), ((), ())), preferred_element_type=F32)


def _rms(x, g):
    ms = jnp.mean(x * x, axis=-1, keepdims=True)
    return x * lax.rsqrt(ms + EPS) * g


def _silu(z):
    return z * jax.nn.sigmoid(z)


def _rope(x, cos, sin_signed):
    return x * cos + pltpu.roll(x, HEAD_DIM // 2, 1) * sin_signed


def _params(sem=None, flags=None):
    return pltpu.CompilerParams(dimension_semantics=sem, vmem_limit_bytes=VMEM_LIMIT, flags=flags)


def _const_spec(shape):
    zeros = (0,) * len(shape)
    return pl.BlockSpec(shape, lambda *_: zeros, pipeline_mode=pl.Buffered(1))


def _split_residues(slab_ref, val):
    q = TM // 4
    slab_ref[0] = val
    for m in range(4):
        slab_ref[1, m * q:(m + 1) * q, :] = slab_ref[0, pl.ds(m, q, stride=4), :]
    return {4 * m2 + m: slab_ref[1, pl.ds(m * q + m2, SUB, stride=4), :]
            for m in range(4) for m2 in range(4)}


def _merge_residues(slab_ref, pieces):
    q = TM // 4
    for m in range(4):
        for m2 in range(4):
            slab_ref[1, pl.ds(m * q + m2, SUB, stride=4), :] = pieces[4 * m2 + m]
    for m in range(4):
        slab_ref[0, pl.ds(m, q, stride=4), :] = slab_ref[1, m * q:(m + 1) * q, :]
    return slab_ref[0]


def _to_residue_major(slab_ref, val):
    ncol = val.shape[1] // HEAD_DIM
    cols = [_split_residues(slab_ref.at[c], val[:, c * HEAD_DIM:(c + 1) * HEAD_DIM])
            for c in range(ncol)]
    return [jnp.concatenate([cols[c][r] for c in range(ncol)], axis=-1) for r in range(NRES)]


def _to_token_order(slab_ref, val):
    ncol = val.shape[1] // HEAD_DIM
    cols = [_merge_residues(slab_ref.at[c],
                            {r: val[r * SUB:(r + 1) * SUB, c * HEAD_DIM:(c + 1) * HEAD_DIM]
                             for r in range(NRES)})
            for c in range(ncol)]
    return jnp.concatenate(cols, axis=-1)


def _set_spec(width):
    per = TA // TM
    return pl.BlockSpec((1, 1, NRES, SUB, width), lambda b, i: (b, i // per, 0, i % per, 0))


def _load_set(ref):
    return jnp.concatenate([ref[0, 0, r] for r in range(NRES)], axis=0)


def _conv_project_chunk(h, win_ref, j):
    return [_dot(h, win_ref[:, k * D_MODEL + j * CH:k * D_MODEL + (j + 1) * CH]) for k in range(4)]


def _conv_mix_chunk(proj, wconv_ref, j, u_prev1, u_prev2):
    cs = slice(j * CH, (j + 1) * CH)
    b_gate, c_gate, u_in, z = proj
    u = c_gate * u_in
    conv = (u_prev2(u, cs) * wconv_ref[0:1, cs] + u_prev1(u, cs) * wconv_ref[1:2, cs]
            + u * wconv_ref[2:3, cs])
    return u, b_gate * conv * _silu(z)


def _conv_chunks(h, win_ref, wconv_ref, wout_ref, u_prev1, u_prev2, keep_u):
    nch = D_MODEL // CH
    acc = jnp.zeros((h.shape[0], D_MODEL), F32)
    proj = _conv_project_chunk(h, win_ref, 0)
    for j in range(nch):
        nxt = _conv_project_chunk(h, win_ref, j + 1) if j + 1 < nch else None
        u, y = _conv_mix_chunk(proj, wconv_ref, j, u_prev1, u_prev2)
        keep_u(u, slice(j * CH, (j + 1) * CH))
        acc = acc + _dot(y.astype(BF16), wout_ref[j * CH:(j + 1) * CH, :])
        proj = nxt
    return acc


def _shift_sample_caches(cache_refs, new_ref, out_refs, win_ref, rows):
    nr = 2 * KV_HEADS
    for p in range(rows):
        for g in range(N_GROUPS):
            c_ref, o_ref = cache_refs[g], out_refs[g]
            keep = (WINDOWS[g] - 1) * nr
            o_ref[p, 0:keep, :] = c_ref[p, nr:nr + keep, :]
            o_ref[p, keep:keep + nr, :] = new_ref[p, g]
            stride = nr * DILATIONS[g]
            for kh in range(KV_HEADS):
                slot = 2 * (g * KV_HEADS + kh)
                win_ref[p, slot] = c_ref[p, pl.ds(kh, BAND, stride=stride), :].astype(BF16)
                win_ref[p, slot + 1] = c_ref[p, pl.ds(KV_HEADS + kh, BAND, stride=stride), :].astype(BF16)


def _conv_prompt_kernel(*refs, residue_major_out, shift_rows):
    x_ref, gpre_ref, win_ref, wconv_ref, wout_ref, gpost_ref = refs[:6]
    refs = refs[6:]
    if shift_rows:
        cache_refs, new_ref, refs = refs[:3], refs[3], refs[4:]
    (y_ref, st_ref), refs = refs[:2], refs[2:]
    if shift_rows:
        out_cache_refs, window_ref, refs = refs[:3], refs[3], refs[4:]
    tail_ref, refs = refs[0], refs[1:]
    if residue_major_out:
        slab_ref, = refs

    @pl.when(pl.program_id(1) == 0)
    def _():
        tail_ref[...] = jnp.zeros_like(tail_ref)

    x = x_ref[0]
    h = _rms(x, gpre_ref[...]).astype(BF16)
    row = lax.broadcasted_iota(jnp.int32, (TM, CH), 0)

    def prev1(u, cs):
        return jnp.where(row == 0, tail_ref[1:2, cs], pltpu.roll(u, 1, 0))

    def prev2(u, cs):
        return jnp.where(row == 0, tail_ref[0:1, cs],
                         jnp.where(row == 1, tail_ref[1:2, cs], pltpu.roll(u, 2, 0)))

    def keep_tail(u, cs):
        tail_ref[0:2, cs] = u[TM - 2:TM, :]

    acc = _conv_chunks(h, win_ref, wconv_ref, wout_ref, prev1, prev2, keep_tail)
    out = x + _rms(acc, gpost_ref[...])
    if residue_major_out:
        for r, piece in enumerate(_to_residue_major(slab_ref, out)):
            y_ref[0, 0, r] = piece
    else:
        y_ref[0] = out
    st_ref[0] = tail_ref[0:2, :]

    if shift_rows:
        _shift_sample_caches(cache_refs, new_ref, out_cache_refs, window_ref, shift_rows)


def _conv_layer_prompt(x, g_pre, w_in, w_conv, w_out, g_post, residue_major_out, shift=None):
    bsz, seq, _ = x.shape
    nt = seq // TM
    n_steps = bsz * nt
    scratch = [pltpu.VMEM((8, D_MODEL), F32)]
    if residue_major_out:
        y_spec = _set_spec(D_MODEL)
        y_shape = jax.ShapeDtypeStruct((bsz, seq // TA, NRES, BAND, D_MODEL), F32)
        scratch.append(pltpu.VMEM((D_MODEL // HEAD_DIM, 2, TM, HEAD_DIM), F32))
    else:
        y_spec = pl.BlockSpec((1, TM, D_MODEL), lambda b, i: (b, i, 0))
        y_shape = jax.ShapeDtypeStruct((bsz, seq, D_MODEL), F32)
    extra_in, extra_in_specs, extra_out, extra_out_specs, shift_rows = [], [], [], [], 0
    if shift is not None:
        caches, new_rows = shift
        n_s = caches[0].shape[0]
        shift_rows = -(-n_s // n_steps)
        assert n_s % shift_rows == 0
        last = n_s // shift_rows - 1
        blk = lambda a: pl.BlockSpec(
            (shift_rows,) + a.shape[1:],
            lambda b, i: (jnp.minimum(b * nt + i, last),) + (0,) * (a.ndim - 1))
        win_sds = jax.ShapeDtypeStruct((n_s, 2 * N_GROUPS * KV_HEADS, BAND, HEAD_DIM), BF16)
        extra_in = list(caches) + [new_rows]
        extra_in_specs = [blk(a) for a in extra_in]
        extra_out = [jax.ShapeDtypeStruct(c.shape, c.dtype) for c in caches] + [win_sds]
        extra_out_specs = [blk(a) for a in extra_out]
    return pl.pallas_call(
        functools.partial(_conv_prompt_kernel, residue_major_out=residue_major_out,
                          shift_rows=shift_rows),
        grid=(bsz, seq // TM),
        in_specs=[
            pl.BlockSpec((1, TM, D_MODEL), lambda b, i: (b, i, 0)),
            _const_spec((1, D_MODEL)),
            _const_spec((D_MODEL, 4 * D_MODEL)),
            _const_spec((3, D_MODEL)),
            _const_spec((D_MODEL, D_MODEL)),
            _const_spec((1, D_MODEL)),
        ] + extra_in_specs,
        out_specs=[y_spec, pl.BlockSpec((1, 2, D_MODEL), lambda b, i: (b, 0, 0))] + extra_out_specs,
        out_shape=[y_shape, jax.ShapeDtypeStruct((bsz, 2, D_MODEL), F32)] + extra_out,
        scratch_shapes=scratch,
        compiler_params=_params(("arbitrary", "arbitrary")),
        name="conv_layer_prompt",
    )(x, g_pre, w_in, w_conv, w_out, g_post, *extra_in)


def _conv_sample_kernel(x_ref, p0_ref, p1_ref, gpre_ref, win_ref, wconv_ref, wout_ref, gpost_ref,
                        y_ref, u_ref):
    x = x_ref[...]
    h = _rms(x, gpre_ref[...]).astype(BF16)
    def keep_u(u, cs):
        u_ref[:, cs] = u

    acc = _conv_chunks(h, win_ref, wconv_ref, wout_ref,
                       lambda u, cs: p1_ref[:, cs], lambda u, cs: p0_ref[:, cs], keep_u)
    y_ref[...] = x + _rms(acc, gpost_ref[...])


def _conv_layer_sample(x, past0, past1, g_pre, w_in, w_conv, w_out, g_post):
    n = x.shape[0]
    return pl.pallas_call(
        _conv_sample_kernel,
        out_shape=[jax.ShapeDtypeStruct((n, D_MODEL), F32)] * 2,
        compiler_params=_params(),
        name="conv_layer_sample",
    )(x, past0, past1, g_pre, w_in, w_conv, w_out, g_post)


def _kv_prompt_kernel(x_ref, g_ref, w_ref, cos_ref, sin_ref, kv0_ref, kv1_ref, kv2_ref,
                      c0_ref, c1_ref, c2_ref, slab_ref, *, first_cache_tile):
    h = _rms(_load_set(x_ref), g_ref[...]).astype(BF16)
    cos = jnp.concatenate([cos_ref[0, r] for r in range(NRES)], axis=0)
    sin = jnp.concatenate([sin_ref[0, r] for r in range(NRES)], axis=0)
    kv = _dot(h, w_ref[...])
    kv_refs = (kv0_ref, kv1_ref, kv2_ref)
    fulls = []
    for g in range(N_GROUPS):
        ks, vs = [], []
        for hh in range(KV_HEADS):
            c = (g * KV_HEADS + hh) * HEAD_DIM
            k = _rope(kv[:, c:c + HEAD_DIM], cos, sin)
            v = kv[:, KV_HALF + c:KV_HALF + c + HEAD_DIM]
            both = jnp.concatenate([k, v], axis=-1).astype(GROUP_DTYPES[g])
            for r in range(NRES):
                kv_refs[g][0, hh, 0, r] = both[r * SUB:(r + 1) * SUB]
            ks.append(k)
            vs.append(v)
        fulls.append(jnp.concatenate(ks + vs, axis=-1))

    @pl.when(pl.program_id(1) >= first_cache_tile)
    def _():
        for g, c_ref in enumerate((c0_ref, c1_ref, c2_ref)):
            rows = min(TM, WINDOWS[g])
            c_ref[0] = _to_token_order(slab_ref, fulls[g])[TM - rows:, :]


def _kv_proj_prompt(x5, g, w_kv, cos5, sin5):
    bsz, nta = x5.shape[:2]
    seq = nta * TA
    nt = seq // TM
    per = TA // TM
    cache_specs, cache_shapes, firsts = [], [], []
    for grp in range(N_GROUPS):
        w = min(WINDOWS[grp], seq)
        rows = min(TM, w)
        first = nt - w // rows
        firsts.append(first)
        cache_specs.append(pl.BlockSpec(
            (1, rows, 4 * HEAD_DIM), lambda b, i, first=first: (b, jnp.maximum(i - first, 0), 0)))
        cache_shapes.append(jax.ShapeDtypeStruct((bsz, w, 4 * HEAD_DIM), F32))
    kv_spec = pl.BlockSpec((1, KV_HEADS, 1, NRES, SUB, KW),
                           lambda b, i: (b, 0, i // per, 0, i % per, 0))
    kv_shapes = [jax.ShapeDtypeStruct((bsz, KV_HEADS, nta, NRES, BAND, KW), dt)
                 for dt in GROUP_DTYPES]
    tab_spec = pl.BlockSpec((1, NRES, SUB, HEAD_DIM), lambda b, i: (i // per, 0, i % per, 0))
    outs = pl.pallas_call(
        functools.partial(_kv_prompt_kernel, first_cache_tile=min(firsts)),
        grid=(bsz, nt),
        in_specs=[_set_spec(D_MODEL), _const_spec((1, D_MODEL)),
                  _const_spec((D_MODEL, 2 * KV_HALF)), tab_spec, tab_spec],
        out_specs=[kv_spec] * N_GROUPS + cache_specs,
        out_shape=kv_shapes + cache_shapes,
        scratch_shapes=[pltpu.VMEM((4, 2, TM, HEAD_DIM), F32)],
        compiler_params=_params(("arbitrary", "arbitrary")),
        name="kv_proj_prompt",
    )(x5, g, w_kv, cos5, sin5)
    kvs = [o.reshape(bsz, KV_HEADS, seq, KW) for o in outs[:N_GROUPS]]
    return kvs, outs[N_GROUPS:]


def _q_project(x, g_ref, w_ref, cos_ref, sin_ref, q_refs, z_ref):
    h = _rms(x, g_ref[...]).astype(BF16)
    cos = cos_ref[...]
    sin = sin_ref[...]
    units = [(g, kh) for g in range(N_GROUPS) for kh in range(KV_HEADS)]
    project = lambda u: _dot(h, w_ref[:, (u[0] * KV_HEADS + u[1]) * QW:(u[0] * KV_HEADS + u[1] + 1) * QW])
    cur = project(units[0])
    for n, (g, kh) in enumerate(units):
        nxt = project(units[n + 1]) if n + 1 < len(units) else _dot(h, w_ref[:, Q_WIDTH:])
        heads = [_rope(cur[:, j * HEAD_DIM:(j + 1) * HEAD_DIM], cos, sin) for j in range(Q_PER_KV)]
        q_refs[g][0, kh] = jnp.concatenate(heads, axis=-1).astype(GROUP_DTYPES[g])
        cur = nxt
    z_ref[0] = cur


def _q_prompt_kernel(x_ref, g_ref, w_ref, cos_ref, sin_ref, q0_ref, q1_ref, q2_ref, z_ref):
    _q_project(x_ref[0], g_ref, w_ref, cos_ref, sin_ref, (q0_ref, q1_ref, q2_ref), z_ref)


def _q_specs(bsz, seq):
    x_spec = pl.BlockSpec((1, TM, D_MODEL), lambda b, i: (b, i, 0))
    tab_spec = pl.BlockSpec((TM, HEAD_DIM), lambda b, i: (i, 0))
    in_specs = [_const_spec((1, D_MODEL)), _const_spec((D_MODEL, Q_WIDTH + D_MODEL)),
                tab_spec, tab_spec]
    out_specs = [pl.BlockSpec((1, KV_HEADS, TM, QW), lambda b, i: (b, 0, i, 0))] * N_GROUPS + [x_spec]
    out_shape = ([jax.ShapeDtypeStruct((bsz, KV_HEADS, seq, QW), dt) for dt in GROUP_DTYPES]
                 + [jax.ShapeDtypeStruct((bsz, seq, D_MODEL), F32)])
    return x_spec, in_specs, out_specs, out_shape


def _q_proj_prompt(x, g, w_in, cos_q, sin_q):
    bsz, seq, _ = x.shape
    x_spec, in_specs, out_specs, out_shape = _q_specs(bsz, seq)
    return pl.pallas_call(
        _q_prompt_kernel,
        grid=(bsz, seq // TM),
        in_specs=[x_spec] + in_specs,
        out_specs=out_specs,
        out_shape=out_shape,
        compiler_params=_params(("arbitrary", "arbitrary")),
        name="q_proj_prompt",
    )(x, g, w_in, cos_q, sin_q)


def _block_runs(g, blk):
    if g == 2:
        return [(blk * BAND, BAND)]
    if g == 1:
        r4, c = blk
        return [((4 * mm + r4) * BAND + 32 * c, 32) for mm in range(4)]
    return [(r * BAND + 8 * blk, 8) for r in range(NRES)]


def _block_pos(g):
    a = jnp.arange(BAND)
    if g == 2:
        return a
    if g == 1:
        return 4 * (a % 32) + a // 32
    return NRES * (a % 8) + a // 8


def _band_bias():
    out = []
    for g in range(N_GROUPS):
        pos = _block_pos(g)
        qpos = jnp.tile(pos, Q_PER_KV)[:, None]
        ok_prev = pos[None, :] >= qpos
        ok_cur = pos[None, :] <= qpos
        neg = jnp.full(ok_cur.shape, NEG, F32)
        cur = jnp.where(ok_cur, 0.0, neg)
        out.append(jnp.concatenate([jnp.where(ok_prev, 0.0, neg), cur], axis=1))
        out.append(jnp.concatenate([neg, cur], axis=1))
    return jnp.stack(out)


def _attn_prompt_kernel(q0_ref, q1_ref, q2_ref, kv0_ref, kv1_ref, kv2_ref, bias_ref, o_ref,
                        acc_ref, m_ref, l_ref, biasf_ref, p0_ref, p1_ref, p2_ref):
    q_refs = (q0_ref, q1_ref, q2_ref)
    kv_refs = (kv0_ref, kv1_ref, kv2_ref)
    prev_refs = (p0_ref, p1_ref, p2_ref)
    prev_rows = (8, 32, BAND)
    first = pl.program_id(2) == 0

    @pl.when(first)
    def _():
        for p_ref in prev_refs:
            p_ref[...] = jnp.zeros_like(p_ref)

    for g in range(N_GROUPS):
        biasf_ref[g] = jnp.where(first, bias_ref[2 * g + 1], bias_ref[2 * g])
    ones = jnp.ones((2 * BAND, HEAD_DIM), BF16)
    m_rows = Q_PER_KV * BAND

    def gather(read, runs):
        return jnp.concatenate([read(rs, n) for rs, n in runs], axis=0)

    def heads_state(ref, runs):
        return jnp.concatenate(
            [gather(lambda rs, n, j=j: ref[j, rs:rs + n, :], runs) for j in range(Q_PER_KV)], axis=0)

    def scatter_state(ref, val, runs):
        for j in range(Q_PER_KV):
            off = j * BAND
            for rs, n in runs:
                ref[j, rs:rs + n, :] = val[off:off + n]
                off += n

    for g in range(N_GROUPS):
        q_ref, kv_ref, p_ref = q_refs[g], kv_refs[g], prev_refs[g]
        if g == 2:
            blocks = [(r, None) for r in range(NRES)]
        elif g == 1:
            blocks = [((r4, c), (r4, c - 1) if c else None) for r4 in range(4) for c in range(4)]
        else:
            blocks = [(c, c - 1 if c else None) for c in range(NRES)]
        for blk, prev_blk in blocks:
            runs = _block_runs(g, blk)
            q4 = jnp.concatenate(
                [gather(lambda rs, n, j=j: q_ref[0, 0, rs:rs + n, j * HEAD_DIM:(j + 1) * HEAD_DIM],
                        runs) for j in range(Q_PER_KV)], axis=0).astype(BF16)
            if prev_blk is None:
                pr = prev_rows[g]
                if g == 2:
                    pruns = [(blk * BAND, BAND)]
                elif g == 1:
                    pruns = [((4 * mm + blk[0]) * pr, pr) for mm in range(4)]
                else:
                    pruns = [(r * pr, pr) for r in range(NRES)]
                read_prev = lambda cols: gather(lambda rs, n: p_ref[rs:rs + n, cols], pruns)
                bias = biasf_ref[g]
            else:
                pruns = _block_runs(g, prev_blk)
                read_prev = lambda cols: gather(lambda rs, n: kv_ref[0, 0, rs:rs + n, cols], pruns)
                bias = bias_ref[2 * g]
            read_cur = lambda cols: gather(lambda rs, n: kv_ref[0, 0, rs:rs + n, cols], runs)
            kcols, vcols = slice(0, HEAD_DIM), slice(HEAD_DIM, KW)
            k_cat = jnp.concatenate([read_prev(kcols), read_cur(kcols)], axis=0).astype(BF16)
            v_cat = jnp.concatenate([read_prev(vcols), read_cur(vcols)], axis=0).astype(BF16)

            s = _dot_nt(q4, k_cat) + bias
            m_new = jnp.broadcast_to(jnp.max(s, axis=-1, keepdims=True), (m_rows, HEAD_DIM))
            if g > 0:
                m_old = heads_state(m_ref, runs)
                m_new = jnp.maximum(m_old, m_new)
                alpha = jnp.exp2(m_old - m_new)
            p = jnp.exp2(jnp.concatenate([s[:, :BAND] - m_new, s[:, BAND:] - m_new], axis=1))
            ov = _dot(p.astype(BF16), jnp.concatenate([v_cat, ones], axis=1))
            acc_new, l_new = ov[:, :HEAD_DIM], ov[:, HEAD_DIM:]
            if g > 0:
                acc_new = alpha * heads_state(acc_ref, runs) + acc_new
                l_new = alpha * heads_state(l_ref, runs) + l_new
            if g < N_GROUPS - 1:
                scatter_state(acc_ref, acc_new, runs)
                scatter_state(m_ref, m_new, runs)
                scatter_state(l_ref, l_new, runs)
            else:
                o = (acc_new / l_new).astype(BF16)
                (rs, n), = runs
                for j in range(Q_PER_KV):
                    o_ref[0, rs:rs + n, j * HEAD_DIM:(j + 1) * HEAD_DIM] = o[j * BAND:(j + 1) * BAND]

    for g in range(N_GROUPS):
        pr = prev_rows[g]
        for r in range(NRES):
            prev_refs[g][r * pr:(r + 1) * pr, :] = kv_refs[g][0, 0, (r + 1) * BAND - pr:(r + 1) * BAND, :]


def _attn_prompt(qs, kvs, bias):
    bsz, _, seq, _ = qs[0].shape
    blk = lambda width: pl.BlockSpec((1, 1, TA, width), lambda b, h, i: (b, h, i, 0))
    return pl.pallas_call(
        _attn_prompt_kernel,
        grid=(bsz, KV_HEADS, seq // TA),
        in_specs=[blk(QW)] * N_GROUPS + [blk(KW)] * N_GROUPS + [_const_spec(bias.shape)],
        out_specs=pl.BlockSpec((1, TA, QW), lambda b, h, i: (b, i, h)),
        out_shape=jax.ShapeDtypeStruct((bsz, seq, KV_HEADS * QW), BF16),
        scratch_shapes=[
            pltpu.VMEM((Q_PER_KV, TA, HEAD_DIM), F32),
            pltpu.VMEM((Q_PER_KV, TA, HEAD_DIM), F32),
            pltpu.VMEM((Q_PER_KV, TA, HEAD_DIM), F32),
            pltpu.VMEM((N_GROUPS, Q_PER_KV * BAND, 2 * BAND), F32),
            pltpu.VMEM((NRES * 8, KW), GROUP_DTYPES[0]),
            pltpu.VMEM((NRES * 32, KW), GROUP_DTYPES[1]),
            pltpu.VMEM((NRES * BAND, KW), GROUP_DTYPES[2]),
        ],
        compiler_params=_params(("arbitrary", "arbitrary", "arbitrary")),
        name="attn_prompt",
    )(*qs, *kvs, bias)


def _out_prompt_kernel(o_ref, z_ref, x_ref, wout_ref, gpost_ref, y_ref, slab_ref):
    y = (_load_set(o_ref).astype(F32) * _silu(_load_set(z_ref))).astype(BF16)
    res = _load_set(x_ref) + _rms(_dot(y, wout_ref[...]), gpost_ref[...])
    y_ref[0] = _to_token_order(slab_ref, res)


def _out_q_prompt_kernel(o_ref, z_ref, x_ref, wout_ref, gpost_ref, gpre_ref, win_ref, cos_ref, sin_ref,
                         y_ref, q0_ref, q1_ref, q2_ref, zn_ref):
    y = (o_ref[0].astype(F32) * _silu(z_ref[0])).astype(BF16)
    x_new = x_ref[0] + _rms(_dot(y, wout_ref[...]), gpost_ref[...])
    y_ref[0] = x_new
    _q_project(x_new, gpre_ref, win_ref, cos_ref, sin_ref, (q0_ref, q1_ref, q2_ref), zn_ref)


def _out_q_proj_prompt(o, z, x, w_out, g_post, g_pre_next, w_in_next, cos_q, sin_q):
    bsz, seq, _ = x.shape
    x_spec, q_in_specs, q_out_specs, q_out_shape = _q_specs(bsz, seq)
    return pl.pallas_call(
        _out_q_prompt_kernel,
        grid=(bsz, seq // TM),
        in_specs=[x_spec] * 3 + [_const_spec((D_MODEL, D_MODEL)), _const_spec((1, D_MODEL))]
        + q_in_specs,
        out_specs=[x_spec] + q_out_specs,
        out_shape=[jax.ShapeDtypeStruct((bsz, seq, D_MODEL), F32)] + q_out_shape,
        compiler_params=_params(("arbitrary", "arbitrary")),
        name="out_q_proj_prompt",
    )(o, z, x, w_out, g_post, g_pre_next, w_in_next, cos_q, sin_q)


def _out_proj_prompt(o, z, x, w_out, g_post):
    bsz, seq, _ = x.shape
    view = lambda a: a.reshape(bsz, seq // TA, NRES, BAND, D_MODEL)
    return pl.pallas_call(
        _out_prompt_kernel,
        grid=(bsz, seq // TM),
        in_specs=[_set_spec(D_MODEL)] * 3
        + [_const_spec((D_MODEL, D_MODEL)), _const_spec((1, D_MODEL))],
        out_specs=pl.BlockSpec((1, TM, D_MODEL), lambda b, i: (b, i, 0)),
        out_shape=jax.ShapeDtypeStruct((bsz, seq, D_MODEL), F32),
        scratch_shapes=[pltpu.VMEM((D_MODEL // HEAD_DIM, 2, TM, HEAD_DIM), F32)],
        compiler_params=_params(("arbitrary", "arbitrary")),
        name="out_proj_prompt",
    )(view(o), view(z), view(x), w_out, g_post)


def _proj_sample_kernel(x_ref, g_ref, w_ref, cos_ref, sin_ref, y_ref, *, rope_heads):
    h = _rms(x_ref[...], g_ref[...]).astype(BF16)
    y = _dot(h, w_ref[...])
    cos = cos_ref[...]
    sin = sin_ref[...]
    for hd in range(rope_heads):
        cs = slice(hd * HEAD_DIM, (hd + 1) * HEAD_DIM)
        y_ref[:, cs] = _rope(y[:, cs], cos, sin)
    rest = rope_heads * HEAD_DIM
    y_ref[:, rest:] = y[:, rest:]


def _proj_sample(x, g, w, cos, sin, rope_heads):
    n = x.shape[0]
    return pl.pallas_call(
        functools.partial(_proj_sample_kernel, rope_heads=rope_heads),
        out_shape=jax.ShapeDtypeStruct((n, w.shape[1]), F32),
        compiler_params=_params(),
        name=f"proj_sample_{rope_heads}",
    )(x, g, w, cos, sin)


def _attn_sample_body(qz, kvn, window_kv, y_ref):
    pieces = []
    for kh in range(KV_HEADS):
        scores, s_new, vals, v_new = [], [], [], []
        for g in range(N_GROUPS):
            c = (g * HEADS + kh * Q_PER_KV) * HEAD_DIM
            q4 = jnp.concatenate(
                [qz[:, c + j * HEAD_DIM:c + (j + 1) * HEAD_DIM] for j in range(Q_PER_KV)], axis=0)
            q8 = jnp.concatenate([q4, q4], axis=0)
            k_rows, v_rows = window_kv(g, kh)
            kc = (g * KV_HEADS + kh) * HEAD_DIM
            k_n = kvn[:, kc:kc + HEAD_DIM]
            v_n = kvn[:, KV_HALF + kc:KV_HALF + kc + HEAD_DIM]
            scores.append(_dot_nt(q8.astype(BF16), k_rows))
            s_new.append(jnp.sum(q8 * k_n, axis=-1, keepdims=True))
            vals.append(v_rows)
            v_new.append(v_n)
        m = functools.reduce(jnp.maximum,
                             [jnp.max(s, axis=-1, keepdims=True) for s in scores] + s_new)
        ps = [jnp.exp(s - m) for s in scores]
        pn = [jnp.exp(s - m) for s in s_new]
        l = sum(jnp.sum(p, axis=-1, keepdims=True) for p in ps) + sum(pn)
        o = sum(_dot(ps[g].astype(BF16), vals[g]) + pn[g] * v_new[g] for g in range(N_GROUPS)) / l
        pieces += [o[j:j + 1, :] for j in range(Q_PER_KV)]
    o_row = jnp.concatenate(pieces, axis=-1)
    y_ref[0] = o_row * _silu(qz[:, Q_WIDTH:])


def _attn_sample_kernel(qz_ref, kvn_ref, win_ref, y_ref):
    def window_kv(g, kh):
        slot = 2 * (g * KV_HEADS + kh)
        return win_ref[0, slot], win_ref[0, slot + 1]

    _attn_sample_body(qz_ref[0], kvn_ref[0], window_kv, y_ref)


def _attn_sample(qz, kv_new, win):
    n = qz.shape[0]
    row_spec = lambda a: pl.BlockSpec((1, 1, a.shape[-1]), lambda b: (b, 0, 0))
    return pl.pallas_call(
        _attn_sample_kernel,
        grid=(n,),
        in_specs=[row_spec(qz), row_spec(kv_new),
                  pl.BlockSpec((1,) + win.shape[1:], lambda b: (b, 0, 0, 0))],
        out_specs=pl.BlockSpec((1, 1, D_MODEL), lambda b: (b, 0, 0)),
        out_shape=jax.ShapeDtypeStruct((n, 1, D_MODEL), F32),
        compiler_params=_params(("arbitrary",)),
        name="attn_sample",
    )(qz, kv_new, win)


def _out_sample_kernel(y_ref, x_ref, wout_ref, gpost_ref, o_ref):
    out = _dot(y_ref[...].astype(BF16), wout_ref[...])
    o_ref[...] = x_ref[...] + _rms(out, gpost_ref[...])


def _out_sample(y, x, w_out, g_post):
    return pl.pallas_call(
        _out_sample_kernel,
        out_shape=jax.ShapeDtypeStruct(x.shape, F32),
        compiler_params=_params(),
        name="out_sample",
    )(y, x, w_out, g_post)


def _rope_tables(pos):
    half = HEAD_DIM // 2
    inv_freq = ROPE_THETA ** (-jnp.arange(half, dtype=F32) * 2.0 / HEAD_DIM)
    ang = pos.astype(F32)[:, None] * inv_freq[None, :]
    cos = jnp.cos(ang)
    sin = jnp.sin(ang)
    return jnp.concatenate([cos, cos], axis=-1), jnp.concatenate([-sin, sin], axis=-1)


def kernel(x_prompt, x_sample, state_conv, cache_kv_g0, cache_kv_g1, cache_kv_g2,
           a_norm_pre, a_w_in, a_conv, a_w_out, a_norm_post,
           kv_norm, w_kv, b_norm_pre, b_w_in, b_w_out, b_norm_post):
    bsz, seq, _ = x_prompt.shape
    n_s = x_sample.shape[0]
    n_a = a_w_in.shape[0]
    n_b = b_w_in.shape[0]
    assert x_sample.shape[1] == 1 and seq % TA == 0 and n_a >= 1 and n_b >= 1

    pos_rm = jnp.arange(seq).reshape(seq // TA, BAND, NRES).transpose(0, 2, 1).reshape(seq)
    cos_p, sin_p = _rope_tables(pos_rm)
    cos_s, sin_s = _rope_tables(PAST_LEN + jnp.arange(1))
    row = lambda v: v.reshape(1, -1)

    a_w_in16 = a_w_in.astype(BF16)
    a_w_out16 = a_w_out.astype(BF16)
    w_kv16 = w_kv.astype(BF16)
    b_w_in16 = b_w_in.astype(BF16)
    b_w_out16 = b_w_out.astype(BF16)

    conv_args = [(row(a_norm_pre[layer]), a_w_in16[layer], a_conv[layer], a_w_out16[layer],
                  row(a_norm_post[layer])) for layer in range(n_a)]

    hs = x_sample.reshape(n_s, D_MODEL)
    conv_s = []
    for layer in range(n_a):
        past0 = state_conv[layer, :, 0]
        past1 = state_conv[layer, :, 1]
        hs, u_s = _conv_layer_sample(hs, past0, past1, *conv_args[layer])
        conv_s.append(jnp.stack([past1, u_s], axis=1))
    kv_s = _proj_sample(hs, row(kv_norm), w_kv16, cos_s, sin_s, N_GROUPS * KV_HEADS)
    nr = 2 * KV_HEADS
    cache_rows = [c.reshape(n_s, c.shape[1] * nr, HEAD_DIM)
                  for c in (cache_kv_g0, cache_kv_g1, cache_kv_g2)]
    new_rows = jnp.stack([kv_s[:, :KV_HALF].reshape(n_s, N_GROUPS, KV_HEADS, HEAD_DIM),
                          kv_s[:, KV_HALF:].reshape(n_s, N_GROUPS, KV_HEADS, HEAD_DIM)],
                         axis=2).reshape(n_s, N_GROUPS, nr, HEAD_DIM)

    hp = x_prompt
    conv_p = []
    for layer in range(n_a):
        hp, st_p, *extra = _conv_layer_prompt(
            hp, *conv_args[layer], residue_major_out=(layer == n_a - 1),
            shift=(cache_rows, new_rows) if layer == 0 else None)
        conv_p.append(st_p)
        if layer == 0:
            *shifted, win = extra
            kv_new_s = [c.reshape(n_s, WINDOWS[g], 2, KV_HEADS, HEAD_DIM)
                        for g, c in enumerate(shifted)]

    kv_s3 = kv_s.reshape(n_s, 1, -1)
    for j in range(n_b):
        qz = _proj_sample(hs, row(b_norm_pre[j]), b_w_in16[j], cos_s * SCALE, sin_s * SCALE,
                          N_GROUPS * HEADS)
        y_s = _attn_sample(qz.reshape(n_s, 1, -1), kv_s3, win)
        hs = _out_sample(y_s.reshape(n_s, D_MODEL), hs, b_w_out16[j], row(b_norm_post[j]))

    tab5 = lambda t: t.reshape(seq // TA, NRES, BAND, HEAD_DIM)
    kv_p, cs = _kv_proj_prompt(hp, row(kv_norm), w_kv16, tab5(cos_p), tab5(sin_p))
    kv_new_p = [c.reshape(bsz, c.shape[1], 2, KV_HEADS, HEAD_DIM) for c in cs]
    hp = hp.reshape(bsz, seq, D_MODEL)
    bias = _band_bias()
    cos_q = cos_p * (SCALE * LOG2E)
    sin_q = sin_p * (SCALE * LOG2E)
    *q, z = _q_proj_prompt(hp, row(b_norm_pre[0]), b_w_in16[0], cos_q, sin_q)
    for j in range(n_b):
        g_post = row(b_norm_post[j])
        o = _attn_prompt(q, kv_p, bias)
        if j + 1 < n_b:
            hp, *q, z = _out_q_proj_prompt(o, z, hp, b_w_out16[j], g_post, row(b_norm_pre[j + 1]),
                                           b_w_in16[j + 1], cos_q, sin_q)
        else:
            hp = _out_proj_prompt(o, z, hp, b_w_out16[j], g_post)

    return (hp, hs.reshape(n_s, 1, D_MODEL), jnp.stack(conv_p, axis=0),
            kv_new_p[0], kv_new_p[1], kv_new_p[2],
            jnp.stack(conv_s, axis=0), kv_new_s[0], kv_new_s[1], kv_new_s[2])
```

```python
import functools

import jax
import jax.numpy as jnp
from jax import lax
from jax.experimental import pallas as pl
from jax.experimental.pallas import tpu as pltpu

D_MODEL = 1024
HEAD_DIM = 128
N_GROUPS = 3
WINDOWS = (128, 512, 2048)
DILATIONS = (1, 4, 16)
HEADS = 8
KV_HEADS = 2
Q_PER_KV = HEADS // KV_HEADS
Q_WIDTH = N_GROUPS * HEADS * HEAD_DIM
KV_HALF = N_GROUPS * KV_HEADS * HEAD_DIM
PAST_LEN = 8192
ROPE_THETA = 10000.0
EPS = 1e-6
SCALE = HEAD_DIM ** -0.5
NEG = -1e30
LOG2E = 1.4426950408889634

BAND = 128
NRES = 16
TA = NRES * BAND
TM = 512
SUB = TM // NRES
CH = 256
SAMPLE_ROWS = 8
QW = Q_PER_KV * HEAD_DIM
KW = 2 * HEAD_DIM
VMEM_LIMIT = 54 * 1024 * 1024

F32 = jnp.float32
BF16 = jnp.bfloat16
GROUP_DTYPES = (F32, BF16, BF16)


def _dot(a, b):
    return jnp.dot(a, b, preferred_element_type=F32)


def _dot_nt(a, b):
    return lax.dot_general(a, b, (((1,), (1,)), ((), ())), preferred_element_type=F32)


def _rms(x, g):
    ms = jnp.mean(x * x, axis=-1, keepdims=True)
    return x * lax.rsqrt(ms + EPS) * g


def _silu(z):
    return z * jax.nn.sigmoid(z)


def _rope(x, cos, sin_signed):
    return x * cos + pltpu.roll(x, HEAD_DIM // 2, 1) * sin_signed


def _params(sem=None, flags=None):
    return pltpu.CompilerParams(dimension_semantics=sem, vmem_limit_bytes=VMEM_LIMIT, flags=flags)


def _const_spec(shape):
    zeros = (0,) * len(shape)
    return pl.BlockSpec(shape, lambda *_: zeros, pipeline_mode=pl.Buffered(1))


def _split_residues(slab_ref, val):
    q = TM // 4
    slab_ref[0] = val
    for m in range(4):
        slab_ref[1, m * q:(m + 1) * q, :] = slab_ref[0, pl.ds(m, q, stride=4), :]
    return {4 * m2 + m: slab_ref[1, pl.ds(m * q + m2, SUB, stride=4), :]
            for m in range(4) for m2 in range(4)}


def _merge_residues(slab_ref, pieces):
    q = TM // 4
    for m in range(4):
        for m2 in range(4):
            slab_ref[1, pl.ds(m * q + m2, SUB, stride=4), :] = pieces[4 * m2 + m]
    for m in range(4):
        slab_ref[0, pl.ds(m, q, stride=4), :] = slab_ref[1, m * q:(m + 1) * q, :]
    return slab_ref[0]


def _to_residue_major(slab_ref, val):
    ncol = val.shape[1] // HEAD_DIM
    cols = [_split_residues(slab_ref.at[c], val[:, c * HEAD_DIM:(c + 1) * HEAD_DIM])
            for c in range(ncol)]
    return [jnp.concatenate([cols[c][r] for c in range(ncol)], axis=-1) for r in range(NRES)]


def _to_token_order(slab_ref, val):
    ncol = val.shape[1] // HEAD_DIM
    cols = [_merge_residues(slab_ref.at[c],
                            {r: val[r * SUB:(r + 1) * SUB, c * HEAD_DIM:(c + 1) * HEAD_DIM]
                             for r in range(NRES)})
            for c in range(ncol)]
    return jnp.concatenate(cols, axis=-1)


def _set_spec(width):
    per = TA // TM
    return pl.BlockSpec((1, 1, NRES, SUB, width), lambda b, i: (b, i // per, 0, i % per, 0))


def _load_set(ref):
    return jnp.concatenate([ref[0, 0, r] for r in range(NRES)], axis=0)


def _conv_project_chunk(h, win_ref, j):
    return [_dot(h, win_ref[:, k * D_MODEL + j * CH:k * D_MODEL + (j + 1) * CH]) for k in range(4)]


def _conv_mix_chunk(proj, wconv_ref, j, u_prev1, u_prev2):
    cs = slice(j * CH, (j + 1) * CH)
    b_gate, c_gate, u_in, z = proj
    u = c_gate * u_in
    conv = (u_prev2(u, cs) * wconv_ref[0:1, cs] + u_prev1(u, cs) * wconv_ref[1:2, cs]
            + u * wconv_ref[2:3, cs])
    return u, b_gate * conv * _silu(z)


def _conv_chunks(h, win_ref, wconv_ref, wout_ref, u_prev1, u_prev2, keep_u):
    nch = D_MODEL // CH
    acc = jnp.zeros((h.shape[0], D_MODEL), F32)
    proj = _conv_project_chunk(h, win_ref, 0)
    for j in range(nch):
        nxt = _conv_project_chunk(h, win_ref, j + 1) if j + 1 < nch else None
        u, y = _conv_mix_chunk(proj, wconv_ref, j, u_prev1, u_prev2)
        keep_u(u, slice(j * CH, (j + 1) * CH))
        acc = acc + _dot(y.astype(BF16), wout_ref[j * CH:(j + 1) * CH, :])
        proj = nxt
    return acc


def _shift_sample_caches(cache_refs, new_ref, out_refs, win_ref, rows):
    nr = 2 * KV_HEADS
    for p in range(rows):
        for g in range(N_GROUPS):
            c_ref, o_ref = cache_refs[g], out_refs[g]
            keep = (WINDOWS[g] - 1) * nr
            o_ref[p, 0:keep, :] = c_ref[p, nr:nr + keep, :]
            o_ref[p, keep:keep + nr, :] = new_ref[p, g]
            stride = nr * DILATIONS[g]
            for kh in range(KV_HEADS):
                slot = 2 * (g * KV_HEADS + kh)
                win_ref[p, slot] = c_ref[p, pl.ds(kh, BAND, stride=stride), :].astype(BF16)
                win_ref[p, slot + 1] = c_ref[p, pl.ds(KV_HEADS + kh, BAND, stride=stride), :].astype(BF16)


def _conv_prompt_kernel(*refs, residue_major_out, shift_rows):
    x_ref, gpre_ref, win_ref, wconv_ref, wout_ref, gpost_ref = refs[:6]
    refs = refs[6:]
    if shift_rows:
        cache_refs, new_ref, refs = refs[:3], refs[3], refs[4:]
    (y_ref, st_ref), refs = refs[:2], refs[2:]
    if shift_rows:
        out_cache_refs, window_ref, refs = refs[:3], refs[3], refs[4:]
    tail_ref, refs = refs[0], refs[1:]
    if residue_major_out:
        slab_ref, = refs

    @pl.when(pl.program_id(1) == 0)
    def _():
        tail_ref[...] = jnp.zeros_like(tail_ref)

    x = x_ref[0]
    h = _rms(x, gpre_ref[...]).astype(BF16)
    row = lax.broadcasted_iota(jnp.int32, (TM, CH), 0)

    def prev1(u, cs):
        return jnp.where(row == 0, tail_ref[1:2, cs], pltpu.roll(u, 1, 0))

    def prev2(u, cs):
        return jnp.where(row == 0, tail_ref[0:1, cs],
                         jnp.where(row == 1, tail_ref[1:2, cs], pltpu.roll(u, 2, 0)))

    def keep_tail(u, cs):
        tail_ref[0:2, cs] = u[TM - 2:TM, :]

    acc = _conv_chunks(h, win_ref, wconv_ref, wout_ref, prev1, prev2, keep_tail)
    out = x + _rms(acc, gpost_ref[...])
    if residue_major_out:
        for r, piece in enumerate(_to_residue_major(slab_ref, out)):
            y_ref[0, 0, r] = piece
    else:
        y_ref[0] = out
    st_ref[0] = tail_ref[0:2, :]

    if shift_rows:
        _shift_sample_caches(cache_refs, new_ref, out_cache_refs, window_ref, shift_rows)


def _conv_layer_prompt(x, g_pre, w_in, w_conv, w_out, g_post, residue_major_out, shift=None):
    bsz, seq, _ = x.shape
    nt = seq // TM
    n_steps = bsz * nt
    scratch = [pltpu.VMEM((8, D_MODEL), F32)]
    if residue_major_out:
        y_spec = _set_spec(D_MODEL)
        y_shape = jax.ShapeDtypeStruct((bsz, seq // TA, NRES, BAND, D_MODEL), F32)
        scratch.append(pltpu.VMEM((D_MODEL // HEAD_DIM, 2, TM, HEAD_DIM), F32))
    else:
        y_spec = pl.BlockSpec((1, TM, D_MODEL), lambda b, i: (b, i, 0))
        y_shape = jax.ShapeDtypeStruct((bsz, seq, D_MODEL), F32)
    extra_in, extra_in_specs, extra_out, extra_out_specs, shift_rows = [], [], [], [], 0
    if shift is not None:
        caches, new_rows = shift
        n_s = caches[0].shape[0]
        shift_rows = -(-n_s // n_steps)
        assert n_s % shift_rows == 0
        last = n_s // shift_rows - 1
        blk = lambda a: pl.BlockSpec(
            (shift_rows,) + a.shape[1:],
            lambda b, i: (jnp.minimum(b * nt + i, last),) + (0,) * (a.ndim - 1))
        win_sds = jax.ShapeDtypeStruct((n_s, 2 * N_GROUPS * KV_HEADS, BAND, HEAD_DIM), BF16)
        extra_in = list(caches) + [new_rows]
        extra_in_specs = [blk(a) for a in extra_in]
        extra_out = [jax.ShapeDtypeStruct(c.shape, c.dtype) for c in caches] + [win_sds]
        extra_out_specs = [blk(a) for a in extra_out]
    return pl.pallas_call(
        functools.partial(_conv_prompt_kernel, residue_major_out=residue_major_out,
                          shift_rows=shift_rows),
        grid=(bsz, seq // TM),
        in_specs=[
            pl.BlockSpec((1, TM, D_MODEL), lambda b, i: (b, i, 0)),
            _const_spec((1, D_MODEL)),
            _const_spec((D_MODEL, 4 * D_MODEL)),
            _const_spec((3, D_MODEL)),
            _const_spec((D_MODEL, D_MODEL)),
            _const_spec((1, D_MODEL)),
        ] + extra_in_specs,
        out_specs=[y_spec, pl.BlockSpec((1, 2, D_MODEL), lambda b, i: (b, 0, 0))] + extra_out_specs,
        out_shape=[y_shape, jax.ShapeDtypeStruct((bsz, 2, D_MODEL), F32)] + extra_out,
        scratch_shapes=scratch,
        compiler_params=_params(("arbitrary", "arbitrary")),
        name="conv_layer_prompt",
    )(x, g_pre, w_in, w_conv, w_out, g_post, *extra_in)


def _conv_sample_kernel(x_ref, p0_ref, p1_ref, gpre_ref, win_ref, wconv_ref, wout_ref, gpost_ref,
                        y_ref, u_ref):
    x = x_ref[...]
    h = _rms(x, gpre_ref[...]).astype(BF16)
    def keep_u(u, cs):
        u_ref[:, cs] = u

    acc = _conv_chunks(h, win_ref, wconv_ref, wout_ref,
                       lambda u, cs: p1_ref[:, cs], lambda u, cs: p0_ref[:, cs], keep_u)
    y_ref[...] = x + _rms(acc, gpost_ref[...])


def _conv_layer_sample(x, past0, past1, g_pre, w_in, w_conv, w_out, g_post):
    n = x.shape[0]
    return pl.pallas_call(
        _conv_sample_kernel,
        out_shape=[jax.ShapeDtypeStruct((n, D_MODEL), F32)] * 2,
        compiler_params=_params(),
        name="conv_layer_sample",
    )(x, past0, past1, g_pre, w_in, w_conv, w_out, g_post)


def _kv_prompt_kernel(x_ref, g_ref, w_ref, cos_ref, sin_ref, kv0_ref, kv1_ref, kv2_ref,
                      c0_ref, c1_ref, c2_ref, slab_ref, *, first_cache_tile):
    h = _rms(_load_set(x_ref), g_ref[...]).astype(BF16)
    cos = jnp.concatenate([cos_ref[0, r] for r in range(NRES)], axis=0)
    sin = jnp.concatenate([sin_ref[0, r] for r in range(NRES)], axis=0)
    kv = _dot(h, w_ref[...])
    kv_refs = (kv0_ref, kv1_ref, kv2_ref)
    fulls = []
    for g in range(N_GROUPS):
        ks, vs = [], []
        for hh in range(KV_HEADS):
            c = (g * KV_HEADS + hh) * HEAD_DIM
            k = _rope(kv[:, c:c + HEAD_DIM], cos, sin)
            v = kv[:, KV_HALF + c:KV_HALF + c + HEAD_DIM]
            both = jnp.concatenate([k, v], axis=-1).astype(GROUP_DTYPES[g])
            for r in range(NRES):
                kv_refs[g][0, hh, 0, r] = both[r * SUB:(r + 1) * SUB]
            ks.append(k)
            vs.append(v)
        fulls.append(jnp.concatenate(ks + vs, axis=-1))

    @pl.when(pl.program_id(1) >= first_cache_tile)
    def _():
        for g, c_ref in enumerate((c0_ref, c1_ref, c2_ref)):
            rows = min(TM, WINDOWS[g])
            c_ref[0] = _to_token_order(slab_ref, fulls[g])[TM - rows:, :]


def _kv_proj_prompt(x5, g, w_kv, cos5, sin5):
    bsz, nta = x5.shape[:2]
    seq = nta * TA
    nt = seq // TM
    per = TA // TM
    cache_specs, cache_shapes, firsts = [], [], []
    for grp in range(N_GROUPS):
        w = min(WINDOWS[grp], seq)
        rows = min(TM, w)
        first = nt - w // rows
        firsts.append(first)
        cache_specs.append(pl.BlockSpec(
            (1, rows, 4 * HEAD_DIM), lambda b, i, first=first: (b, jnp.maximum(i - first, 0), 0)))
        cache_shapes.append(jax.ShapeDtypeStruct((bsz, w, 4 * HEAD_DIM), F32))
    kv_spec = pl.BlockSpec((1, KV_HEADS, 1, NRES, SUB, KW),
                           lambda b, i: (b, 0, i // per, 0, i % per, 0))
    kv_shapes = [jax.ShapeDtypeStruct((bsz, KV_HEADS, nta, NRES, BAND, KW), dt)
                 for dt in GROUP_DTYPES]
    tab_spec = pl.BlockSpec((1, NRES, SUB, HEAD_DIM), lambda b, i: (i // per, 0, i % per, 0))
    outs = pl.pallas_call(
        functools.partial(_kv_prompt_kernel, first_cache_tile=min(firsts)),
        grid=(bsz, nt),
        in_specs=[_set_spec(D_MODEL), _const_spec((1, D_MODEL)),
                  _const_spec((D_MODEL, 2 * KV_HALF)), tab_spec, tab_spec],
        out_specs=[kv_spec] * N_GROUPS + cache_specs,
        out_shape=kv_shapes + cache_shapes,
        scratch_shapes=[pltpu.VMEM((4, 2, TM, HEAD_DIM), F32)],
        compiler_params=_params(("arbitrary", "arbitrary")),
        name="kv_proj_prompt",
    )(x5, g, w_kv, cos5, sin5)
    kvs = [o.reshape(bsz, KV_HEADS, seq, KW) for o in outs[:N_GROUPS]]
    return kvs, outs[N_GROUPS:]


def _q_project(x, g_ref, w_ref, cos_ref, sin_ref, q_refs, z_ref):
    h = _rms(x, g_ref[...]).astype(BF16)
    cos = cos_ref[...]
    sin = sin_ref[...]
    units = [(g, kh) for g in range(N_GROUPS) for kh in range(KV_HEADS)]
    project = lambda u: _dot(h, w_ref[:, (u[0] * KV_HEADS + u[1]) * QW:(u[0] * KV_HEADS + u[1] + 1) * QW])
    cur = project(units[0])
    for n, (g, kh) in enumerate(units):
        nxt = project(units[n + 1]) if n + 1 < len(units) else _dot(h, w_ref[:, Q_WIDTH:])
        heads = [_rope(cur[:, j * HEAD_DIM:(j + 1) * HEAD_DIM], cos, sin) for j in range(Q_PER_KV)]
        q_refs[g][0, kh] = jnp.concatenate(heads, axis=-1).astype(GROUP_DTYPES[g])
        cur = nxt
    z_ref[0] = cur.astype(BF16)


def _q_prompt_kernel(x_ref, g_ref, w_ref, cos_ref, sin_ref, q0_ref, q1_ref, q2_ref, z_ref):
    _q_project(x_ref[0], g_ref, w_ref, cos_ref, sin_ref, (q0_ref, q1_ref, q2_ref), z_ref)


def _q_specs(bsz, seq):
    x_spec = pl.BlockSpec((1, TM, D_MODEL), lambda b, i: (b, i, 0))
    tab_spec = pl.BlockSpec((TM, HEAD_DIM), lambda b, i: (i, 0))
    in_specs = [_const_spec((1, D_MODEL)), _const_spec((D_MODEL, Q_WIDTH + D_MODEL)),
                tab_spec, tab_spec]
    out_specs = [pl.BlockSpec((1, KV_HEADS, TM, QW), lambda b, i: (b, 0, i, 0))] * N_GROUPS + [x_spec]
    out_shape = ([jax.ShapeDtypeStruct((bsz, KV_HEADS, seq, QW), dt) for dt in GROUP_DTYPES]
                 + [jax.ShapeDtypeStruct((bsz, seq, D_MODEL), BF16)])
    return x_spec, in_specs, out_specs, out_shape


def _q_proj_prompt(x, g, w_in, cos_q, sin_q):
    bsz, seq, _ = x.shape
    x_spec, in_specs, out_specs, out_shape = _q_specs(bsz, seq)
    return pl.pallas_call(
        _q_prompt_kernel,
        grid=(bsz, seq // TM),
        in_specs=[x_spec] + in_specs,
        out_specs=out_specs,
        out_shape=out_shape,
        compiler_params=_params(("arbitrary", "arbitrary")),
        name="q_proj_prompt",
    )(x, g, w_in, cos_q, sin_q)


def _block_runs(g, blk):
    if g == 2:
        return [(blk * BAND, BAND)]
    if g == 1:
        r4, c = blk
        return [((4 * mm + r4) * BAND + 32 * c, 32) for mm in range(4)]
    return [(r * BAND + 8 * blk, 8) for r in range(NRES)]


def _block_pos(g):
    a = jnp.arange(BAND)
    if g == 2:
        return a
    if g == 1:
        return 4 * (a % 32) + a // 32
    return NRES * (a % 8) + a // 8


def _band_bias():
    out = []
    for g in range(N_GROUPS):
        pos = _block_pos(g)
        qpos = jnp.tile(pos, Q_PER_KV)[:, None]
        ok_prev = pos[None, :] >= qpos
        ok_cur = pos[None, :] <= qpos
        neg = jnp.full(ok_cur.shape, NEG, F32)
        cur = jnp.where(ok_cur, 0.0, neg)
        out.append(jnp.concatenate([jnp.where(ok_prev, 0.0, neg), cur], axis=1))
        out.append(jnp.concatenate([neg, cur], axis=1))
    return jnp.stack(out)


def _attn_prompt_kernel(q0_ref, q1_ref, q2_ref, kv0_ref, kv1_ref, kv2_ref, bias_ref, o_ref,
                        acc_ref, m_ref, l_ref, biasf_ref, p0_ref, p1_ref, p2_ref):
    q_refs = (q0_ref, q1_ref, q2_ref)
    kv_refs = (kv0_ref, kv1_ref, kv2_ref)
    prev_refs = (p0_ref, p1_ref, p2_ref)
    prev_rows = (8, 32, BAND)
    first = pl.program_id(2) == 0

    @pl.when(first)
    def _():
        for p_ref in prev_refs:
            p_ref[...] = jnp.zeros_like(p_ref)

    for g in range(N_GROUPS):
        biasf_ref[g] = jnp.where(first, bias_ref[2 * g + 1], bias_ref[2 * g])
    ones = jnp.ones((2 * BAND, HEAD_DIM), BF16)
    m_rows = Q_PER_KV * BAND

    def gather(read, runs):
        return jnp.concatenate([read(rs, n) for rs, n in runs], axis=0)

    def heads_state(ref, runs):
        return jnp.concatenate(
            [gather(lambda rs, n, j=j: ref[j, rs:rs + n, :], runs) for j in range(Q_PER_KV)], axis=0)

    def scatter_state(ref, val, runs):
        for j in range(Q_PER_KV):
            off = j * BAND
            for rs, n in runs:
                ref[j, rs:rs + n, :] = val[off:off + n]
                off += n

    for g in range(N_GROUPS):
        q_ref, kv_ref, p_ref = q_refs[g], kv_refs[g], prev_refs[g]
        if g == 2:
            blocks = [(r, None) for r in range(NRES)]
        elif g == 1:
            blocks = [((r4, c), (r4, c - 1) if c else None) for r4 in range(4) for c in range(4)]
        else:
            blocks = [(c, c - 1 if c else None) for c in range(NRES)]
        for blk, prev_blk in blocks:
            runs = _block_runs(g, blk)
            q4 = jnp.concatenate(
                [gather(lambda rs, n, j=j: q_ref[0, 0, rs:rs + n, j * HEAD_DIM:(j + 1) * HEAD_DIM],
                        runs) for j in range(Q_PER_KV)], axis=0).astype(BF16)
            if prev_blk is None:
                pr = prev_rows[g]
                if g == 2:
                    pruns = [(blk * BAND, BAND)]
                elif g == 1:
                    pruns = [((4 * mm + blk[0]) * pr, pr) for mm in range(4)]
                else:
                    pruns = [(r * pr, pr) for r in range(NRES)]
                read_prev = lambda cols: gather(lambda rs, n: p_ref[rs:rs + n, cols], pruns)
                bias = biasf_ref[g]
            else:
                pruns = _block_runs(g, prev_blk)
                read_prev = lambda cols: gather(lambda rs, n: kv_ref[0, 0, rs:rs + n, cols], pruns)
                bias = bias_ref[2 * g]
            read_cur = lambda cols: gather(lambda rs, n: kv_ref[0, 0, rs:rs + n, cols], runs)
            kcols, vcols = slice(0, HEAD_DIM), slice(HEAD_DIM, KW)
            k_cat = jnp.concatenate([read_prev(kcols), read_cur(kcols)], axis=0).astype(BF16)
            v_cat = jnp.concatenate([read_prev(vcols), read_cur(vcols)], axis=0).astype(BF16)

            s = _dot_nt(q4, k_cat) + bias
            m_new = jnp.broadcast_to(jnp.max(s, axis=-1, keepdims=True), (m_rows, HEAD_DIM))
            if g > 0:
                m_old = heads_state(m_ref, runs)
                m_new = jnp.maximum(m_old, m_new)
                alpha = jnp.exp2(m_old - m_new)
            p = jnp.exp2(jnp.concatenate([s[:, :BAND] - m_new, s[:, BAND:] - m_new], axis=1))
            ov = _dot(p.astype(BF16), jnp.concatenate([v_cat, ones], axis=1))
            acc_new, l_new = ov[:, :HEAD_DIM], ov[:, HEAD_DIM:]
            if g > 0:
                acc_new = alpha * heads_state(acc_ref, runs) + acc_new
                l_new = alpha * heads_state(l_ref, runs) + l_new
            if g < N_GROUPS - 1:
                scatter_state(acc_ref, acc_new, runs)
                scatter_state(m_ref, m_new, runs)
                scatter_state(l_ref, l_new, runs)
            else:
                o = (acc_new / l_new).astype(BF16)
                (rs, n), = runs
                for j in range(Q_PER_KV):
                    o_ref[0, rs:rs + n, j * HEAD_DIM:(j + 1) * HEAD_DIM] = o[j * BAND:(j + 1) * BAND]

    for g in range(N_GROUPS):
        pr = prev_rows[g]
        for r in range(NRES):
            prev_refs[g][r * pr:(r + 1) * pr, :] = kv_refs[g][0, 0, (r + 1) * BAND - pr:(r + 1) * BAND, :]


def _attn_prompt(qs, kvs, bias):
    bsz, _, seq, _ = qs[0].shape
    blk = lambda width: pl.BlockSpec((1, 1, TA, width), lambda b, h, i: (b, h, i, 0))
    return pl.pallas_call(
        _attn_prompt_kernel,
        grid=(bsz, KV_HEADS, seq // TA),
        in_specs=[blk(QW)] * N_GROUPS + [blk(KW)] * N_GROUPS + [_const_spec(bias.shape)],
        out_specs=pl.BlockSpec((1, TA, QW), lambda b, h, i: (b, i, h)),
        out_shape=jax.ShapeDtypeStruct((bsz, seq, KV_HEADS * QW), BF16),
        scratch_shapes=[
            pltpu.VMEM((Q_PER_KV, TA, HEAD_DIM), F32),
            pltpu.VMEM((Q_PER_KV, TA, HEAD_DIM), F32),
            pltpu.VMEM((Q_PER_KV, TA, HEAD_DIM), F32),
            pltpu.VMEM((N_GROUPS, Q_PER_KV * BAND, 2 * BAND), F32),
            pltpu.VMEM((NRES * 8, KW), GROUP_DTYPES[0]),
            pltpu.VMEM((NRES * 32, KW), GROUP_DTYPES[1]),
            pltpu.VMEM((NRES * BAND, KW), GROUP_DTYPES[2]),
        ],
        compiler_params=_params(("arbitrary", "arbitrary", "arbitrary")),
        name="attn_prompt",
    )(*qs, *kvs, bias)


def _out_prompt_kernel(o_ref, z_ref, x_ref, wout_ref, gpost_ref, y_ref, slab_ref):
    y = (_load_set(o_ref).astype(F32) * _silu(_load_set(z_ref).astype(F32))).astype(BF16)
    res = _load_set(x_ref) + _rms(_dot(y, wout_ref[...]), gpost_ref[...])
    y_ref[0] = _to_token_order(slab_ref, res)


def _out_q_prompt_kernel(o_ref, z_ref, x_ref, wout_ref, gpost_ref, gpre_ref, win_ref, cos_ref, sin_ref,
                         y_ref, q0_ref, q1_ref, q2_ref, zn_ref):
    y = (o_ref[0].astype(F32) * _silu(z_ref[0].astype(F32))).astype(BF16)
    x_new = x_ref[0] + _rms(_dot(y, wout_ref[...]), gpost_ref[...])
    y_ref[0] = x_new
    _q_project(x_new, gpre_ref, win_ref, cos_ref, sin_ref, (q0_ref, q1_ref, q2_ref), zn_ref)


def _out_q_proj_prompt(o, z, x, w_out, g_post, g_pre_next, w_in_next, cos_q, sin_q):
    bsz, seq, _ = x.shape
    x_spec, q_in_specs, q_out_specs, q_out_shape = _q_specs(bsz, seq)
    return pl.pallas_call(
        _out_q_prompt_kernel,
        grid=(bsz, seq // TM),
        in_specs=[x_spec] * 3 + [_const_spec((D_MODEL, D_MODEL)), _const_spec((1, D_MODEL))]
        + q_in_specs,
        out_specs=[x_spec] + q_out_specs,
        out_shape=[jax.ShapeDtypeStruct((bsz, seq, D_MODEL), F32)] + q_out_shape,
        compiler_params=_params(("arbitrary", "arbitrary")),
        name="out_q_proj_prompt",
    )(o, z, x, w_out, g_post, g_pre_next, w_in_next, cos_q, sin_q)


def _out_proj_prompt(o, z, x, w_out, g_post):
    bsz, seq, _ = x.shape
    view = lambda a: a.reshape(bsz, seq // TA, NRES, BAND, D_MODEL)
    return pl.pallas_call(
        _out_prompt_kernel,
        grid=(bsz, seq // TM),
        in_specs=[_set_spec(D_MODEL)] * 3
        + [_const_spec((D_MODEL, D_MODEL)), _const_spec((1, D_MODEL))],
        out_specs=pl.BlockSpec((1, TM, D_MODEL), lambda b, i: (b, i, 0)),
        out_shape=jax.ShapeDtypeStruct((bsz, seq, D_MODEL), F32),
        scratch_shapes=[pltpu.VMEM((D_MODEL // HEAD_DIM, 2, TM, HEAD_DIM), F32)],
        compiler_params=_params(("arbitrary", "arbitrary")),
        name="out_proj_prompt",
    )(view(o), view(z), view(x), w_out, g_post)


def _proj_sample_kernel(x_ref, g_ref, w_ref, cos_ref, sin_ref, y_ref, *, rope_heads):
    h = _rms(x_ref[...], g_ref[...]).astype(BF16)
    y = _dot(h, w_ref[...])
    cos = cos_ref[...]
    sin = sin_ref[...]
    for hd in range(rope_heads):
        cs = slice(hd * HEAD_DIM, (hd + 1) * HEAD_DIM)
        y_ref[:, cs] = _rope(y[:, cs], cos, sin)
    rest = rope_heads * HEAD_DIM
    y_ref[:, rest:] = y[:, rest:]


def _proj_sample(x, g, w, cos, sin, rope_heads):
    n = x.shape[0]
    return pl.pallas_call(
        functools.partial(_proj_sample_kernel, rope_heads=rope_heads),
        out_shape=jax.ShapeDtypeStruct((n, w.shape[1]), F32),
        compiler_params=_params(),
        name=f"proj_sample_{rope_heads}",
    )(x, g, w, cos, sin)


def _attn_sample_row(qz, kvn, window_kv):
    pieces = []
    for kh in range(KV_HEADS):
        scores, s_new, vals, v_new = [], [], [], []
        for g in range(N_GROUPS):
            c = (g * HEADS + kh * Q_PER_KV) * HEAD_DIM
            q4 = jnp.concatenate(
                [qz[:, c + j * HEAD_DIM:c + (j + 1) * HEAD_DIM] for j in range(Q_PER_KV)], axis=0)
            q8 = jnp.concatenate([q4, q4], axis=0)
            k_rows, v_rows = window_kv(g, kh)
            kc = (g * KV_HEADS + kh) * HEAD_DIM
            k_n = kvn[:, kc:kc + HEAD_DIM]
            v_n = kvn[:, KV_HALF + kc:KV_HALF + kc + HEAD_DIM]
            scores.append(_dot_nt(q8.astype(BF16), k_rows))
            s_new.append(jnp.sum(q8 * k_n, axis=-1, keepdims=True))
            vals.append(v_rows)
            v_new.append(v_n)
        m = functools.reduce(jnp.maximum,
                             [jnp.max(s, axis=-1, keepdims=True) for s in scores] + s_new)
        ps = [jnp.exp(s - m) for s in scores]
        pn = [jnp.exp(s - m) for s in s_new]
        l = sum(jnp.sum(p, axis=-1, keepdims=True) for p in ps) + sum(pn)
        o = sum(_dot(ps[g].astype(BF16), vals[g]) + pn[g] * v_new[g] for g in range(N_GROUPS)) / l
        pieces += [o[j:j + 1, :] for j in range(Q_PER_KV)]
    o_row = jnp.concatenate(pieces, axis=-1)
    return o_row * _silu(qz[:, Q_WIDTH:])


def _attn_sample_kernel(qz_ref, kvn_ref, win_ref, y_ref):
    for p in range(SAMPLE_ROWS):
        def window_kv(g, kh, p=p):
            slot = 2 * (g * KV_HEADS + kh)
            return win_ref[p, slot], win_ref[p, slot + 1]

        y_ref[p] = _attn_sample_row(qz_ref[p], kvn_ref[p], window_kv)


def _attn_sample(qz, kv_new, win):
    n = qz.shape[0]
    assert n % SAMPLE_ROWS == 0
    row_spec = lambda a: pl.BlockSpec((SAMPLE_ROWS, 1, a.shape[-1]), lambda b: (b, 0, 0))
    return pl.pallas_call(
        _attn_sample_kernel,
        grid=(n // SAMPLE_ROWS,),
        in_specs=[row_spec(qz), row_spec(kv_new),
                  pl.BlockSpec((SAMPLE_ROWS,) + win.shape[1:], lambda b: (b, 0, 0, 0))],
        out_specs=pl.BlockSpec((SAMPLE_ROWS, 1, D_MODEL), lambda b: (b, 0, 0)),
        out_shape=jax.ShapeDtypeStruct((n, 1, D_MODEL), F32),
        compiler_params=_params(("arbitrary",)),
        name="attn_sample",
    )(qz, kv_new, win)


def _out_sample_kernel(y_ref, x_ref, wout_ref, gpost_ref, o_ref):
    out = _dot(y_ref[...].astype(BF16), wout_ref[...])
    o_ref[...] = x_ref[...] + _rms(out, gpost_ref[...])


def _out_sample(y, x, w_out, g_post):
    return pl.pallas_call(
        _out_sample_kernel,
        out_shape=jax.ShapeDtypeStruct(x.shape, F32),
        compiler_params=_params(),
        name="out_sample",
    )(y, x, w_out, g_post)


def _inv_freq():
    return ROPE_THETA ** (-jnp.arange(HEAD_DIM // 2, dtype=F32) * 2.0 / HEAD_DIM)


def _full_width(cos, sin):
    return jnp.concatenate([cos, cos], axis=-1), jnp.concatenate([-sin, sin], axis=-1)


def _rope_tables(pos):
    ang = pos.astype(F32)[:, None] * _inv_freq()[None, :]
    return _full_width(jnp.cos(ang), jnp.sin(ang))


def _rope_tables_residue_major(seq):
    f = _inv_freq()
    base = (jnp.arange(seq // TA)[:, None] * TA + NRES * jnp.arange(BAND)[None, :]).astype(F32)
    a = base[:, None, :, None] * f
    b = jnp.arange(NRES, dtype=F32)[None, :, None, None] * f
    cos_a, sin_a, cos_b, sin_b = jnp.cos(a), jnp.sin(a), jnp.cos(b), jnp.sin(b)
    cos = (cos_a * cos_b - sin_a * sin_b).reshape(seq, HEAD_DIM // 2)
    sin = (sin_a * cos_b + cos_a * sin_b).reshape(seq, HEAD_DIM // 2)
    return _full_width(cos, sin)


def kernel(x_prompt, x_sample, state_conv, cache_kv_g0, cache_kv_g1, cache_kv_g2,
           a_norm_pre, a_w_in, a_conv, a_w_out, a_norm_post,
           kv_norm, w_kv, b_norm_pre, b_w_in, b_w_out, b_norm_post):
    bsz, seq, _ = x_prompt.shape
    n_s = x_sample.shape[0]
    n_a = a_w_in.shape[0]
    n_b = b_w_in.shape[0]
    assert x_sample.shape[1] == 1 and seq % TA == 0 and n_a >= 1 and n_b >= 1

    cos_p, sin_p = _rope_tables_residue_major(seq)
    cos_s, sin_s = _rope_tables(PAST_LEN + jnp.arange(1))
    row = lambda v: v.reshape(1, -1)

    a_w_in16 = [a_w_in[layer].astype(BF16) for layer in range(n_a)]
    a_w_out16 = [a_w_out[layer].astype(BF16) for layer in range(n_a)]
    w_kv16 = w_kv.astype(BF16)
    b_w_in16 = [b_w_in[j].astype(BF16) for j in range(n_b)]
    b_w_out16 = [b_w_out[j].astype(BF16) for j in range(n_b)]

    conv_args = [(row(a_norm_pre[layer]), a_w_in16[layer], a_conv[layer], a_w_out16[layer],
                  row(a_norm_post[layer])) for layer in range(n_a)]

    hs = x_sample.reshape(n_s, D_MODEL)
    conv_s = []
    for layer in range(n_a):
        past0 = state_conv[layer, :, 0]
        past1 = state_conv[layer, :, 1]
        hs, u_s = _conv_layer_sample(hs, past0, past1, *conv_args[layer])
        conv_s.append(jnp.stack([past1, u_s], axis=1))
    kv_s = _proj_sample(hs, row(kv_norm), w_kv16, cos_s, sin_s, N_GROUPS * KV_HEADS)
    nr = 2 * KV_HEADS
    cache_rows = [c.reshape(n_s, c.shape[1] * nr, HEAD_DIM)
                  for c in (cache_kv_g0, cache_kv_g1, cache_kv_g2)]
    new_rows = jnp.stack([kv_s[:, :KV_HALF].reshape(n_s, N_GROUPS, KV_HEADS, HEAD_DIM),
                          kv_s[:, KV_HALF:].reshape(n_s, N_GROUPS, KV_HEADS, HEAD_DIM)],
                         axis=2).reshape(n_s, N_GROUPS, nr, HEAD_DIM)

    hp = x_prompt
    conv_p = []
    for layer in range(n_a):
        hp, st_p, *extra = _conv_layer_prompt(
            hp, *conv_args[layer], residue_major_out=(layer == n_a - 1),
            shift=(cache_rows, new_rows) if layer == 0 else None)
        conv_p.append(st_p)
        if layer == 0:
            *shifted, win = extra
            kv_new_s = [c.reshape(n_s, WINDOWS[g], 2, KV_HEADS, HEAD_DIM)
                        for g, c in enumerate(shifted)]

    kv_s3 = kv_s.reshape(n_s, 1, -1)
    for j in range(n_b):
        qz = _proj_sample(hs, row(b_norm_pre[j]), b_w_in16[j], cos_s * SCALE, sin_s * SCALE,
                          N_GROUPS * HEADS)
        y_s = _attn_sample(qz.reshape(n_s, 1, -1), kv_s3, win)
        hs = _out_sample(y_s.reshape(n_s, D_MODEL), hs, b_w_out16[j], row(b_norm_post[j]))

    tab5 = lambda t: t.reshape(seq // TA, NRES, BAND, HEAD_DIM)
    kv_p, cs = _kv_proj_prompt(hp, row(kv_norm), w_kv16, tab5(cos_p), tab5(sin_p))
    kv_new_p = [c.reshape(bsz, c.shape[1], 2, KV_HEADS, HEAD_DIM) for c in cs]
    hp = hp.reshape(bsz, seq, D_MODEL)
    bias = _band_bias()
    cos_q = cos_p * (SCALE * LOG2E)
    sin_q = sin_p * (SCALE * LOG2E)
    *q, z = _q_proj_prompt(hp, row(b_norm_pre[0]), b_w_in16[0], cos_q, sin_q)
    for j in range(n_b):
        g_post = row(b_norm_post[j])
        o = _attn_prompt(q, kv_p, bias)
        if j + 1 < n_b:
            hp, *q, z = _out_q_proj_prompt(o, z, hp, b_w_out16[j], g_post, row(b_norm_pre[j + 1]),
                                           b_w_in16[j + 1], cos_q, sin_q)
        else:
            hp = _out_proj_prompt(o, z, hp, b_w_out16[j], g_post)

    return (hp, hs.reshape(n_s, 1, D_MODEL), jnp.stack(conv_p, axis=0),
            kv_new_p[0], kv_new_p[1], kv_new_p[2],
            jnp.stack(conv_s, axis=0), kv_new_s[0], kv_new_s[1], kv_new_s[2])
```

```python
import functools

import jax
import jax.numpy as jnp
from jax import lax
from jax.experimental import pallas as pl
from jax.experimental.pallas import tpu as pltpu

D_MODEL = 1024
HEAD_DIM = 128
N_GROUPS = 3
WINDOWS = (128, 512, 2048)
DILATIONS = (1, 4, 16)
HEADS = 8
KV_HEADS = 2
Q_PER_KV = HEADS // KV_HEADS
Q_WIDTH = N_GROUPS * HEADS * HEAD_DIM
KV_HALF = N_GROUPS * KV_HEADS * HEAD_DIM
PAST_LEN = 8192
ROPE_THETA = 10000.0
EPS = 1e-6
SCALE = HEAD_DIM ** -0.5
NEG = -1e30
LOG2E = 1.4426950408889634

BAND = 128
NRES = 16
TA = NRES * BAND
TM = 512
SUB = TM // NRES
CH = 256
SAMPLE_ROWS = 8
QW = Q_PER_KV * HEAD_DIM
KW = 2 * HEAD_DIM
VMEM_LIMIT = 54 * 1024 * 1024

F32 = jnp.float32
BF16 = jnp.bfloat16
GROUP_DTYPES = (F32, BF16, BF16)


def _dot(a, b):
    return jnp.dot(a, b, preferred_element_type=F32)


def _dot_nt(a, b):
    return lax.dot_general(a, b, (((1,), (1,)), ((), ())), preferred_element_type=F32)


def _rms(x, g):
    ms = jnp.mean(x * x, axis=-1, keepdims=True)
    return x * lax.rsqrt(ms + EPS) * g


def _silu(z):
    return z * jax.nn.sigmoid(z)


def _rope(x, cos, sin_signed):
    return x * cos + pltpu.roll(x, HEAD_DIM // 2, 1) * sin_signed


def _params(sem=None, flags=None):
    return pltpu.CompilerParams(dimension_semantics=sem, vmem_limit_bytes=VMEM_LIMIT, flags=flags)


def _const_spec(shape):
    zeros = (0,) * len(shape)
    return pl.BlockSpec(shape, lambda *_: zeros, pipeline_mode=pl.Buffered(1))


def _layer_spec(w, layer):
    return pl.BlockSpec((None,) + w.shape[1:], lambda *_: (layer, 0, 0),
                        pipeline_mode=pl.Buffered(1))


def _whole_spec(a):
    zeros = (0,) * a.ndim
    return pl.BlockSpec(a.shape, lambda *_: zeros)


def _split_residues(slab_ref, val):
    q = TM // 4
    slab_ref[0] = val
    for m in range(4):
        slab_ref[1, m * q:(m + 1) * q, :] = slab_ref[0, pl.ds(m, q, stride=4), :]
    return {4 * m2 + m: slab_ref[1, pl.ds(m * q + m2, SUB, stride=4), :]
            for m in range(4) for m2 in range(4)}


def _merge_residues(slab_ref, pieces):
    q = TM // 4
    for m in range(4):
        for m2 in range(4):
            slab_ref[1, pl.ds(m * q + m2, SUB, stride=4), :] = pieces[4 * m2 + m]
    for m in range(4):
        slab_ref[0, pl.ds(m, q, stride=4), :] = slab_ref[1, m * q:(m + 1) * q, :]
    return slab_ref[0]


def _to_residue_major(slab_ref, val):
    ncol = val.shape[1] // HEAD_DIM
    cols = [_split_residues(slab_ref.at[c], val[:, c * HEAD_DIM:(c + 1) * HEAD_DIM])
            for c in range(ncol)]
    return [jnp.concatenate([cols[c][r] for c in range(ncol)], axis=-1) for r in range(NRES)]


def _to_token_order(slab_ref, val):
    ncol = val.shape[1] // HEAD_DIM
    cols = [_merge_residues(slab_ref.at[c],
                            {r: val[r * SUB:(r + 1) * SUB, c * HEAD_DIM:(c + 1) * HEAD_DIM]
                             for r in range(NRES)})
            for c in range(ncol)]
    return jnp.concatenate(cols, axis=-1)


def _set_spec(width, sets=1):
    per = TA // (sets * TM)
    return pl.BlockSpec((1, 1, NRES, sets * SUB, width), lambda b, i: (b, i // per, 0, i % per, 0))


def _load_set(ref, t=0):
    return jnp.concatenate([ref[0, 0, r, t * SUB:(t + 1) * SUB, :] for r in range(NRES)], axis=0)


def _conv_project_chunk(h, win_ref, j):
    return [_dot(h, win_ref[:, k * D_MODEL + j * CH:k * D_MODEL + (j + 1) * CH]) for k in range(4)]


def _conv_mix_chunk(proj, wconv_ref, j, u_prev1, u_prev2):
    cs = slice(j * CH, (j + 1) * CH)
    b_gate, c_gate, u_in, z = proj
    u = c_gate * u_in
    conv = (u_prev2(u, cs) * wconv_ref[0:1, cs] + u_prev1(u, cs) * wconv_ref[1:2, cs]
            + u * wconv_ref[2:3, cs])
    return u, b_gate * conv * _silu(z)


def _conv_chunks(h, win_ref, wconv_ref, wout_ref, u_prev1, u_prev2, keep_u):
    nch = D_MODEL // CH
    acc = jnp.zeros((h.shape[0], D_MODEL), F32)
    proj = _conv_project_chunk(h, win_ref, 0)
    for j in range(nch):
        nxt = _conv_project_chunk(h, win_ref, j + 1) if j + 1 < nch else None
        u, y = _conv_mix_chunk(proj, wconv_ref, j, u_prev1, u_prev2)
        keep_u(u, slice(j * CH, (j + 1) * CH))
        acc = acc + _dot(y.astype(BF16), wout_ref[j * CH:(j + 1) * CH, :])
        proj = nxt
    return acc


def _shift_sample_caches(cache_refs, new_ref, out_refs, win_ref, rows):
    nr = 2 * KV_HEADS
    for p in range(rows):
        for g in range(N_GROUPS):
            c_ref, o_ref = cache_refs[g], out_refs[g]
            keep = (WINDOWS[g] - 1) * nr
            o_ref[p, 0:keep, :] = c_ref[p, nr:nr + keep, :]
            o_ref[p, keep:keep + nr, :] = new_ref[p, g]
            stride = nr * DILATIONS[g]
            for kh in range(KV_HEADS):
                slot = 2 * (g * KV_HEADS + kh)
                win_ref[p, slot] = c_ref[p, pl.ds(kh, BAND, stride=stride), :].astype(BF16)
                win_ref[p, slot + 1] = c_ref[p, pl.ds(KV_HEADS + kh, BAND, stride=stride), :].astype(BF16)


def _conv_prompt_kernel(*refs, residue_major_out, shift_rows, tiles):
    x_ref, gpre_ref, win_ref, wconv_ref, wout_ref, gpost_ref = refs[:6]
    refs = refs[6:]
    if shift_rows:
        cache_refs, new_ref, refs = refs[:3], refs[3], refs[4:]
    (y_ref, st_ref), refs = refs[:2], refs[2:]
    if shift_rows:
        out_cache_refs, window_ref, refs = refs[:3], refs[3], refs[4:]
    tail_ref, refs = refs[0], refs[1:]
    if residue_major_out:
        slab_ref, = refs

    @pl.when(pl.program_id(1) == 0)
    def _():
        tail_ref[...] = jnp.zeros_like(tail_ref)

    row = lax.broadcasted_iota(jnp.int32, (TM, CH), 0)

    def prev1(u, cs):
        return jnp.where(row == 0, tail_ref[1:2, cs], pltpu.roll(u, 1, 0))

    def prev2(u, cs):
        return jnp.where(row == 0, tail_ref[0:1, cs],
                         jnp.where(row == 1, tail_ref[1:2, cs], pltpu.roll(u, 2, 0)))

    def keep_tail(u, cs):
        tail_ref[0:2, cs] = u[TM - 2:TM, :]

    for t in range(tiles):
        x = x_ref[0, t * TM:(t + 1) * TM, :]
        h = _rms(x, gpre_ref[...]).astype(BF16)
        acc = _conv_chunks(h, win_ref, wconv_ref, wout_ref, prev1, prev2, keep_tail)
        out = x + _rms(acc, gpost_ref[...])
        if residue_major_out:
            for r, piece in enumerate(_to_residue_major(slab_ref, out)):
                y_ref[0, 0, r, t * SUB:(t + 1) * SUB, :] = piece
        else:
            y_ref[0, t * TM:(t + 1) * TM, :] = out
    st_ref[0] = tail_ref[0:2, :]

    if shift_rows:
        _shift_sample_caches(cache_refs, new_ref, out_cache_refs, window_ref, shift_rows)


def _conv_layer_prompt(x, g_pre, w_in, w_conv, w_out, g_post, layer, residue_major_out, shift=None):
    bsz, seq, _ = x.shape
    tiles = 1 if shift is not None else 2
    rows = tiles * TM
    nt = seq // rows
    n_steps = bsz * nt
    scratch = [pltpu.VMEM((8, D_MODEL), F32)]
    if residue_major_out:
        y_spec = _set_spec(D_MODEL, tiles)
        y_shape = jax.ShapeDtypeStruct((bsz, seq // TA, NRES, BAND, D_MODEL), F32)
        scratch.append(pltpu.VMEM((D_MODEL // HEAD_DIM, 2, TM, HEAD_DIM), F32))
    else:
        y_spec = pl.BlockSpec((1, rows, D_MODEL), lambda b, i: (b, i, 0))
        y_shape = jax.ShapeDtypeStruct((bsz, seq, D_MODEL), F32)
    extra_in, extra_in_specs, extra_out, extra_out_specs, shift_rows = [], [], [], [], 0
    if shift is not None:
        caches, new_rows = shift
        n_s = caches[0].shape[0]
        shift_rows = -(-n_s // n_steps)
        assert n_s % shift_rows == 0
        last = n_s // shift_rows - 1
        blk = lambda a: pl.BlockSpec(
            (shift_rows,) + a.shape[1:],
            lambda b, i: (jnp.minimum(b * nt + i, last),) + (0,) * (a.ndim - 1))
        win_sds = jax.ShapeDtypeStruct((n_s, 2 * N_GROUPS * KV_HEADS, BAND, HEAD_DIM), BF16)
        extra_in = list(caches) + [new_rows]
        extra_in_specs = [blk(a) for a in extra_in]
        extra_out = [jax.ShapeDtypeStruct(c.shape, c.dtype) for c in caches] + [win_sds]
        extra_out_specs = [blk(a) for a in extra_out]
    return pl.pallas_call(
        functools.partial(_conv_prompt_kernel, residue_major_out=residue_major_out,
                          shift_rows=shift_rows, tiles=tiles),
        grid=(bsz, nt),
        in_specs=[
            pl.BlockSpec((1, rows, D_MODEL), lambda b, i: (b, i, 0)),
            _const_spec((1, D_MODEL)),
            _layer_spec(w_in, layer),
            _const_spec((3, D_MODEL)),
            _layer_spec(w_out, layer),
            _const_spec((1, D_MODEL)),
        ] + extra_in_specs,
        out_specs=[y_spec, pl.BlockSpec((1, 2, D_MODEL), lambda b, i: (b, 0, 0))] + extra_out_specs,
        out_shape=[y_shape, jax.ShapeDtypeStruct((bsz, 2, D_MODEL), F32)] + extra_out,
        scratch_shapes=scratch,
        compiler_params=_params(("arbitrary", "arbitrary")),
        name="conv_layer_prompt",
    )(x, g_pre, w_in, w_conv, w_out, g_post, *extra_in)


def _conv_sample_kernel(x_ref, p0_ref, p1_ref, gpre_ref, win_ref, wconv_ref, wout_ref, gpost_ref,
                        y_ref, u_ref):
    x = x_ref[...]
    h = _rms(x, gpre_ref[...]).astype(BF16)
    def keep_u(u, cs):
        u_ref[:, cs] = u

    acc = _conv_chunks(h, win_ref, wconv_ref, wout_ref,
                       lambda u, cs: p1_ref[:, cs], lambda u, cs: p0_ref[:, cs], keep_u)
    y_ref[...] = x + _rms(acc, gpost_ref[...])


def _conv_layer_sample(x, past0, past1, g_pre, w_in, w_conv, w_out, g_post, layer):
    n = x.shape[0]
    return pl.pallas_call(
        _conv_sample_kernel,
        grid=(1,),
        in_specs=[_whole_spec(x), _whole_spec(past0), _whole_spec(past1), _whole_spec(g_pre),
                  _layer_spec(w_in, layer), _whole_spec(w_conv), _layer_spec(w_out, layer),
                  _whole_spec(g_post)],
        out_shape=[jax.ShapeDtypeStruct((n, D_MODEL), F32)] * 2,
        compiler_params=_params(),
        name="conv_layer_sample",
    )(x, past0, past1, g_pre, w_in, w_conv, w_out, g_post)


def _kv_prompt_kernel(x_ref, g_ref, w_ref, cos_ref, sin_ref, kv0_ref, kv1_ref, kv2_ref,
                      c0_ref, c1_ref, c2_ref, slab_ref, *, first_cache_tile):
    h = _rms(_load_set(x_ref), g_ref[...]).astype(BF16)
    cos = jnp.concatenate([cos_ref[0, r] for r in range(NRES)], axis=0)
    sin = jnp.concatenate([sin_ref[0, r] for r in range(NRES)], axis=0)
    kv = _dot(h, w_ref[...])
    kv_refs = (kv0_ref, kv1_ref, kv2_ref)
    fulls = []
    for g in range(N_GROUPS):
        ks, vs = [], []
        for hh in range(KV_HEADS):
            c = (g * KV_HEADS + hh) * HEAD_DIM
            k = _rope(kv[:, c:c + HEAD_DIM], cos, sin)
            v = kv[:, KV_HALF + c:KV_HALF + c + HEAD_DIM]
            both = jnp.concatenate([k, v], axis=-1).astype(GROUP_DTYPES[g])
            for r in range(NRES):
                kv_refs[g][0, hh, 0, r] = both[r * SUB:(r + 1) * SUB]
            ks.append(k)
            vs.append(v)
        fulls.append(jnp.concatenate(ks + vs, axis=-1))

    @pl.when(pl.program_id(1) >= first_cache_tile)
    def _():
        for g, c_ref in enumerate((c0_ref, c1_ref, c2_ref)):
            rows = min(TM, WINDOWS[g])
            c_ref[0] = _to_token_order(slab_ref, fulls[g])[TM - rows:, :]


def _kv_proj_prompt(x5, g, w_kv, cos5, sin5):
    bsz, nta = x5.shape[:2]
    seq = nta * TA
    nt = seq // TM
    per = TA // TM
    cache_specs, cache_shapes, firsts = [], [], []
    for grp in range(N_GROUPS):
        w = min(WINDOWS[grp], seq)
        rows = min(TM, w)
        first = nt - w // rows
        firsts.append(first)
        cache_specs.append(pl.BlockSpec(
            (1, rows, 4 * HEAD_DIM), lambda b, i, first=first: (b, jnp.maximum(i - first, 0), 0)))
        cache_shapes.append(jax.ShapeDtypeStruct((bsz, w, 4 * HEAD_DIM), F32))
    kv_spec = pl.BlockSpec((1, KV_HEADS, 1, NRES, SUB, KW),
                           lambda b, i: (b, 0, i // per, 0, i % per, 0))
    kv_shapes = [jax.ShapeDtypeStruct((bsz, KV_HEADS, nta, NRES, BAND, KW), dt)
                 for dt in GROUP_DTYPES]
    tab_spec = pl.BlockSpec((1, NRES, SUB, HEAD_DIM), lambda b, i: (i // per, 0, i % per, 0))
    outs = pl.pallas_call(
        functools.partial(_kv_prompt_kernel, first_cache_tile=min(firsts)),
        grid=(bsz, nt),
        in_specs=[_set_spec(D_MODEL), _const_spec((1, D_MODEL)),
                  _const_spec((D_MODEL, 2 * KV_HALF)), tab_spec, tab_spec],
        out_specs=[kv_spec] * N_GROUPS + cache_specs,
        out_shape=kv_shapes + cache_shapes,
        scratch_shapes=[pltpu.VMEM((4, 2, TM, HEAD_DIM), F32)],
        compiler_params=_params(("arbitrary", "arbitrary")),
        name="kv_proj_prompt",
    )(x5, g, w_kv, cos5, sin5)
    kvs = [o.reshape(bsz, KV_HEADS, seq, KW) for o in outs[:N_GROUPS]]
    return kvs, outs[N_GROUPS:]


def _q_project(x, g_ref, w_ref, cos_ref, sin_ref, q_refs, z_ref, rows=slice(0, TM)):
    h = _rms(x, g_ref[...]).astype(BF16)
    cos = cos_ref[rows, :]
    sin = sin_ref[rows, :]
    units = [(g, kh) for g in range(N_GROUPS) for kh in range(KV_HEADS)]
    project = lambda u: _dot(h, w_ref[:, (u[0] * KV_HEADS + u[1]) * QW:(u[0] * KV_HEADS + u[1] + 1) * QW])
    cur = project(units[0])
    for n, (g, kh) in enumerate(units):
        nxt = project(units[n + 1]) if n + 1 < len(units) else _dot(h, w_ref[:, Q_WIDTH:])
        heads = [_rope(cur[:, j * HEAD_DIM:(j + 1) * HEAD_DIM], cos, sin) for j in range(Q_PER_KV)]
        q_refs[g][0, kh, rows, :] = jnp.concatenate(heads, axis=-1).astype(GROUP_DTYPES[g])
        cur = nxt
    z_ref[0, rows, :] = cur.astype(BF16)


def _q_prompt_kernel(x_ref, g_ref, w_ref, cos_ref, sin_ref, q0_ref, q1_ref, q2_ref, z_ref, *, tiles):
    for t in range(tiles):
        rows = slice(t * TM, (t + 1) * TM)
        _q_project(x_ref[0, rows, :], g_ref, w_ref, cos_ref, sin_ref, (q0_ref, q1_ref, q2_ref),
                   z_ref, rows)


def _q_specs(bsz, seq, w_in, layer, tiles=1):
    rows = tiles * TM
    x_spec = pl.BlockSpec((1, rows, D_MODEL), lambda b, i: (b, i, 0))
    tab_spec = pl.BlockSpec((rows, HEAD_DIM), lambda b, i: (i, 0))
    in_specs = [_const_spec((1, D_MODEL)), _layer_spec(w_in, layer), tab_spec, tab_spec]
    out_specs = [pl.BlockSpec((1, KV_HEADS, rows, QW), lambda b, i: (b, 0, i, 0))] * N_GROUPS + [x_spec]
    out_shape = ([jax.ShapeDtypeStruct((bsz, KV_HEADS, seq, QW), dt) for dt in GROUP_DTYPES]
                 + [jax.ShapeDtypeStruct((bsz, seq, D_MODEL), BF16)])
    return x_spec, in_specs, out_specs, out_shape


def _q_proj_prompt(x, g, w_in, layer, cos_q, sin_q):
    bsz, seq, _ = x.shape
    tiles = 2
    x_spec, in_specs, out_specs, out_shape = _q_specs(bsz, seq, w_in, layer, tiles)
    return pl.pallas_call(
        functools.partial(_q_prompt_kernel, tiles=tiles),
        grid=(bsz, seq // (tiles * TM)),
        in_specs=[x_spec] + in_specs,
        out_specs=out_specs,
        out_shape=out_shape,
        compiler_params=_params(("arbitrary", "arbitrary")),
        name="q_proj_prompt",
    )(x, g, w_in, cos_q, sin_q)


def _block_runs(g, blk):
    if g == 2:
        return [(blk * BAND, BAND)]
    if g == 1:
        r4, c = blk
        return [((4 * mm + r4) * BAND + 32 * c, 32) for mm in range(4)]
    return [(r * BAND + 8 * blk, 8) for r in range(NRES)]


def _block_pos(g):
    a = jnp.arange(BAND)
    if g == 2:
        return a
    if g == 1:
        return 4 * (a % 32) + a // 32
    return NRES * (a % 8) + a // 8


def _band_bias():
    out = []
    for g in range(N_GROUPS):
        pos = _block_pos(g)
        qpos = jnp.tile(pos, Q_PER_KV)[:, None]
        ok_prev = pos[None, :] >= qpos
        ok_cur = pos[None, :] <= qpos
        neg = jnp.full(ok_cur.shape, NEG, F32)
        cur = jnp.where(ok_cur, 0.0, neg)
        out.append(jnp.concatenate([jnp.where(ok_prev, 0.0, neg), cur], axis=1))
        out.append(jnp.concatenate([neg, cur], axis=1))
    return jnp.stack(out)


def _attn_prompt_kernel(q0_ref, q1_ref, q2_ref, kv0_ref, kv1_ref, kv2_ref, bias_ref, o_ref,
                        acc_ref, m_ref, l_ref, biasf_ref, p0_ref, p1_ref, p2_ref):
    q_refs = (q0_ref, q1_ref, q2_ref)
    kv_refs = (kv0_ref, kv1_ref, kv2_ref)
    prev_refs = (p0_ref, p1_ref, p2_ref)
    prev_rows = (8, 32, BAND)
    first = pl.program_id(2) == 0

    @pl.when(first)
    def _():
        for p_ref in prev_refs:
            p_ref[...] = jnp.zeros_like(p_ref)

    for g in range(N_GROUPS):
        biasf_ref[g] = jnp.where(first, bias_ref[2 * g + 1], bias_ref[2 * g])
    ones = jnp.ones((2 * BAND, HEAD_DIM), BF16)
    m_rows = Q_PER_KV * BAND

    def gather(read, runs):
        return jnp.concatenate([read(rs, n) for rs, n in runs], axis=0)

    def heads_state(ref, runs):
        return jnp.concatenate(
            [gather(lambda rs, n, j=j: ref[j, rs:rs + n, :], runs) for j in range(Q_PER_KV)], axis=0)

    def scatter_state(ref, val, runs):
        for j in range(Q_PER_KV):
            off = j * BAND
            for rs, n in runs:
                ref[j, rs:rs + n, :] = val[off:off + n]
                off += n

    for g in range(N_GROUPS):
        q_ref, kv_ref, p_ref = q_refs[g], kv_refs[g], prev_refs[g]
        if g == 2:
            blocks = [(r, None) for r in range(NRES)]
        elif g == 1:
            blocks = [((r4, c), (r4, c - 1) if c else None) for r4 in range(4) for c in range(4)]
        else:
            blocks = [(c, c - 1 if c else None) for c in range(NRES)]
        for blk, prev_blk in blocks:
            runs = _block_runs(g, blk)
            q4 = jnp.concatenate(
                [gather(lambda rs, n, j=j: q_ref[0, 0, rs:rs + n, j * HEAD_DIM:(j + 1) * HEAD_DIM],
                        runs) for j in range(Q_PER_KV)], axis=0).astype(BF16)
            if prev_blk is None:
                pr = prev_rows[g]
                if g == 2:
                    pruns = [(blk * BAND, BAND)]
                elif g == 1:
                    pruns = [((4 * mm + blk[0]) * pr, pr) for mm in range(4)]
                else:
                    pruns = [(r * pr, pr) for r in range(NRES)]
                read_prev = lambda cols: gather(lambda rs, n: p_ref[rs:rs + n, cols], pruns)
                bias = biasf_ref[g]
            else:
                pruns = _block_runs(g, prev_blk)
                read_prev = lambda cols: gather(lambda rs, n: kv_ref[0, 0, rs:rs + n, cols], pruns)
                bias = bias_ref[2 * g]
            read_cur = lambda cols: gather(lambda rs, n: kv_ref[0, 0, rs:rs + n, cols], runs)
            kcols, vcols = slice(0, HEAD_DIM), slice(HEAD_DIM, KW)
            k_cat = jnp.concatenate([read_prev(kcols), read_cur(kcols)], axis=0).astype(BF16)
            v_cat = jnp.concatenate([read_prev(vcols), read_cur(vcols)], axis=0).astype(BF16)

            s = _dot_nt(q4, k_cat) + bias
            m_new = jnp.broadcast_to(jnp.max(s, axis=-1, keepdims=True), (m_rows, HEAD_DIM))
            if g > 0:
                m_old = heads_state(m_ref, runs)
                m_new = jnp.maximum(m_old, m_new)
                alpha = jnp.exp2(m_old - m_new)
            p = jnp.exp2(jnp.concatenate([s[:, :BAND] - m_new, s[:, BAND:] - m_new], axis=1))
            ov = _dot(p.astype(BF16), jnp.concatenate([v_cat, ones], axis=1))
            acc_new, l_new = ov[:, :HEAD_DIM], ov[:, HEAD_DIM:]
            if g > 0:
                acc_new = alpha * heads_state(acc_ref, runs) + acc_new
                l_new = alpha * heads_state(l_ref, runs) + l_new
            if g < N_GROUPS - 1:
                scatter_state(acc_ref, acc_new, runs)
                scatter_state(m_ref, m_new, runs)
                scatter_state(l_ref, l_new, runs)
            else:
                o = (acc_new / l_new).astype(BF16)
                (rs, n), = runs
                for j in range(Q_PER_KV):
                    o_ref[0, rs:rs + n, j * HEAD_DIM:(j + 1) * HEAD_DIM] = o[j * BAND:(j + 1) * BAND]

    for g in range(N_GROUPS):
        pr = prev_rows[g]
        for r in range(NRES):
            prev_refs[g][r * pr:(r + 1) * pr, :] = kv_refs[g][0, 0, (r + 1) * BAND - pr:(r + 1) * BAND, :]


def _attn_prompt(qs, kvs, bias):
    bsz, _, seq, _ = qs[0].shape
    blk = lambda width: pl.BlockSpec((1, 1, TA, width), lambda b, h, i: (b, h, i, 0))
    return pl.pallas_call(
        _attn_prompt_kernel,
        grid=(bsz, KV_HEADS, seq // TA),
        in_specs=[blk(QW)] * N_GROUPS + [blk(KW)] * N_GROUPS + [_const_spec(bias.shape)],
        out_specs=pl.BlockSpec((1, TA, QW), lambda b, h, i: (b, i, h)),
        out_shape=jax.ShapeDtypeStruct((bsz, seq, KV_HEADS * QW), BF16),
        scratch_shapes=[
            pltpu.VMEM((Q_PER_KV, TA, HEAD_DIM), F32),
            pltpu.VMEM((Q_PER_KV, TA, HEAD_DIM), F32),
            pltpu.VMEM((Q_PER_KV, TA, HEAD_DIM), F32),
            pltpu.VMEM((N_GROUPS, Q_PER_KV * BAND, 2 * BAND), F32),
            pltpu.VMEM((NRES * 8, KW), GROUP_DTYPES[0]),
            pltpu.VMEM((NRES * 32, KW), GROUP_DTYPES[1]),
            pltpu.VMEM((NRES * BAND, KW), GROUP_DTYPES[2]),
        ],
        compiler_params=_params(("arbitrary", "arbitrary", "arbitrary")),
        name="attn_prompt",
    )(*qs, *kvs, bias)


def _out_prompt_kernel(o_ref, z_ref, x_ref, wout_ref, gpost_ref, y_ref, slab_ref, *, tiles):
    for t in range(tiles):
        gate = _silu(_load_set(z_ref, t).astype(F32))
        y = (_load_set(o_ref, t).astype(F32) * gate).astype(BF16)
        res = _load_set(x_ref, t) + _rms(_dot(y, wout_ref[...]), gpost_ref[...])
        y_ref[0, t * TM:(t + 1) * TM, :] = _to_token_order(slab_ref, res)


def _out_q_prompt_kernel(o_ref, z_ref, x_ref, wout_ref, gpost_ref, gpre_ref, win_ref, cos_ref, sin_ref,
                         y_ref, q0_ref, q1_ref, q2_ref, zn_ref):
    y = (o_ref[0].astype(F32) * _silu(z_ref[0].astype(F32))).astype(BF16)
    x_new = x_ref[0] + _rms(_dot(y, wout_ref[...]), gpost_ref[...])
    y_ref[0] = x_new
    _q_project(x_new, gpre_ref, win_ref, cos_ref, sin_ref, (q0_ref, q1_ref, q2_ref), zn_ref)


def _out_q_proj_prompt(o, z, x, w_out, g_post, g_pre_next, w_in, layer, cos_q, sin_q):
    bsz, seq, _ = x.shape
    x_spec, q_in_specs, q_out_specs, q_out_shape = _q_specs(bsz, seq, w_in, layer + 1)
    return pl.pallas_call(
        _out_q_prompt_kernel,
        grid=(bsz, seq // TM),
        in_specs=[x_spec] * 3 + [_layer_spec(w_out, layer), _const_spec((1, D_MODEL))]
        + q_in_specs,
        out_specs=[x_spec] + q_out_specs,
        out_shape=[jax.ShapeDtypeStruct((bsz, seq, D_MODEL), F32)] + q_out_shape,
        compiler_params=_params(("arbitrary", "arbitrary")),
        name="out_q_proj_prompt",
    )(o, z, x, w_out, g_post, g_pre_next, w_in, cos_q, sin_q)


def _out_proj_prompt(o, z, x, w_out, layer, g_post):
    bsz, seq, _ = x.shape
    view = lambda a: a.reshape(bsz, seq // TA, NRES, BAND, D_MODEL)
    tiles = 2
    return pl.pallas_call(
        functools.partial(_out_prompt_kernel, tiles=tiles),
        grid=(bsz, seq // (tiles * TM)),
        in_specs=[_set_spec(D_MODEL, tiles)] * 3
        + [_layer_spec(w_out, layer), _const_spec((1, D_MODEL))],
        out_specs=pl.BlockSpec((1, tiles * TM, D_MODEL), lambda b, i: (b, i, 0)),
        out_shape=jax.ShapeDtypeStruct((bsz, seq, D_MODEL), F32),
        scratch_shapes=[pltpu.VMEM((D_MODEL // HEAD_DIM, 2, TM, HEAD_DIM), F32)],
        compiler_params=_params(("arbitrary", "arbitrary")),
        name="out_proj_prompt",
    )(view(o), view(z), view(x), w_out, g_post)


def _proj_sample_kernel(x_ref, g_ref, w_ref, cos_ref, sin_ref, y_ref, *, rope_heads):
    h = _rms(x_ref[...], g_ref[...]).astype(BF16)
    y = _dot(h, w_ref[...])
    cos = cos_ref[...]
    sin = sin_ref[...]
    for hd in range(rope_heads):
        cs = slice(hd * HEAD_DIM, (hd + 1) * HEAD_DIM)
        y_ref[:, cs] = _rope(y[:, cs], cos, sin)
    rest = rope_heads * HEAD_DIM
    y_ref[:, rest:] = y[:, rest:]


def _proj_sample(x, g, w, cos, sin, rope_heads, layer=None):
    n = x.shape[0]
    w_spec = _whole_spec(w) if layer is None else _layer_spec(w, layer)
    return pl.pallas_call(
        functools.partial(_proj_sample_kernel, rope_heads=rope_heads),
        grid=(1,),
        in_specs=[_whole_spec(x), _whole_spec(g), w_spec, _whole_spec(cos), _whole_spec(sin)],
        out_shape=jax.ShapeDtypeStruct((n, w.shape[-1]), F32),
        compiler_params=_params(),
        name=f"proj_sample_{rope_heads}",
    )(x, g, w, cos, sin)


def _attn_sample_row(qz, kvn, window_kv):
    pieces = []
    for kh in range(KV_HEADS):
        scores, s_new, vals, v_new = [], [], [], []
        for g in range(N_GROUPS):
            c = (g * HEADS + kh * Q_PER_KV) * HEAD_DIM
            q4 = jnp.concatenate(
                [qz[:, c + j * HEAD_DIM:c + (j + 1) * HEAD_DIM] for j in range(Q_PER_KV)], axis=0)
            q8 = jnp.concatenate([q4, q4], axis=0)
            k_rows, v_rows = window_kv(g, kh)
            kc = (g * KV_HEADS + kh) * HEAD_DIM
            k_n = kvn[:, kc:kc + HEAD_DIM]
            v_n = kvn[:, KV_HALF + kc:KV_HALF + kc + HEAD_DIM]
            scores.append(_dot_nt(q8.astype(BF16), k_rows))
            s_new.append(jnp.sum(q8 * k_n, axis=-1, keepdims=True))
            vals.append(v_rows)
            v_new.append(v_n)
        m = functools.reduce(jnp.maximum,
                             [jnp.max(s, axis=-1, keepdims=True) for s in scores] + s_new)
        ps = [jnp.exp(s - m) for s in scores]
        pn = [jnp.exp(s - m) for s in s_new]
        l = sum(jnp.sum(p, axis=-1, keepdims=True) for p in ps) + sum(pn)
        o = sum(_dot(ps[g].astype(BF16), vals[g]) + pn[g] * v_new[g] for g in range(N_GROUPS)) / l
        pieces += [o[j:j + 1, :] for j in range(Q_PER_KV)]
    o_row = jnp.concatenate(pieces, axis=-1)
    return o_row * _silu(qz[:, Q_WIDTH:])


def _attn_sample_kernel(qz_ref, kvn_ref, win_ref, y_ref):
    for p in range(SAMPLE_ROWS):
        def window_kv(g, kh, p=p):
            slot = 2 * (g * KV_HEADS + kh)
            return win_ref[p, slot], win_ref[p, slot + 1]

        y_ref[p] = _attn_sample_row(qz_ref[p], kvn_ref[p], window_kv)


def _attn_sample(qz, kv_new, win):
    n = qz.shape[0]
    assert n % SAMPLE_ROWS == 0
    row_spec = lambda a: pl.BlockSpec((SAMPLE_ROWS, 1, a.shape[-1]), lambda b: (b, 0, 0))
    return pl.pallas_call(
        _attn_sample_kernel,
        grid=(n // SAMPLE_ROWS,),
        in_specs=[row_spec(qz), row_spec(kv_new),
                  pl.BlockSpec((SAMPLE_ROWS,) + win.shape[1:], lambda b: (b, 0, 0, 0))],
        out_specs=pl.BlockSpec((SAMPLE_ROWS, 1, D_MODEL), lambda b: (b, 0, 0)),
        out_shape=jax.ShapeDtypeStruct((n, 1, D_MODEL), F32),
        compiler_params=_params(("arbitrary",)),
        name="attn_sample",
    )(qz, kv_new, win)


def _out_sample_kernel(y_ref, x_ref, wout_ref, gpost_ref, o_ref):
    out = _dot(y_ref[...].astype(BF16), wout_ref[...])
    o_ref[...] = x_ref[...] + _rms(out, gpost_ref[...])


def _out_sample(y, x, w_out, layer, g_post):
    return pl.pallas_call(
        _out_sample_kernel,
        grid=(1,),
        in_specs=[_whole_spec(y), _whole_spec(x), _layer_spec(w_out, layer), _whole_spec(g_post)],
        out_shape=jax.ShapeDtypeStruct(x.shape, F32),
        compiler_params=_params(),
        name="out_sample",
    )(y, x, w_out, g_post)


def _inv_freq():
    return ROPE_THETA ** (-jnp.arange(HEAD_DIM // 2, dtype=F32) * 2.0 / HEAD_DIM)


def _full_width(cos, sin):
    return jnp.concatenate([cos, cos], axis=-1), jnp.concatenate([-sin, sin], axis=-1)


def _rope_tables(pos):
    ang = pos.astype(F32)[:, None] * _inv_freq()[None, :]
    return _full_width(jnp.cos(ang), jnp.sin(ang))


def _rope_tables_residue_major(seq):
    f = jnp.tile(_inv_freq(), 2)
    sign = jnp.where(jnp.arange(HEAD_DIM) < HEAD_DIM // 2, -1.0, 1.0).astype(F32)
    base = (jnp.arange(seq // TA)[:, None] * TA + NRES * jnp.arange(BAND)[None, :]).astype(F32)
    a = base[:, None, :, None] * f
    b = jnp.arange(NRES, dtype=F32)[None, :, None, None] * f
    cos_a, sin_a, cos_b, sin_b = jnp.cos(a), jnp.sin(a), jnp.cos(b), jnp.sin(b)
    cos = (cos_a * cos_b - sin_a * sin_b).reshape(seq, HEAD_DIM)
    sin = (sign * (sin_a * cos_b + cos_a * sin_b)).reshape(seq, HEAD_DIM)
    return cos, sin


def kernel(x_prompt, x_sample, state_conv, cache_kv_g0, cache_kv_g1, cache_kv_g2,
           a_norm_pre, a_w_in, a_conv, a_w_out, a_norm_post,
           kv_norm, w_kv, b_norm_pre, b_w_in, b_w_out, b_norm_post):
    bsz, seq, _ = x_prompt.shape
    n_s = x_sample.shape[0]
    n_a = a_w_in.shape[0]
    n_b = b_w_in.shape[0]
    assert x_sample.shape[1] == 1 and seq % TA == 0 and n_a >= 1 and n_b >= 1

    cos_p, sin_p = _rope_tables_residue_major(seq)
    cos_s, sin_s = _rope_tables(PAST_LEN + jnp.arange(1))
    row = lambda v: v.reshape(1, -1)

    a_w_in16 = a_w_in.astype(BF16)
    a_w_out16 = a_w_out.astype(BF16)
    w_kv16 = w_kv.astype(BF16)
    b_w_in16 = b_w_in.astype(BF16)
    b_w_out16 = b_w_out.astype(BF16)

    conv_args = [(row(a_norm_pre[layer]), a_w_in16, a_conv[layer], a_w_out16,
                  row(a_norm_post[layer]), layer) for layer in range(n_a)]

    hs = x_sample.reshape(n_s, D_MODEL)
    conv_s = []
    for layer in range(n_a):
        past0 = state_conv[layer, :, 0]
        past1 = state_conv[layer, :, 1]
        hs, u_s = _conv_layer_sample(hs, past0, past1, *conv_args[layer])
        conv_s.append(jnp.stack([past1, u_s], axis=1))
    kv_s = _proj_sample(hs, row(kv_norm), w_kv16, cos_s, sin_s, N_GROUPS * KV_HEADS)
    nr = 2 * KV_HEADS
    cache_rows = [c.reshape(n_s, c.shape[1] * nr, HEAD_DIM)
                  for c in (cache_kv_g0, cache_kv_g1, cache_kv_g2)]
    new_rows = jnp.stack([kv_s[:, :KV_HALF].reshape(n_s, N_GROUPS, KV_HEADS, HEAD_DIM),
                          kv_s[:, KV_HALF:].reshape(n_s, N_GROUPS, KV_HEADS, HEAD_DIM)],
                         axis=2).reshape(n_s, N_GROUPS, nr, HEAD_DIM)

    hp = x_prompt
    conv_p = []
    for layer in range(n_a):
        hp, st_p, *extra = _conv_layer_prompt(
            hp, *conv_args[layer], residue_major_out=(layer == n_a - 1),
            shift=(cache_rows, new_rows) if layer == 0 else None)
        conv_p.append(st_p)
        if layer == 0:
            *shifted, win = extra
            kv_new_s = [c.reshape(n_s, WINDOWS[g], 2, KV_HEADS, HEAD_DIM)
                        for g, c in enumerate(shifted)]

    kv_s3 = kv_s.reshape(n_s, 1, -1)
    for j in range(n_b):
        qz = _proj_sample(hs, row(b_norm_pre[j]), b_w_in16, cos_s * SCALE, sin_s * SCALE,
                          N_GROUPS * HEADS, layer=j)
        y_s = _attn_sample(qz.reshape(n_s, 1, -1), kv_s3, win)
        hs = _out_sample(y_s.reshape(n_s, D_MODEL), hs, b_w_out16, j, row(b_norm_post[j]))

    tab5 = lambda t: t.reshape(seq // TA, NRES, BAND, HEAD_DIM)
    kv_p, cs = _kv_proj_prompt(hp, row(kv_norm), w_kv16, tab5(cos_p), tab5(sin_p))
    kv_new_p = [c.reshape(bsz, c.shape[1], 2, KV_HEADS, HEAD_DIM) for c in cs]
    hp = hp.reshape(bsz, seq, D_MODEL)
    bias = _band_bias()
    cos_q = cos_p * (SCALE * LOG2E)
    sin_q = sin_p * (SCALE * LOG2E)
    *q, z = _q_proj_prompt(hp, row(b_norm_pre[0]), b_w_in16, 0, cos_q, sin_q)
    for j in range(n_b):
        g_post = row(b_norm_post[j])
        o = _attn_prompt(q, kv_p, bias)
        if j + 1 < n_b:
            hp, *q, z = _out_q_proj_prompt(o, z, hp, b_w_out16, g_post, row(b_norm_pre[j + 1]),
                                           b_w_in16, j, cos_q, sin_q)
        else:
            hp = _out_proj_prompt(o, z, hp, b_w_out16, j, g_post)

    return (hp, hs.reshape(n_s, 1, D_MODEL), jnp.stack(conv_p, axis=0),
            kv_new_p[0], kv_new_p[1], kv_new_p[2],
            jnp.stack(conv_s, axis=0), kv_new_s[0], kv_new_s[1], kv_new_s[2])
```

```python
import functools

import jax
import jax.numpy as jnp
from jax import lax
from jax.experimental import pallas as pl
from jax.experimental.pallas import tpu as pltpu

D_MODEL = 1024
HEAD_DIM = 128
N_GROUPS = 3
WINDOWS = (128, 512, 2048)
DILATIONS = (1, 4, 16)
HEADS = 8
KV_HEADS = 2
Q_PER_KV = HEADS // KV_HEADS
Q_WIDTH = N_GROUPS * HEADS * HEAD_DIM
KV_HALF = N_GROUPS * KV_HEADS * HEAD_DIM
PAST_LEN = 8192
ROPE_THETA = 10000.0
EPS = 1e-6
SCALE = HEAD_DIM ** -0.5
NEG = -1e30
LOG2E = 1.4426950408889634

BAND = 128
NRES = 16
TA = NRES * BAND
TM = 512
SUB = TM // NRES
CH = 256
SAMPLE_ROWS = 8
QW = Q_PER_KV * HEAD_DIM
KW = 2 * HEAD_DIM
VMEM_LIMIT = 54 * 1024 * 1024

F32 = jnp.float32
BF16 = jnp.bfloat16
GROUP_DTYPES = (F32, BF16, BF16)


def _dot(a, b):
    return jnp.dot(a, b, preferred_element_type=F32)


def _dot_nt(a, b):
    return lax.dot_general(a, b, (((1,), (1,)), ((), ())), preferred_element_type=F32)


def _rms(x, g):
    ms = jnp.mean(x * x, axis=-1, keepdims=True)
    return x * lax.rsqrt(ms + EPS) * g


def _silu(z):
    return z * jax.nn.sigmoid(z)


def _rope(x, cos, sin_signed):
    return x * cos + pltpu.roll(x, HEAD_DIM // 2, 1) * sin_signed


def _params(sem=None, flags=None):
    return pltpu.CompilerParams(dimension_semantics=sem, vmem_limit_bytes=VMEM_LIMIT, flags=flags)


def _const_spec(shape):
    zeros = (0,) * len(shape)
    return pl.BlockSpec(shape, lambda *_: zeros, pipeline_mode=pl.Buffered(1))


def _layer_spec(w, layer):
    return pl.BlockSpec((None,) + w.shape[1:], lambda *_: (layer, 0, 0),
                        pipeline_mode=pl.Buffered(1))


def _whole_spec(a):
    zeros = (0,) * a.ndim
    return pl.BlockSpec(a.shape, lambda *_: zeros)


def _split_residues(slab_ref, val):
    q = TM // 4
    slab_ref[0] = val
    for m in range(4):
        slab_ref[1, m * q:(m + 1) * q, :] = slab_ref[0, pl.ds(m, q, stride=4), :]
    return {4 * m2 + m: slab_ref[1, pl.ds(m * q + m2, SUB, stride=4), :]
            for m in range(4) for m2 in range(4)}


def _merge_residues(slab_ref, pieces):
    q = TM // 4
    for m in range(4):
        for m2 in range(4):
            slab_ref[1, pl.ds(m * q + m2, SUB, stride=4), :] = pieces[4 * m2 + m]
    for m in range(4):
        slab_ref[0, pl.ds(m, q, stride=4), :] = slab_ref[1, m * q:(m + 1) * q, :]
    return slab_ref[0]


def _to_residue_major(slab_ref, val):
    ncol = val.shape[1] // HEAD_DIM
    cols = [_split_residues(slab_ref.at[c], val[:, c * HEAD_DIM:(c + 1) * HEAD_DIM])
            for c in range(ncol)]
    return [jnp.concatenate([cols[c][r] for c in range(ncol)], axis=-1) for r in range(NRES)]


def _to_token_order(slab_ref, val):
    ncol = val.shape[1] // HEAD_DIM
    cols = [_merge_residues(slab_ref.at[c],
                            {r: val[r * SUB:(r + 1) * SUB, c * HEAD_DIM:(c + 1) * HEAD_DIM]
                             for r in range(NRES)})
            for c in range(ncol)]
    return jnp.concatenate(cols, axis=-1)


def _set_spec(width, sets=1):
    per = TA // (sets * TM)
    return pl.BlockSpec((1, 1, NRES, sets * SUB, width), lambda b, i: (b, i // per, 0, i % per, 0))


def _load_set(ref, t=0):
    return jnp.concatenate([ref[0, 0, r, t * SUB:(t + 1) * SUB, :] for r in range(NRES)], axis=0)


def _conv_project_chunk(h, win_ref, j):
    return [_dot(h, win_ref[:, k * D_MODEL + j * CH:k * D_MODEL + (j + 1) * CH]) for k in range(4)]


def _conv_mix_chunk(proj, wconv_ref, j, u_prev1, u_prev2):
    cs = slice(j * CH, (j + 1) * CH)
    b_gate, c_gate, u_in, z = proj
    u = c_gate * u_in
    conv = (u_prev2(u, cs) * wconv_ref[0:1, cs] + u_prev1(u, cs) * wconv_ref[1:2, cs]
            + u * wconv_ref[2:3, cs])
    return u, b_gate * conv * _silu(z)


def _conv_chunks(h, win_ref, wconv_ref, wout_ref, u_prev1, u_prev2, keep_u):
    nch = D_MODEL // CH
    acc = jnp.zeros((h.shape[0], D_MODEL), F32)
    proj = _conv_project_chunk(h, win_ref, 0)
    for j in range(nch):
        nxt = _conv_project_chunk(h, win_ref, j + 1) if j + 1 < nch else None
        u, y = _conv_mix_chunk(proj, wconv_ref, j, u_prev1, u_prev2)
        keep_u(u, slice(j * CH, (j + 1) * CH))
        acc = acc + _dot(y.astype(BF16), wout_ref[j * CH:(j + 1) * CH, :])
        proj = nxt
    return acc


def _shift_sample_caches(cache_refs, new_ref, out_refs, win_ref, rows):
    nr = 2 * KV_HEADS
    for p in range(rows):
        for g in range(N_GROUPS):
            c_ref, o_ref = cache_refs[g], out_refs[g]
            keep = (WINDOWS[g] - 1) * nr
            o_ref[p, 0:keep, :] = c_ref[p, nr:nr + keep, :]
            o_ref[p, keep:keep + nr, :] = new_ref[p, g]
            stride = nr * DILATIONS[g]
            for kh in range(KV_HEADS):
                slot = 2 * (g * KV_HEADS + kh)
                win_ref[p, slot] = c_ref[p, pl.ds(kh, BAND, stride=stride), :].astype(BF16)
                win_ref[p, slot + 1] = c_ref[p, pl.ds(KV_HEADS + kh, BAND, stride=stride), :].astype(BF16)


def _conv_prompt_kernel(*refs, residue_major_out, shift_rows, tiles):
    x_ref, gpre_ref, win_ref, wconv_ref, wout_ref, gpost_ref = refs[:6]
    refs = refs[6:]
    if shift_rows:
        cache_refs, new_ref, refs = refs[:3], refs[3], refs[4:]
    (y_ref, st_ref), refs = refs[:2], refs[2:]
    if shift_rows:
        out_cache_refs, window_ref, refs = refs[:3], refs[3], refs[4:]
    tail_ref, refs = refs[0], refs[1:]
    if residue_major_out:
        slab_ref, = refs

    @pl.when(pl.program_id(1) == 0)
    def _():
        tail_ref[...] = jnp.zeros_like(tail_ref)

    row = lax.broadcasted_iota(jnp.int32, (TM, CH), 0)

    def prev1(u, cs):
        return jnp.where(row == 0, tail_ref[1:2, cs], pltpu.roll(u, 1, 0))

    def prev2(u, cs):
        return jnp.where(row == 0, tail_ref[0:1, cs],
                         jnp.where(row == 1, tail_ref[1:2, cs], pltpu.roll(u, 2, 0)))

    def keep_tail(u, cs):
        tail_ref[0:2, cs] = u[TM - 2:TM, :]

    nch = D_MODEL // CH
    xs = [x_ref[0, t * TM:(t + 1) * TM, :] for t in range(tiles)]
    hs = [_rms(x, gpre_ref[...]).astype(BF16) for x in xs]
    units = [(t, j) for t in range(tiles) for j in range(nch)]
    proj = _conv_project_chunk(hs[0], win_ref, 0)
    ys = []
    for n, (t, j) in enumerate(units):
        if n + 1 < len(units):
            tn, jn = units[n + 1]
            nxt = _conv_project_chunk(hs[tn], win_ref, jn)
        u, y = _conv_mix_chunk(proj, wconv_ref, j, prev1, prev2)
        keep_tail(u, slice(j * CH, (j + 1) * CH))
        ys.append(y.astype(BF16))
        proj = nxt
        if j == nch - 1:
            acc = _dot(jnp.concatenate(ys, axis=-1), wout_ref[...])
            ys = []
            out = xs[t] + _rms(acc, gpost_ref[...])
            if residue_major_out:
                for r, piece in enumerate(_to_residue_major(slab_ref, out)):
                    y_ref[0, 0, r, t * SUB:(t + 1) * SUB, :] = piece
            else:
                y_ref[0, t * TM:(t + 1) * TM, :] = out
    st_ref[0] = tail_ref[0:2, :]

    if shift_rows:
        _shift_sample_caches(cache_refs, new_ref, out_cache_refs, window_ref, shift_rows)


def _conv_layer_prompt(x, g_pre, w_in, w_conv, w_out, g_post, layer, residue_major_out, shift=None):
    bsz, seq, _ = x.shape
    tiles = 1 if shift is not None else 2
    rows = tiles * TM
    nt = seq // rows
    n_steps = bsz * nt
    scratch = [pltpu.VMEM((8, D_MODEL), F32)]
    if residue_major_out:
        y_spec = _set_spec(D_MODEL, tiles)
        y_shape = jax.ShapeDtypeStruct((bsz, seq // TA, NRES, BAND, D_MODEL), F32)
        scratch.append(pltpu.VMEM((D_MODEL // HEAD_DIM, 2, TM, HEAD_DIM), F32))
    else:
        y_spec = pl.BlockSpec((1, rows, D_MODEL), lambda b, i: (b, i, 0))
        y_shape = jax.ShapeDtypeStruct((bsz, seq, D_MODEL), F32)
    extra_in, extra_in_specs, extra_out, extra_out_specs, shift_rows = [], [], [], [], 0
    if shift is not None:
        caches, new_rows = shift
        n_s = caches[0].shape[0]
        shift_rows = -(-n_s // n_steps)
        assert n_s % shift_rows == 0
        last = n_s // shift_rows - 1
        blk = lambda a: pl.BlockSpec(
            (shift_rows,) + a.shape[1:],
            lambda b, i: (jnp.minimum(b * nt + i, last),) + (0,) * (a.ndim - 1))
        win_sds = jax.ShapeDtypeStruct((n_s, 2 * N_GROUPS * KV_HEADS, BAND, HEAD_DIM), BF16)
        extra_in = list(caches) + [new_rows]
        extra_in_specs = [blk(a) for a in extra_in]
        extra_out = [jax.ShapeDtypeStruct(c.shape, c.dtype) for c in caches] + [win_sds]
        extra_out_specs = [blk(a) for a in extra_out]
    return pl.pallas_call(
        functools.partial(_conv_prompt_kernel, residue_major_out=residue_major_out,
                          shift_rows=shift_rows, tiles=tiles),
        grid=(bsz, nt),
        in_specs=[
            pl.BlockSpec((1, rows, D_MODEL), lambda b, i: (b, i, 0)),
            _const_spec((1, D_MODEL)),
            _layer_spec(w_in, layer),
            _const_spec((3, D_MODEL)),
            _layer_spec(w_out, layer),
            _const_spec((1, D_MODEL)),
        ] + extra_in_specs,
        out_specs=[y_spec, pl.BlockSpec((1, 2, D_MODEL), lambda b, i: (b, 0, 0))] + extra_out_specs,
        out_shape=[y_shape, jax.ShapeDtypeStruct((bsz, 2, D_MODEL), F32)] + extra_out,
        scratch_shapes=scratch,
        compiler_params=_params(("arbitrary", "arbitrary")),
        name="conv_layer_prompt",
    )(x, g_pre, w_in, w_conv, w_out, g_post, *extra_in)


def _conv_sample_kernel(x_ref, p0_ref, p1_ref, gpre_ref, win_ref, wconv_ref, wout_ref, gpost_ref,
                        y_ref, u_ref):
    x = x_ref[...]
    h = _rms(x, gpre_ref[...]).astype(BF16)
    def keep_u(u, cs):
        u_ref[:, cs] = u

    acc = _conv_chunks(h, win_ref, wconv_ref, wout_ref,
                       lambda u, cs: p1_ref[:, cs], lambda u, cs: p0_ref[:, cs], keep_u)
    y_ref[...] = x + _rms(acc, gpost_ref[...])


def _conv_layer_sample(x, past0, past1, g_pre, w_in, w_conv, w_out, g_post, layer):
    n = x.shape[0]
    return pl.pallas_call(
        _conv_sample_kernel,
        grid=(1,),
        in_specs=[_whole_spec(x), _whole_spec(past0), _whole_spec(past1), _whole_spec(g_pre),
                  _layer_spec(w_in, layer), _whole_spec(w_conv), _layer_spec(w_out, layer),
                  _whole_spec(g_post)],
        out_shape=[jax.ShapeDtypeStruct((n, D_MODEL), F32)] * 2,
        compiler_params=_params(),
        name="conv_layer_sample",
    )(x, past0, past1, g_pre, w_in, w_conv, w_out, g_post)


def _kv_prompt_kernel(x_ref, g_ref, w_ref, cos_ref, sin_ref, kv0_ref, kv1_ref, kv2_ref,
                      c0_ref, c1_ref, c2_ref, slab_ref, *, first_cache_tile):
    h = _rms(_load_set(x_ref), g_ref[...]).astype(BF16)
    cos = jnp.concatenate([cos_ref[0, r] for r in range(NRES)], axis=0)
    sin = jnp.concatenate([sin_ref[0, r] for r in range(NRES)], axis=0)
    kv = _dot(h, w_ref[...])
    kv_refs = (kv0_ref, kv1_ref, kv2_ref)
    fulls = []
    for g in range(N_GROUPS):
        ks, vs = [], []
        for hh in range(KV_HEADS):
            c = (g * KV_HEADS + hh) * HEAD_DIM
            k = _rope(kv[:, c:c + HEAD_DIM], cos, sin)
            v = kv[:, KV_HALF + c:KV_HALF + c + HEAD_DIM]
            both = jnp.concatenate([k, v], axis=-1).astype(GROUP_DTYPES[g])
            for r in range(NRES):
                kv_refs[g][0, hh, 0, r] = both[r * SUB:(r + 1) * SUB]
            ks.append(k)
            vs.append(v)
        fulls.append(jnp.concatenate(ks + vs, axis=-1))

    @pl.when(pl.program_id(1) >= first_cache_tile)
    def _():
        for g, c_ref in enumerate((c0_ref, c1_ref, c2_ref)):
            rows = min(TM, WINDOWS[g])
            c_ref[0] = _to_token_order(slab_ref, fulls[g])[TM - rows:, :]


def _kv_proj_prompt(x5, g, w_kv, cos5, sin5):
    bsz, nta = x5.shape[:2]
    seq = nta * TA
    nt = seq // TM
    per = TA // TM
    cache_specs, cache_shapes, firsts = [], [], []
    for grp in range(N_GROUPS):
        w = min(WINDOWS[grp], seq)
        rows = min(TM, w)
        first = nt - w // rows
        firsts.append(first)
        cache_specs.append(pl.BlockSpec(
            (1, rows, 4 * HEAD_DIM), lambda b, i, first=first: (b, jnp.maximum(i - first, 0), 0)))
        cache_shapes.append(jax.ShapeDtypeStruct((bsz, w, 4 * HEAD_DIM), F32))
    kv_spec = pl.BlockSpec((1, KV_HEADS, 1, NRES, SUB, KW),
                           lambda b, i: (b, 0, i // per, 0, i % per, 0))
    kv_shapes = [jax.ShapeDtypeStruct((bsz, KV_HEADS, nta, NRES, BAND, KW), dt)
                 for dt in GROUP_DTYPES]
    tab_spec = pl.BlockSpec((1, NRES, SUB, HEAD_DIM), lambda b, i: (i // per, 0, i % per, 0))
    outs = pl.pallas_call(
        functools.partial(_kv_prompt_kernel, first_cache_tile=min(firsts)),
        grid=(bsz, nt),
        in_specs=[_set_spec(D_MODEL), _const_spec((1, D_MODEL)),
                  _const_spec((D_MODEL, 2 * KV_HALF)), tab_spec, tab_spec],
        out_specs=[kv_spec] * N_GROUPS + cache_specs,
        out_shape=kv_shapes + cache_shapes,
        scratch_shapes=[pltpu.VMEM((4, 2, TM, HEAD_DIM), F32)],
        compiler_params=_params(("arbitrary", "arbitrary")),
        name="kv_proj_prompt",
    )(x5, g, w_kv, cos5, sin5)
    kvs = [o.reshape(bsz, KV_HEADS, seq, KW) for o in outs[:N_GROUPS]]
    return kvs, outs[N_GROUPS:]


def _q_project(x, g_ref, w_ref, cos_ref, sin_ref, q_refs, z_ref, rows=slice(0, TM)):
    h = _rms(x, g_ref[...]).astype(BF16)
    cos = cos_ref[rows, :]
    sin = sin_ref[rows, :]
    units = [(g, kh) for g in range(N_GROUPS) for kh in range(KV_HEADS)]
    project = lambda u: _dot(h, w_ref[:, (u[0] * KV_HEADS + u[1]) * QW:(u[0] * KV_HEADS + u[1] + 1) * QW])
    cur = project(units[0])
    for n, (g, kh) in enumerate(units):
        nxt = project(units[n + 1]) if n + 1 < len(units) else _dot(h, w_ref[:, Q_WIDTH:])
        heads = [_rope(cur[:, j * HEAD_DIM:(j + 1) * HEAD_DIM], cos, sin) for j in range(Q_PER_KV)]
        q_refs[g][0, kh, rows, :] = jnp.concatenate(heads, axis=-1).astype(GROUP_DTYPES[g])
        cur = nxt
    z_ref[0, rows, :] = cur.astype(BF16)


def _q_prompt_kernel(x_ref, g_ref, w_ref, cos_ref, sin_ref, q0_ref, q1_ref, q2_ref, z_ref, *, tiles):
    for t in range(tiles):
        rows = slice(t * TM, (t + 1) * TM)
        _q_project(x_ref[0, rows, :], g_ref, w_ref, cos_ref, sin_ref, (q0_ref, q1_ref, q2_ref),
                   z_ref, rows)


def _q_specs(bsz, seq, w_in, layer, tiles=1):
    rows = tiles * TM
    x_spec = pl.BlockSpec((1, rows, D_MODEL), lambda b, i: (b, i, 0))
    tab_spec = pl.BlockSpec((rows, HEAD_DIM), lambda b, i: (i, 0))
    in_specs = [_const_spec((1, D_MODEL)), _layer_spec(w_in, layer), tab_spec, tab_spec]
    out_specs = [pl.BlockSpec((1, KV_HEADS, rows, QW), lambda b, i: (b, 0, i, 0))] * N_GROUPS + [x_spec]
    out_shape = ([jax.ShapeDtypeStruct((bsz, KV_HEADS, seq, QW), dt) for dt in GROUP_DTYPES]
                 + [jax.ShapeDtypeStruct((bsz, seq, D_MODEL), BF16)])
    return x_spec, in_specs, out_specs, out_shape


def _q_proj_prompt(x, g, w_in, layer, cos_q, sin_q):
    bsz, seq, _ = x.shape
    tiles = 2
    x_spec, in_specs, out_specs, out_shape = _q_specs(bsz, seq, w_in, layer, tiles)
    return pl.pallas_call(
        functools.partial(_q_prompt_kernel, tiles=tiles),
        grid=(bsz, seq // (tiles * TM)),
        in_specs=[x_spec] + in_specs,
        out_specs=out_specs,
        out_shape=out_shape,
        compiler_params=_params(("arbitrary", "arbitrary")),
        name="q_proj_prompt",
    )(x, g, w_in, cos_q, sin_q)


def _block_runs(g, blk):
    if g == 2:
        return [(blk * BAND, BAND)]
    if g == 1:
        r4, c = blk
        return [((4 * mm + r4) * BAND + 32 * c, 32) for mm in range(4)]
    return [(r * BAND + 8 * blk, 8) for r in range(NRES)]


def _block_pos(g):
    a = jnp.arange(BAND)
    if g == 2:
        return a
    if g == 1:
        return 4 * (a % 32) + a // 32
    return NRES * (a % 8) + a // 8


def _band_bias():
    out = []
    for g in range(N_GROUPS):
        pos = _block_pos(g)
        qpos = jnp.tile(pos, Q_PER_KV)[:, None]
        ok_prev = pos[None, :] >= qpos
        ok_cur = pos[None, :] <= qpos
        neg = jnp.full(ok_cur.shape, NEG, F32)
        cur = jnp.where(ok_cur, 0.0, neg)
        out.append(jnp.concatenate([jnp.where(ok_prev, 0.0, neg), cur], axis=1))
        out.append(jnp.concatenate([neg, cur], axis=1))
    return jnp.stack(out)


def _attn_prompt_kernel(q0_ref, q1_ref, q2_ref, kv0_ref, kv1_ref, kv2_ref, bias_ref, o_ref,
                        acc_ref, m_ref, l_ref, biasf_ref, p0_ref, p1_ref, p2_ref):
    q_refs = (q0_ref, q1_ref, q2_ref)
    kv_refs = (kv0_ref, kv1_ref, kv2_ref)
    prev_refs = (p0_ref, p1_ref, p2_ref)
    prev_rows = (8, 32, BAND)
    first = pl.program_id(2) == 0

    @pl.when(first)
    def _():
        for p_ref in prev_refs:
            p_ref[...] = jnp.zeros_like(p_ref)

    for g in range(N_GROUPS):
        biasf_ref[g] = jnp.where(first, bias_ref[2 * g + 1], bias_ref[2 * g])
    ones = jnp.ones((2 * BAND, HEAD_DIM), BF16)
    m_rows = Q_PER_KV * BAND

    def gather(read, runs):
        return jnp.concatenate([read(rs, n) for rs, n in runs], axis=0)

    def heads_state(ref, runs):
        return jnp.concatenate(
            [gather(lambda rs, n, j=j: ref[j, rs:rs + n, :], runs) for j in range(Q_PER_KV)], axis=0)

    def scatter_state(ref, val, runs):
        for j in range(Q_PER_KV):
            off = j * BAND
            for rs, n in runs:
                ref[j, rs:rs + n, :] = val[off:off + n]
                off += n

    for g in range(N_GROUPS):
        q_ref, kv_ref, p_ref = q_refs[g], kv_refs[g], prev_refs[g]
        if g == 2:
            blocks = [(r, None) for r in range(NRES)]
        elif g == 1:
            blocks = [((r4, c), (r4, c - 1) if c else None) for r4 in range(4) for c in range(4)]
        else:
            blocks = [(c, c - 1 if c else None) for c in range(NRES)]
        for blk, prev_blk in blocks:
            runs = _block_runs(g, blk)
            q4 = jnp.concatenate(
                [gather(lambda rs, n, j=j: q_ref[0, 0, rs:rs + n, j * HEAD_DIM:(j + 1) * HEAD_DIM],
                        runs) for j in range(Q_PER_KV)], axis=0).astype(BF16)
            if prev_blk is None:
                pr = prev_rows[g]
                if g == 2:
                    pruns = [(blk * BAND, BAND)]
                elif g == 1:
                    pruns = [((4 * mm + blk[0]) * pr, pr) for mm in range(4)]
                else:
                    pruns = [(r * pr, pr) for r in range(NRES)]
                read_prev = lambda cols: gather(lambda rs, n: p_ref[rs:rs + n, cols], pruns)
                bias = biasf_ref[g]
            else:
                pruns = _block_runs(g, prev_blk)
                read_prev = lambda cols: gather(lambda rs, n: kv_ref[0, 0, rs:rs + n, cols], pruns)
                bias = bias_ref[2 * g]
            read_cur = lambda cols: gather(lambda rs, n: kv_ref[0, 0, rs:rs + n, cols], runs)
            kcols, vcols = slice(0, HEAD_DIM), slice(HEAD_DIM, KW)
            k_cat = jnp.concatenate([read_prev(kcols), read_cur(kcols)], axis=0).astype(BF16)
            v_cat = jnp.concatenate([read_prev(vcols), read_cur(vcols)], axis=0).astype(BF16)

            s = _dot_nt(q4, k_cat) + bias
            m_new = jnp.broadcast_to(jnp.max(s, axis=-1, keepdims=True), (m_rows, HEAD_DIM))
            if g > 0:
                m_old = heads_state(m_ref, runs)
                m_new = jnp.maximum(m_old, m_new)
                alpha = jnp.exp2(m_old - m_new)
            p = jnp.exp2(jnp.concatenate([s[:, :BAND] - m_new, s[:, BAND:] - m_new], axis=1))
            ov = _dot(p.astype(BF16), jnp.concatenate([v_cat, ones], axis=1))
            acc_new, l_new = ov[:, :HEAD_DIM], ov[:, HEAD_DIM:]
            if g > 0:
                acc_new = alpha * heads_state(acc_ref, runs) + acc_new
                l_new = alpha * heads_state(l_ref, runs) + l_new
            if g < N_GROUPS - 1:
                scatter_state(acc_ref, acc_new, runs)
                scatter_state(m_ref, m_new, runs)
                scatter_state(l_ref, l_new, runs)
            else:
                o = (acc_new / l_new).astype(BF16)
                (rs, n), = runs
                for j in range(Q_PER_KV):
                    o_ref[0, rs:rs + n, j * HEAD_DIM:(j + 1) * HEAD_DIM] = o[j * BAND:(j + 1) * BAND]

    for g in range(N_GROUPS):
        pr = prev_rows[g]
        for r in range(NRES):
            prev_refs[g][r * pr:(r + 1) * pr, :] = kv_refs[g][0, 0, (r + 1) * BAND - pr:(r + 1) * BAND, :]


def _attn_prompt(qs, kvs, bias):
    bsz, _, seq, _ = qs[0].shape
    blk = lambda width: pl.BlockSpec((1, 1, TA, width), lambda b, h, i: (b, h, i, 0))
    return pl.pallas_call(
        _attn_prompt_kernel,
        grid=(bsz, KV_HEADS, seq // TA),
        in_specs=[blk(QW)] * N_GROUPS + [blk(KW)] * N_GROUPS + [_const_spec(bias.shape)],
        out_specs=pl.BlockSpec((1, TA, QW), lambda b, h, i: (b, i, h)),
        out_shape=jax.ShapeDtypeStruct((bsz, seq, KV_HEADS * QW), BF16),
        scratch_shapes=[
            pltpu.VMEM((Q_PER_KV, TA, HEAD_DIM), F32),
            pltpu.VMEM((Q_PER_KV, TA, HEAD_DIM), F32),
            pltpu.VMEM((Q_PER_KV, TA, HEAD_DIM), F32),
            pltpu.VMEM((N_GROUPS, Q_PER_KV * BAND, 2 * BAND), F32),
            pltpu.VMEM((NRES * 8, KW), GROUP_DTYPES[0]),
            pltpu.VMEM((NRES * 32, KW), GROUP_DTYPES[1]),
            pltpu.VMEM((NRES * BAND, KW), GROUP_DTYPES[2]),
        ],
        compiler_params=_params(("arbitrary", "arbitrary", "arbitrary")),
        name="attn_prompt",
    )(*qs, *kvs, bias)


def _out_prompt_kernel(o_ref, z_ref, x_ref, wout_ref, gpost_ref, y_ref, slab_ref, *, tiles):
    for t in range(tiles):
        gate = _silu(_load_set(z_ref, t).astype(F32))
        y = (_load_set(o_ref, t).astype(F32) * gate).astype(BF16)
        res = _load_set(x_ref, t) + _rms(_dot(y, wout_ref[...]), gpost_ref[...])
        y_ref[0, t * TM:(t + 1) * TM, :] = _to_token_order(slab_ref, res)


def _out_q_prompt_kernel(o_ref, z_ref, x_ref, wout_ref, gpost_ref, gpre_ref, win_ref, cos_ref, sin_ref,
                         y_ref, q0_ref, q1_ref, q2_ref, zn_ref):
    y = (o_ref[0].astype(F32) * _silu(z_ref[0].astype(F32))).astype(BF16)
    x_new = x_ref[0] + _rms(_dot(y, wout_ref[...]), gpost_ref[...])
    y_ref[0] = x_new
    _q_project(x_new, gpre_ref, win_ref, cos_ref, sin_ref, (q0_ref, q1_ref, q2_ref), zn_ref)


def _out_q_proj_prompt(o, z, x, w_out, g_post, g_pre_next, w_in, layer, cos_q, sin_q):
    bsz, seq, _ = x.shape
    x_spec, q_in_specs, q_out_specs, q_out_shape = _q_specs(bsz, seq, w_in, layer + 1)
    return pl.pallas_call(
        _out_q_prompt_kernel,
        grid=(bsz, seq // TM),
        in_specs=[x_spec] * 3 + [_layer_spec(w_out, layer), _const_spec((1, D_MODEL))]
        + q_in_specs,
        out_specs=[x_spec] + q_out_specs,
        out_shape=[jax.ShapeDtypeStruct((bsz, seq, D_MODEL), F32)] + q_out_shape,
        compiler_params=_params(("arbitrary", "arbitrary")),
        name="out_q_proj_prompt",
    )(o, z, x, w_out, g_post, g_pre_next, w_in, cos_q, sin_q)


def _out_proj_prompt(o, z, x, w_out, layer, g_post):
    bsz, seq, _ = x.shape
    view = lambda a: a.reshape(bsz, seq // TA, NRES, BAND, D_MODEL)
    tiles = 2
    return pl.pallas_call(
        functools.partial(_out_prompt_kernel, tiles=tiles),
        grid=(bsz, seq // (tiles * TM)),
        in_specs=[_set_spec(D_MODEL, tiles)] * 3
        + [_layer_spec(w_out, layer), _const_spec((1, D_MODEL))],
        out_specs=pl.BlockSpec((1, tiles * TM, D_MODEL), lambda b, i: (b, i, 0)),
        out_shape=jax.ShapeDtypeStruct((bsz, seq, D_MODEL), F32),
        scratch_shapes=[pltpu.VMEM((D_MODEL // HEAD_DIM, 2, TM, HEAD_DIM), F32)],
        compiler_params=_params(("arbitrary", "arbitrary")),
        name="out_proj_prompt",
    )(view(o), view(z), view(x), w_out, g_post)


def _proj_sample_kernel(x_ref, g_ref, w_ref, cos_ref, sin_ref, y_ref, *, rope_heads):
    h = _rms(x_ref[...], g_ref[...]).astype(BF16)
    y = _dot(h, w_ref[...])
    cos = cos_ref[...]
    sin = sin_ref[...]
    for hd in range(rope_heads):
        cs = slice(hd * HEAD_DIM, (hd + 1) * HEAD_DIM)
        y_ref[:, cs] = _rope(y[:, cs], cos, sin)
    rest = rope_heads * HEAD_DIM
    y_ref[:, rest:] = y[:, rest:]


def _proj_sample(x, g, w, cos, sin, rope_heads, layer=None):
    n = x.shape[0]
    w_spec = _whole_spec(w) if layer is None else _layer_spec(w, layer)
    return pl.pallas_call(
        functools.partial(_proj_sample_kernel, rope_heads=rope_heads),
        grid=(1,),
        in_specs=[_whole_spec(x), _whole_spec(g), w_spec, _whole_spec(cos), _whole_spec(sin)],
        out_shape=jax.ShapeDtypeStruct((n, w.shape[-1]), F32),
        compiler_params=_params(),
        name=f"proj_sample_{rope_heads}",
    )(x, g, w, cos, sin)


def _attn_sample_row(qz, kvn, window_kv):
    pieces = []
    for kh in range(KV_HEADS):
        scores, s_new, vals, v_new = [], [], [], []
        for g in range(N_GROUPS):
            c = (g * HEADS + kh * Q_PER_KV) * HEAD_DIM
            q4 = jnp.concatenate(
                [qz[:, c + j * HEAD_DIM:c + (j + 1) * HEAD_DIM] for j in range(Q_PER_KV)], axis=0)
            q8 = jnp.concatenate([q4, q4], axis=0)
            k_rows, v_rows = window_kv(g, kh)
            kc = (g * KV_HEADS + kh) * HEAD_DIM
            k_n = kvn[:, kc:kc + HEAD_DIM]
            v_n = kvn[:, KV_HALF + kc:KV_HALF + kc + HEAD_DIM]
            scores.append(_dot_nt(q8.astype(BF16), k_rows))
            s_new.append(jnp.sum(q8 * k_n, axis=-1, keepdims=True))
            vals.append(v_rows)
            v_new.append(v_n)
        m = functools.reduce(jnp.maximum,
                             [jnp.max(s, axis=-1, keepdims=True) for s in scores] + s_new)
        ps = [jnp.exp(s - m) for s in scores]
        pn = [jnp.exp(s - m) for s in s_new]
        l = sum(jnp.sum(p, axis=-1, keepdims=True) for p in ps) + sum(pn)
        o = sum(_dot(ps[g].astype(BF16), vals[g]) + pn[g] * v_new[g] for g in range(N_GROUPS)) / l
        pieces += [o[j:j + 1, :] for j in range(Q_PER_KV)]
    o_row = jnp.concatenate(pieces, axis=-1)
    return o_row * _silu(qz[:, Q_WIDTH:])


def _attn_sample_kernel(qz_ref, kvn_ref, win_ref, y_ref):
    for p in range(SAMPLE_ROWS):
        def window_kv(g, kh, p=p):
            slot = 2 * (g * KV_HEADS + kh)
            return win_ref[p, slot], win_ref[p, slot + 1]

        y_ref[p] = _attn_sample_row(qz_ref[p], kvn_ref[p], window_kv)


def _attn_sample(qz, kv_new, win):
    n = qz.shape[0]
    assert n % SAMPLE_ROWS == 0
    row_spec = lambda a: pl.BlockSpec((SAMPLE_ROWS, 1, a.shape[-1]), lambda b: (b, 0, 0))
    return pl.pallas_call(
        _attn_sample_kernel,
        grid=(n // SAMPLE_ROWS,),
        in_specs=[row_spec(qz), row_spec(kv_new),
                  pl.BlockSpec((SAMPLE_ROWS,) + win.shape[1:], lambda b: (b, 0, 0, 0))],
        out_specs=pl.BlockSpec((SAMPLE_ROWS, 1, D_MODEL), lambda b: (b, 0, 0)),
        out_shape=jax.ShapeDtypeStruct((n, 1, D_MODEL), F32),
        compiler_params=_params(("arbitrary",)),
        name="attn_sample",
    )(qz, kv_new, win)


def _out_sample_kernel(y_ref, x_ref, wout_ref, gpost_ref, o_ref):
    out = _dot(y_ref[...].astype(BF16), wout_ref[...])
    o_ref[...] = x_ref[...] + _rms(out, gpost_ref[...])


def _out_sample(y, x, w_out, layer, g_post):
    return pl.pallas_call(
        _out_sample_kernel,
        grid=(1,),
        in_specs=[_whole_spec(y), _whole_spec(x), _layer_spec(w_out, layer), _whole_spec(g_post)],
        out_shape=jax.ShapeDtypeStruct(x.shape, F32),
        compiler_params=_params(),
        name="out_sample",
    )(y, x, w_out, g_post)


def _inv_freq():
    return ROPE_THETA ** (-jnp.arange(HEAD_DIM // 2, dtype=F32) * 2.0 / HEAD_DIM)


def _full_width(cos, sin):
    return jnp.concatenate([cos, cos], axis=-1), jnp.concatenate([-sin, sin], axis=-1)


def _rope_tables(pos):
    ang = pos.astype(F32)[:, None] * _inv_freq()[None, :]
    return _full_width(jnp.cos(ang), jnp.sin(ang))


def _rope_tables_residue_major(seq):
    f = jnp.tile(_inv_freq(), 2)
    sign = jnp.where(jnp.arange(HEAD_DIM) < HEAD_DIM // 2, -1.0, 1.0).astype(F32)
    base = (jnp.arange(seq // TA)[:, None] * TA + NRES * jnp.arange(BAND)[None, :]).astype(F32)
    a = base[:, None, :, None] * f
    b = jnp.arange(NRES, dtype=F32)[None, :, None, None] * f
    cos_a, sin_a, cos_b, sin_b = jnp.cos(a), jnp.sin(a), jnp.cos(b), jnp.sin(b)
    cos = (cos_a * cos_b - sin_a * sin_b).reshape(seq, HEAD_DIM)
    sin = (sign * (sin_a * cos_b + cos_a * sin_b)).reshape(seq, HEAD_DIM)
    return cos, sin


def kernel(x_prompt, x_sample, state_conv, cache_kv_g0, cache_kv_g1, cache_kv_g2,
           a_norm_pre, a_w_in, a_conv, a_w_out, a_norm_post,
           kv_norm, w_kv, b_norm_pre, b_w_in, b_w_out, b_norm_post):
    bsz, seq, _ = x_prompt.shape
    n_s = x_sample.shape[0]
    n_a = a_w_in.shape[0]
    n_b = b_w_in.shape[0]
    assert x_sample.shape[1] == 1 and seq % TA == 0 and n_a >= 1 and n_b >= 1

    cos_p, sin_p = _rope_tables_residue_major(seq)
    cos_s, sin_s = _rope_tables(PAST_LEN + jnp.arange(1))
    row = lambda v: v.reshape(1, -1)

    a_w_in16 = a_w_in.astype(BF16)
    a_w_out16 = a_w_out.astype(BF16)
    w_kv16 = w_kv.astype(BF16)
    b_w_in16 = b_w_in.astype(BF16)
    b_w_out16 = b_w_out.astype(BF16)

    conv_args = [(row(a_norm_pre[layer]), a_w_in16, a_conv[layer], a_w_out16,
                  row(a_norm_post[layer]), layer) for layer in range(n_a)]

    hs = x_sample.reshape(n_s, D_MODEL)
    conv_s = []
    for layer in range(n_a):
        past0 = state_conv[layer, :, 0]
        past1 = state_conv[layer, :, 1]
        hs, u_s = _conv_layer_sample(hs, past0, past1, *conv_args[layer])
        conv_s.append(jnp.stack([past1, u_s], axis=1))
    kv_s = _proj_sample(hs, row(kv_norm), w_kv16, cos_s, sin_s, N_GROUPS * KV_HEADS)
    nr = 2 * KV_HEADS
    cache_rows = [c.reshape(n_s, c.shape[1] * nr, HEAD_DIM)
                  for c in (cache_kv_g0, cache_kv_g1, cache_kv_g2)]
    new_rows = jnp.stack([kv_s[:, :KV_HALF].reshape(n_s, N_GROUPS, KV_HEADS, HEAD_DIM),
                          kv_s[:, KV_HALF:].reshape(n_s, N_GROUPS, KV_HEADS, HEAD_DIM)],
                         axis=2).reshape(n_s, N_GROUPS, nr, HEAD_DIM)

    hp = x_prompt
    conv_p = []
    for layer in range(n_a):
        hp, st_p, *extra = _conv_layer_prompt(
            hp, *conv_args[layer], residue_major_out=(layer == n_a - 1),
            shift=(cache_rows, new_rows) if layer == 0 else None)
        conv_p.append(st_p)
        if layer == 0:
            *shifted, win = extra
            kv_new_s = [c.reshape(n_s, WINDOWS[g], 2, KV_HEADS, HEAD_DIM)
                        for g, c in enumerate(shifted)]

    kv_s3 = kv_s.reshape(n_s, 1, -1)
    for j in range(n_b):
        qz = _proj_sample(hs, row(b_norm_pre[j]), b_w_in16, cos_s * SCALE, sin_s * SCALE,
                          N_GROUPS * HEADS, layer=j)
        y_s = _attn_sample(qz.reshape(n_s, 1, -1), kv_s3, win)
        hs = _out_sample(y_s.reshape(n_s, D_MODEL), hs, b_w_out16, j, row(b_norm_post[j]))

    tab5 = lambda t: t.reshape(seq // TA, NRES, BAND, HEAD_DIM)
    kv_p, cs = _kv_proj_prompt(hp, row(kv_norm), w_kv16, tab5(cos_p), tab5(sin_p))
    kv_new_p = [c.reshape(bsz, c.shape[1], 2, KV_HEADS, HEAD_DIM) for c in cs]
    hp = hp.reshape(bsz, seq, D_MODEL)
    bias = _band_bias()
    cos_q = cos_p * (SCALE * LOG2E)
    sin_q = sin_p * (SCALE * LOG2E)
    *q, z = _q_proj_prompt(hp, row(b_norm_pre[0]), b_w_in16, 0, cos_q, sin_q)
    for j in range(n_b):
        g_post = row(b_norm_post[j])
        o = _attn_prompt(q, kv_p, bias)
        if j + 1 < n_b:
            hp, *q, z = _out_q_proj_prompt(o, z, hp, b_w_out16, g_post, row(b_norm_pre[j + 1]),
                                           b_w_in16, j, cos_q, sin_q)
        else:
            hp = _out_proj_prompt(o, z, hp, b_w_out16, j, g_post)

    return (hp, hs.reshape(n_s, 1, D_MODEL), jnp.stack(conv_p, axis=0),
            kv_new_p[0], kv_new_p[1], kv_new_p[2],
            jnp.stack(conv_s, axis=0), kv_new_s[0], kv_new_s[1], kv_new_s[2])
```

```python
import functools

import jax
import jax.numpy as jnp
from jax import lax
from jax.experimental import pallas as pl
from jax.experimental.pallas import tpu as pltpu

D_MODEL = 1024
HEAD_DIM = 128
N_GROUPS = 3
WINDOWS = (128, 512, 2048)
DILATIONS = (1, 4, 16)
HEADS = 8
KV_HEADS = 2
Q_PER_KV = HEADS // KV_HEADS
Q_WIDTH = N_GROUPS * HEADS * HEAD_DIM
KV_HALF = N_GROUPS * KV_HEADS * HEAD_DIM
PAST_LEN = 8192
ROPE_THETA = 10000.0
EPS = 1e-6
SCALE = HEAD_DIM ** -0.5
NEG = -1e30
LOG2E = 1.4426950408889634

BAND = 128
NRES = 16
TA = NRES * BAND
TM = 512
SUB = TM // NRES
CH = 256
SAMPLE_ROWS = 8
QW = Q_PER_KV * HEAD_DIM
KW = 2 * HEAD_DIM
VMEM_LIMIT = 54 * 1024 * 1024

F32 = jnp.float32
BF16 = jnp.bfloat16
GROUP_DTYPES = (F32, BF16, BF16)


def _dot(a, b):
    return jnp.dot(a, b, preferred_element_type=F32)


def _dot_nt(a, b):
    return lax.dot_general(a, b, (((1,), (1,)), ((), ())), preferred_element_type=F32)


def _rms(x, g):
    ms = jnp.mean(x * x, axis=-1, keepdims=True)
    return x * lax.rsqrt(ms + EPS) * g


def _silu(z):
    return z * jax.nn.sigmoid(z)


def _rope(x, cos, sin_signed):
    return x * cos + pltpu.roll(x, HEAD_DIM // 2, 1) * sin_signed


def _params(sem=None, flags=None):
    return pltpu.CompilerParams(dimension_semantics=sem, vmem_limit_bytes=VMEM_LIMIT, flags=flags)


def _const_spec(shape):
    zeros = (0,) * len(shape)
    return pl.BlockSpec(shape, lambda *_: zeros, pipeline_mode=pl.Buffered(1))


def _layer_spec(w, layer):
    return pl.BlockSpec((None,) + w.shape[1:], lambda *_: (layer, 0, 0),
                        pipeline_mode=pl.Buffered(1))


def _whole_spec(a):
    zeros = (0,) * a.ndim
    return pl.BlockSpec(a.shape, lambda *_: zeros)


def _split_residues(slab_ref, val):
    q = TM // 4
    slab_ref[0] = val
    for m in range(4):
        slab_ref[1, m * q:(m + 1) * q, :] = slab_ref[0, pl.ds(m, q, stride=4), :]
    return {4 * m2 + m: slab_ref[1, pl.ds(m * q + m2, SUB, stride=4), :]
            for m in range(4) for m2 in range(4)}


def _merge_residues(slab_ref, pieces):
    q = TM // 4
    for m in range(4):
        for m2 in range(4):
            slab_ref[1, pl.ds(m * q + m2, SUB, stride=4), :] = pieces[4 * m2 + m]
    for m in range(4):
        slab_ref[0, pl.ds(m, q, stride=4), :] = slab_ref[1, m * q:(m + 1) * q, :]
    return slab_ref[0]


def _to_residue_major(slab_ref, val):
    ncol = val.shape[1] // HEAD_DIM
    cols = [_split_residues(slab_ref.at[c], val[:, c * HEAD_DIM:(c + 1) * HEAD_DIM])
            for c in range(ncol)]
    return [jnp.concatenate([cols[c][r] for c in range(ncol)], axis=-1) for r in range(NRES)]


def _to_token_order(slab_ref, val):
    ncol = val.shape[1] // HEAD_DIM
    cols = [_merge_residues(slab_ref.at[c],
                            {r: val[r * SUB:(r + 1) * SUB, c * HEAD_DIM:(c + 1) * HEAD_DIM]
                             for r in range(NRES)})
            for c in range(ncol)]
    return jnp.concatenate(cols, axis=-1)


def _set_spec(width, sets=1):
    per = TA // (sets * TM)
    return pl.BlockSpec((1, 1, NRES, sets * SUB, width), lambda b, i: (b, i // per, 0, i % per, 0))


def _load_set(ref, t=0):
    return jnp.concatenate([ref[0, 0, r, t * SUB:(t + 1) * SUB, :] for r in range(NRES)], axis=0)


def _conv_project_chunk(h, win_ref, j):
    return [_dot(h, win_ref[:, k * D_MODEL + j * CH:k * D_MODEL + (j + 1) * CH]) for k in range(4)]


def _conv_mix_chunk(proj, wconv_ref, j, u_prev1, u_prev2):
    cs = slice(j * CH, (j + 1) * CH)
    b_gate, c_gate, u_in, z = proj
    u = c_gate * u_in
    conv = (u_prev2(u, cs) * wconv_ref[0:1, cs] + u_prev1(u, cs) * wconv_ref[1:2, cs]
            + u * wconv_ref[2:3, cs])
    return u, b_gate * conv * _silu(z)


def _conv_chunks(h, win_ref, wconv_ref, wout_ref, u_prev1, u_prev2, keep_u):
    nch = D_MODEL // CH
    acc = jnp.zeros((h.shape[0], D_MODEL), F32)
    proj = _conv_project_chunk(h, win_ref, 0)
    for j in range(nch):
        nxt = _conv_project_chunk(h, win_ref, j + 1) if j + 1 < nch else None
        u, y = _conv_mix_chunk(proj, wconv_ref, j, u_prev1, u_prev2)
        keep_u(u, slice(j * CH, (j + 1) * CH))
        acc = acc + _dot(y.astype(BF16), wout_ref[j * CH:(j + 1) * CH, :])
        proj = nxt
    return acc


def _shift_sample_caches(cache_refs, new_ref, out_refs, win_ref, rows):
    nr = 2 * KV_HEADS
    for p in range(rows):
        for g in range(N_GROUPS):
            c_ref, o_ref = cache_refs[g], out_refs[g]
            keep = (WINDOWS[g] - 1) * nr
            o_ref[p, 0:keep, :] = c_ref[p, nr:nr + keep, :]
            o_ref[p, keep:keep + nr, :] = new_ref[p, g]
            stride = nr * DILATIONS[g]
            for kh in range(KV_HEADS):
                slot = 2 * (g * KV_HEADS + kh)
                win_ref[p, slot] = c_ref[p, pl.ds(kh, BAND, stride=stride), :].astype(BF16)
                win_ref[p, slot + 1] = c_ref[p, pl.ds(KV_HEADS + kh, BAND, stride=stride), :].astype(BF16)


def _conv_prompt_kernel(*refs, residue_major_out, shift_rows, tiles):
    x_ref, gpre_ref, win_ref, wconv_ref, wout_ref, gpost_ref = refs[:6]
    refs = refs[6:]
    if shift_rows:
        cache_refs, new_ref, refs = refs[:3], refs[3], refs[4:]
    (y_ref, st_ref), refs = refs[:2], refs[2:]
    if shift_rows:
        out_cache_refs, window_ref, refs = refs[:3], refs[3], refs[4:]
    tail_ref, refs = refs[0], refs[1:]
    if residue_major_out:
        slab_ref, = refs

    @pl.when(pl.program_id(1) == 0)
    def _():
        tail_ref[...] = jnp.zeros_like(tail_ref)

    row = lax.broadcasted_iota(jnp.int32, (TM, CH), 0)

    def prev1(u, cs):
        return jnp.where(row == 0, tail_ref[1:2, cs], pltpu.roll(u, 1, 0))

    def prev2(u, cs):
        return jnp.where(row == 0, tail_ref[0:1, cs],
                         jnp.where(row == 1, tail_ref[1:2, cs], pltpu.roll(u, 2, 0)))

    def keep_tail(u, cs):
        tail_ref[0:2, cs] = u[TM - 2:TM, :]

    nch = D_MODEL // CH
    xs = [x_ref[0, t * TM:(t + 1) * TM, :] for t in range(tiles)]
    hs = [_rms(x, gpre_ref[...]).astype(BF16) for x in xs]
    units = [(t, j) for t in range(tiles) for j in range(nch)]
    proj = _conv_project_chunk(hs[0], win_ref, 0)
    ys = []
    for n, (t, j) in enumerate(units):
        if n + 1 < len(units):
            tn, jn = units[n + 1]
            nxt = _conv_project_chunk(hs[tn], win_ref, jn)
        u, y = _conv_mix_chunk(proj, wconv_ref, j, prev1, prev2)
        keep_tail(u, slice(j * CH, (j + 1) * CH))
        ys.append(y.astype(BF16))
        proj = nxt
        if j == nch - 1:
            acc = _dot(jnp.concatenate(ys, axis=-1), wout_ref[...])
            ys = []
            out = xs[t] + _rms(acc, gpost_ref[...])
            if residue_major_out:
                for r, piece in enumerate(_to_residue_major(slab_ref, out)):
                    y_ref[0, 0, r, t * SUB:(t + 1) * SUB, :] = piece
            else:
                y_ref[0, t * TM:(t + 1) * TM, :] = out
    st_ref[0] = tail_ref[0:2, :]

    if shift_rows:
        _shift_sample_caches(cache_refs, new_ref, out_cache_refs, window_ref, shift_rows)


def _conv_layer_prompt(x, g_pre, w_in, w_conv, w_out, g_post, layer, residue_major_out, shift=None):
    bsz, seq, _ = x.shape
    tiles = 1 if shift is not None else 2
    rows = tiles * TM
    nt = seq // rows
    n_steps = bsz * nt
    scratch = [pltpu.VMEM((8, D_MODEL), F32)]
    if residue_major_out:
        y_spec = _set_spec(D_MODEL, tiles)
        y_shape = jax.ShapeDtypeStruct((bsz, seq // TA, NRES, BAND, D_MODEL), F32)
        scratch.append(pltpu.VMEM((D_MODEL // HEAD_DIM, 2, TM, HEAD_DIM), F32))
    else:
        y_spec = pl.BlockSpec((1, rows, D_MODEL), lambda b, i: (b, i, 0))
        y_shape = jax.ShapeDtypeStruct((bsz, seq, D_MODEL), F32)
    extra_in, extra_in_specs, extra_out, extra_out_specs, shift_rows = [], [], [], [], 0
    if shift is not None:
        caches, new_rows = shift
        n_s = caches[0].shape[0]
        shift_rows = -(-n_s // n_steps)
        assert n_s % shift_rows == 0
        last = n_s // shift_rows - 1
        blk = lambda a: pl.BlockSpec(
            (shift_rows,) + a.shape[1:],
            lambda b, i: (jnp.minimum(b * nt + i, last),) + (0,) * (a.ndim - 1))
        win_sds = jax.ShapeDtypeStruct((n_s, 2 * N_GROUPS * KV_HEADS, BAND, HEAD_DIM), BF16)
        extra_in = list(caches) + [new_rows]
        extra_in_specs = [blk(a) for a in extra_in]
        extra_out = [jax.ShapeDtypeStruct(c.shape, c.dtype) for c in caches] + [win_sds]
        extra_out_specs = [blk(a) for a in extra_out]
    return pl.pallas_call(
        functools.partial(_conv_prompt_kernel, residue_major_out=residue_major_out,
                          shift_rows=shift_rows, tiles=tiles),
        grid=(bsz, nt),
        in_specs=[
            pl.BlockSpec((1, rows, D_MODEL), lambda b, i: (b, i, 0)),
            _const_spec((1, D_MODEL)),
            _layer_spec(w_in, layer),
            _const_spec((3, D_MODEL)),
            _layer_spec(w_out, layer),
            _const_spec((1, D_MODEL)),
        ] + extra_in_specs,
        out_specs=[y_spec, pl.BlockSpec((1, 2, D_MODEL), lambda b, i: (b, 0, 0))] + extra_out_specs,
        out_shape=[y_shape, jax.ShapeDtypeStruct((bsz, 2, D_MODEL), F32)] + extra_out,
        scratch_shapes=scratch,
        compiler_params=_params(("arbitrary", "arbitrary")),
        name="conv_layer_prompt",
    )(x, g_pre, w_in, w_conv, w_out, g_post, *extra_in)


def _conv_sample_kernel(x_ref, p0_ref, p1_ref, gpre_ref, win_ref, wconv_ref, wout_ref, gpost_ref,
                        y_ref, u_ref):
    x = x_ref[...]
    h = _rms(x, gpre_ref[...]).astype(BF16)
    def keep_u(u, cs):
        u_ref[:, cs] = u

    acc = _conv_chunks(h, win_ref, wconv_ref, wout_ref,
                       lambda u, cs: p1_ref[:, cs], lambda u, cs: p0_ref[:, cs], keep_u)
    y_ref[...] = x + _rms(acc, gpost_ref[...])


def _conv_layer_sample(x, past0, past1, g_pre, w_in, w_conv, w_out, g_post, layer):
    n = x.shape[0]
    return pl.pallas_call(
        _conv_sample_kernel,
        grid=(1,),
        in_specs=[_whole_spec(x), _whole_spec(past0), _whole_spec(past1), _whole_spec(g_pre),
                  _layer_spec(w_in, layer), _whole_spec(w_conv), _layer_spec(w_out, layer),
                  _whole_spec(g_post)],
        out_shape=[jax.ShapeDtypeStruct((n, D_MODEL), F32)] * 2,
        compiler_params=_params(),
        name="conv_layer_sample",
    )(x, past0, past1, g_pre, w_in, w_conv, w_out, g_post)


def _kv_prompt_kernel(x_ref, g_ref, w_ref, cos_ref, sin_ref, kv0_ref, kv1_ref, kv2_ref,
                      c0_ref, c1_ref, c2_ref, slab_ref, *, first_cache_step, tiles):
    kv_refs = (kv0_ref, kv1_ref, kv2_ref)
    fulls = []
    for t in range(tiles):
        sub = slice(t * SUB, (t + 1) * SUB)
        h = _rms(_load_set(x_ref, t), g_ref[...]).astype(BF16)
        cos = jnp.concatenate([cos_ref[0, r, sub, :] for r in range(NRES)], axis=0)
        sin = jnp.concatenate([sin_ref[0, r, sub, :] for r in range(NRES)], axis=0)
        kv = _dot(h, w_ref[...])
        per_group = []
        for g in range(N_GROUPS):
            ks, vs = [], []
            for hh in range(KV_HEADS):
                c = (g * KV_HEADS + hh) * HEAD_DIM
                k = _rope(kv[:, c:c + HEAD_DIM], cos, sin)
                v = kv[:, KV_HALF + c:KV_HALF + c + HEAD_DIM]
                both = jnp.concatenate([k, v], axis=-1).astype(GROUP_DTYPES[g])
                for r in range(NRES):
                    kv_refs[g][0, hh, 0, r, sub, :] = both[r * SUB:(r + 1) * SUB]
                ks.append(k)
                vs.append(v)
            per_group.append(jnp.concatenate(ks + vs, axis=-1))
        fulls.append(per_group)

    @pl.when(pl.program_id(1) >= first_cache_step)
    def _():
        nr = 2 * KV_HEADS
        for t in range(tiles):
            for g, c_ref in enumerate((c0_ref, c1_ref, c2_ref)):
                if WINDOWS[g] <= TM and t < tiles - 1:
                    continue
                keep = min(TM, WINDOWS[g])
                base = t * TM * nr if WINDOWS[g] > TM else 0
                nat = _to_token_order(slab_ref, fulls[t][g])
                for c in range(nr):
                    c_ref[0, pl.ds(base + c, keep, stride=nr), :] = (
                        nat[TM - keep:, c * HEAD_DIM:(c + 1) * HEAD_DIM])


def _kv_proj_prompt(x5, g, w_kv, cos5, sin5):
    bsz, nta = x5.shape[:2]
    seq = nta * TA
    tiles = 2
    rows_step = tiles * TM
    nt = seq // rows_step
    per = TA // rows_step
    nr = 2 * KV_HEADS
    cache_specs, cache_shapes, firsts = [], [], []
    for grp in range(N_GROUPS):
        w = min(WINDOWS[grp], seq)
        rows = min(rows_step, w) if w > TM else w
        first = nt - max(w // rows_step, 1)
        firsts.append(first)
        cache_specs.append(pl.BlockSpec(
            (1, rows * nr, HEAD_DIM), lambda b, i, first=first: (b, jnp.maximum(i - first, 0), 0)))
        cache_shapes.append(jax.ShapeDtypeStruct((bsz, w * nr, HEAD_DIM), F32))
    kv_spec = pl.BlockSpec((1, KV_HEADS, 1, NRES, tiles * SUB, KW),
                           lambda b, i: (b, 0, i // per, 0, i % per, 0))
    kv_shapes = [jax.ShapeDtypeStruct((bsz, KV_HEADS, nta, NRES, BAND, KW), dt)
                 for dt in GROUP_DTYPES]
    tab_spec = pl.BlockSpec((1, NRES, tiles * SUB, HEAD_DIM), lambda b, i: (i // per, 0, i % per, 0))
    outs = pl.pallas_call(
        functools.partial(_kv_prompt_kernel, first_cache_step=min(firsts), tiles=tiles),
        grid=(bsz, nt),
        in_specs=[_set_spec(D_MODEL, tiles), _const_spec((1, D_MODEL)),
                  _const_spec((D_MODEL, 2 * KV_HALF)), tab_spec, tab_spec],
        out_specs=[kv_spec] * N_GROUPS + cache_specs,
        out_shape=kv_shapes + cache_shapes,
        scratch_shapes=[pltpu.VMEM((4, 2, TM, HEAD_DIM), F32)],
        compiler_params=_params(("arbitrary", "arbitrary")),
        name="kv_proj_prompt",
    )(x5, g, w_kv, cos5, sin5)
    kvs = [o.reshape(bsz, KV_HEADS, seq, KW) for o in outs[:N_GROUPS]]
    caches = [c.reshape(bsz, c.shape[1] // nr, 2, KV_HEADS, HEAD_DIM) for c in outs[N_GROUPS:]]
    return kvs, caches


def _q_project(x, g_ref, w_ref, cos_ref, sin_ref, q_refs, z_ref, rows=slice(0, TM)):
    h = _rms(x, g_ref[...]).astype(BF16)
    cos = cos_ref[rows, :]
    sin = sin_ref[rows, :]
    units = [(g, kh) for g in range(N_GROUPS) for kh in range(KV_HEADS)]
    project = lambda u: _dot(h, w_ref[:, (u[0] * KV_HEADS + u[1]) * QW:(u[0] * KV_HEADS + u[1] + 1) * QW])
    cur = project(units[0])
    for n, (g, kh) in enumerate(units):
        nxt = project(units[n + 1]) if n + 1 < len(units) else _dot(h, w_ref[:, Q_WIDTH:])
        heads = [_rope(cur[:, j * HEAD_DIM:(j + 1) * HEAD_DIM], cos, sin) for j in range(Q_PER_KV)]
        q_refs[g][0, kh, rows, :] = jnp.concatenate(heads, axis=-1).astype(GROUP_DTYPES[g])
        cur = nxt
    z_ref[0, rows, :] = cur.astype(BF16)


def _q_prompt_kernel(x_ref, g_ref, w_ref, cos_ref, sin_ref, q0_ref, q1_ref, q2_ref, z_ref, *, tiles):
    for t in range(tiles):
        rows = slice(t * TM, (t + 1) * TM)
        _q_project(x_ref[0, rows, :], g_ref, w_ref, cos_ref, sin_ref, (q0_ref, q1_ref, q2_ref),
                   z_ref, rows)


def _q_specs(bsz, seq, w_in, layer, tiles=1):
    rows = tiles * TM
    x_spec = pl.BlockSpec((1, rows, D_MODEL), lambda b, i: (b, i, 0))
    tab_spec = pl.BlockSpec((rows, HEAD_DIM), lambda b, i: (i, 0))
    in_specs = [_const_spec((1, D_MODEL)), _layer_spec(w_in, layer), tab_spec, tab_spec]
    out_specs = [pl.BlockSpec((1, KV_HEADS, rows, QW), lambda b, i: (b, 0, i, 0))] * N_GROUPS + [x_spec]
    out_shape = ([jax.ShapeDtypeStruct((bsz, KV_HEADS, seq, QW), dt) for dt in GROUP_DTYPES]
                 + [jax.ShapeDtypeStruct((bsz, seq, D_MODEL), BF16)])
    return x_spec, in_specs, out_specs, out_shape


def _q_proj_prompt(x, g, w_in, layer, cos_q, sin_q):
    bsz, seq, _ = x.shape
    tiles = 2
    x_spec, in_specs, out_specs, out_shape = _q_specs(bsz, seq, w_in, layer, tiles)
    return pl.pallas_call(
        functools.partial(_q_prompt_kernel, tiles=tiles),
        grid=(bsz, seq // (tiles * TM)),
        in_specs=[x_spec] + in_specs,
        out_specs=out_specs,
        out_shape=out_shape,
        compiler_params=_params(("arbitrary", "arbitrary")),
        name="q_proj_prompt",
    )(x, g, w_in, cos_q, sin_q)


def _block_runs(g, blk):
    if g == 2:
        return [(blk * BAND, BAND)]
    if g == 1:
        r4, c = blk
        return [((4 * mm + r4) * BAND + 32 * c, 32) for mm in range(4)]
    return [(r * BAND + 8 * blk, 8) for r in range(NRES)]


def _block_pos(g):
    a = jnp.arange(BAND)
    if g == 2:
        return a
    if g == 1:
        return 4 * (a % 32) + a // 32
    return NRES * (a % 8) + a // 8


def _band_bias():
    out = []
    for g in range(N_GROUPS):
        pos = _block_pos(g)
        qpos = jnp.tile(pos, Q_PER_KV)[:, None]
        ok_prev = pos[None, :] >= qpos
        ok_cur = pos[None, :] <= qpos
        neg = jnp.full(ok_cur.shape, NEG, F32)
        cur = jnp.where(ok_cur, 0.0, neg)
        out.append(jnp.concatenate([jnp.where(ok_prev, 0.0, neg), cur], axis=1))
        out.append(jnp.concatenate([neg, cur], axis=1))
    return jnp.stack(out)


def _attn_prompt_kernel(q0_ref, q1_ref, q2_ref, kv0_ref, kv1_ref, kv2_ref, bias_ref, o_ref,
                        acc_ref, m_ref, l_ref, biasf_ref, p0_ref, p1_ref, p2_ref):
    q_refs = (q0_ref, q1_ref, q2_ref)
    kv_refs = (kv0_ref, kv1_ref, kv2_ref)
    prev_refs = (p0_ref, p1_ref, p2_ref)
    prev_rows = (8, 32, BAND)
    first = pl.program_id(2) == 0

    @pl.when(first)
    def _():
        for p_ref in prev_refs:
            p_ref[...] = jnp.zeros_like(p_ref)

    for g in range(N_GROUPS):
        biasf_ref[g] = jnp.where(first, bias_ref[2 * g + 1], bias_ref[2 * g])
    ones = jnp.ones((2 * BAND, HEAD_DIM), BF16)
    m_rows = Q_PER_KV * BAND

    def gather(read, runs):
        return jnp.concatenate([read(rs, n) for rs, n in runs], axis=0)

    def heads_state(ref, runs):
        return jnp.concatenate(
            [gather(lambda rs, n, j=j: ref[j, rs:rs + n, :], runs) for j in range(Q_PER_KV)], axis=0)

    def scatter_state(ref, val, runs):
        for j in range(Q_PER_KV):
            off = j * BAND
            for rs, n in runs:
                ref[j, rs:rs + n, :] = val[off:off + n]
                off += n

    for g in range(N_GROUPS):
        q_ref, kv_ref, p_ref = q_refs[g], kv_refs[g], prev_refs[g]
        if g == 2:
            blocks = [(r, None) for r in range(NRES)]
        elif g == 1:
            blocks = [((r4, c), (r4, c - 1) if c else None) for r4 in range(4) for c in range(4)]
        else:
            blocks = [(c, c - 1 if c else None) for c in range(NRES)]
        for blk, prev_blk in blocks:
            runs = _block_runs(g, blk)
            q4 = jnp.concatenate(
                [gather(lambda rs, n, j=j: q_ref[0, 0, rs:rs + n, j * HEAD_DIM:(j + 1) * HEAD_DIM],
                        runs) for j in range(Q_PER_KV)], axis=0).astype(BF16)
            if prev_blk is None:
                pr = prev_rows[g]
                if g == 2:
                    pruns = [(blk * BAND, BAND)]
                elif g == 1:
                    pruns = [((4 * mm + blk[0]) * pr, pr) for mm in range(4)]
                else:
                    pruns = [(r * pr, pr) for r in range(NRES)]
                read_prev = lambda cols: gather(lambda rs, n: p_ref[rs:rs + n, cols], pruns)
                bias = biasf_ref[g]
            else:
                pruns = _block_runs(g, prev_blk)
                read_prev = lambda cols: gather(lambda rs, n: kv_ref[0, 0, rs:rs + n, cols], pruns)
                bias = bias_ref[2 * g]
            read_cur = lambda cols: gather(lambda rs, n: kv_ref[0, 0, rs:rs + n, cols], runs)
            kcols, vcols = slice(0, HEAD_DIM), slice(HEAD_DIM, KW)
            k_cat = jnp.concatenate([read_prev(kcols), read_cur(kcols)], axis=0).astype(BF16)
            v_cat = jnp.concatenate([read_prev(vcols), read_cur(vcols)], axis=0).astype(BF16)

            s = _dot_nt(q4, k_cat) + bias
            m_new = jnp.broadcast_to(jnp.max(s, axis=-1, keepdims=True), (m_rows, HEAD_DIM))
            if g > 0:
                m_old = heads_state(m_ref, runs)
                m_new = jnp.maximum(m_old, m_new)
                alpha = jnp.exp2(m_old - m_new)
            p = jnp.exp2(jnp.concatenate([s[:, :BAND] - m_new, s[:, BAND:] - m_new], axis=1))
            ov = _dot(p.astype(BF16), jnp.concatenate([v_cat, ones], axis=1))
            acc_new, l_new = ov[:, :HEAD_DIM], ov[:, HEAD_DIM:]
            if g > 0:
                acc_new = alpha * heads_state(acc_ref, runs) + acc_new
                l_new = alpha * heads_state(l_ref, runs) + l_new
            if g < N_GROUPS - 1:
                scatter_state(acc_ref, acc_new, runs)
                scatter_state(m_ref, m_new, runs)
                scatter_state(l_ref, l_new, runs)
            else:
                o = (acc_new / l_new).astype(BF16)
                (rs, n), = runs
                for j in range(Q_PER_KV):
                    o_ref[0, rs:rs + n, j * HEAD_DIM:(j + 1) * HEAD_DIM] = o[j * BAND:(j + 1) * BAND]

    for g in range(N_GROUPS):
        pr = prev_rows[g]
        for r in range(NRES):
            prev_refs[g][r * pr:(r + 1) * pr, :] = kv_refs[g][0, 0, (r + 1) * BAND - pr:(r + 1) * BAND, :]


def _attn_prompt(qs, kvs, bias):
    bsz, _, seq, _ = qs[0].shape
    blk = lambda width: pl.BlockSpec((1, 1, TA, width), lambda b, h, i: (b, h, i, 0))
    return pl.pallas_call(
        _attn_prompt_kernel,
        grid=(bsz, KV_HEADS, seq // TA),
        in_specs=[blk(QW)] * N_GROUPS + [blk(KW)] * N_GROUPS + [_const_spec(bias.shape)],
        out_specs=pl.BlockSpec((1, TA, QW), lambda b, h, i: (b, i, h)),
        out_shape=jax.ShapeDtypeStruct((bsz, seq, KV_HEADS * QW), BF16),
        scratch_shapes=[
            pltpu.VMEM((Q_PER_KV, TA, HEAD_DIM), F32),
            pltpu.VMEM((Q_PER_KV, TA, HEAD_DIM), F32),
            pltpu.VMEM((Q_PER_KV, TA, HEAD_DIM), F32),
            pltpu.VMEM((N_GROUPS, Q_PER_KV * BAND, 2 * BAND), F32),
            pltpu.VMEM((NRES * 8, KW), GROUP_DTYPES[0]),
            pltpu.VMEM((NRES * 32, KW), GROUP_DTYPES[1]),
            pltpu.VMEM((NRES * BAND, KW), GROUP_DTYPES[2]),
        ],
        compiler_params=_params(("arbitrary", "arbitrary", "arbitrary")),
        name="attn_prompt",
    )(*qs, *kvs, bias)


def _out_prompt_kernel(o_ref, z_ref, x_ref, wout_ref, gpost_ref, y_ref, slab_ref, *, tiles):
    for t in range(tiles):
        gate = _silu(_load_set(z_ref, t).astype(F32))
        y = (_load_set(o_ref, t).astype(F32) * gate).astype(BF16)
        res = _load_set(x_ref, t) + _rms(_dot(y, wout_ref[...]), gpost_ref[...])
        y_ref[0, t * TM:(t + 1) * TM, :] = _to_token_order(slab_ref, res)


def _out_q_prompt_kernel(o_ref, z_ref, x_ref, wout_ref, gpost_ref, gpre_ref, win_ref, cos_ref, sin_ref,
                         y_ref, q0_ref, q1_ref, q2_ref, zn_ref):
    half = TM // 2
    outs = []
    for hf in range(2):
        rows = slice(hf * half, (hf + 1) * half)
        y = (o_ref[0, rows, :].astype(F32) * _silu(z_ref[0, rows, :].astype(F32))).astype(BF16)
        outs.append(_dot(y, wout_ref[...]))
    for hf in range(2):
        rows = slice(hf * half, (hf + 1) * half)
        x_new = x_ref[0, rows, :] + _rms(outs[hf], gpost_ref[...])
        y_ref[0, rows, :] = x_new
        _q_project(x_new, gpre_ref, win_ref, cos_ref, sin_ref, (q0_ref, q1_ref, q2_ref), zn_ref, rows)


def _out_q_proj_prompt(o, z, x, w_out, g_post, g_pre_next, w_in, layer, cos_q, sin_q):
    bsz, seq, _ = x.shape
    x_spec, q_in_specs, q_out_specs, q_out_shape = _q_specs(bsz, seq, w_in, layer + 1)
    return pl.pallas_call(
        _out_q_prompt_kernel,
        grid=(bsz, seq // TM),
        in_specs=[x_spec] * 3 + [_layer_spec(w_out, layer), _const_spec((1, D_MODEL))]
        + q_in_specs,
        out_specs=[x_spec] + q_out_specs,
        out_shape=[jax.ShapeDtypeStruct((bsz, seq, D_MODEL), F32)] + q_out_shape,
        compiler_params=_params(("arbitrary", "arbitrary")),
        name="out_q_proj_prompt",
    )(o, z, x, w_out, g_post, g_pre_next, w_in, cos_q, sin_q)


def _out_proj_prompt(o, z, x, w_out, layer, g_post):
    bsz, seq, _ = x.shape
    view = lambda a: a.reshape(bsz, seq // TA, NRES, BAND, D_MODEL)
    tiles = 2
    return pl.pallas_call(
        functools.partial(_out_prompt_kernel, tiles=tiles),
        grid=(bsz, seq // (tiles * TM)),
        in_specs=[_set_spec(D_MODEL, tiles)] * 3
        + [_layer_spec(w_out, layer), _const_spec((1, D_MODEL))],
        out_specs=pl.BlockSpec((1, tiles * TM, D_MODEL), lambda b, i: (b, i, 0)),
        out_shape=jax.ShapeDtypeStruct((bsz, seq, D_MODEL), F32),
        scratch_shapes=[pltpu.VMEM((D_MODEL // HEAD_DIM, 2, TM, HEAD_DIM), F32)],
        compiler_params=_params(("arbitrary", "arbitrary")),
        name="out_proj_prompt",
    )(view(o), view(z), view(x), w_out, g_post)


def _proj_sample_kernel(x_ref, g_ref, w_ref, cos_ref, sin_ref, y_ref, *, rope_heads):
    h = _rms(x_ref[...], g_ref[...]).astype(BF16)
    y = _dot(h, w_ref[...])
    cos = cos_ref[...]
    sin = sin_ref[...]
    for hd in range(rope_heads):
        cs = slice(hd * HEAD_DIM, (hd + 1) * HEAD_DIM)
        y_ref[:, cs] = _rope(y[:, cs], cos, sin)
    rest = rope_heads * HEAD_DIM
    y_ref[:, rest:] = y[:, rest:]


def _proj_sample(x, g, w, cos, sin, rope_heads, layer=None):
    n = x.shape[0]
    w_spec = _whole_spec(w) if layer is None else _layer_spec(w, layer)
    return pl.pallas_call(
        functools.partial(_proj_sample_kernel, rope_heads=rope_heads),
        grid=(1,),
        in_specs=[_whole_spec(x), _whole_spec(g), w_spec, _whole_spec(cos), _whole_spec(sin)],
        out_shape=jax.ShapeDtypeStruct((n, w.shape[-1]), F32),
        compiler_params=_params(),
        name=f"proj_sample_{rope_heads}",
    )(x, g, w, cos, sin)


def _attn_sample_row(qz, kvn, window_kv):
    pieces = []
    for kh in range(KV_HEADS):
        scores, s_new, vals, v_new = [], [], [], []
        for g in range(N_GROUPS):
            c = (g * HEADS + kh * Q_PER_KV) * HEAD_DIM
            q4 = jnp.concatenate(
                [qz[:, c + j * HEAD_DIM:c + (j + 1) * HEAD_DIM] for j in range(Q_PER_KV)], axis=0)
            q8 = jnp.concatenate([q4, q4], axis=0)
            k_rows, v_rows = window_kv(g, kh)
            kc = (g * KV_HEADS + kh) * HEAD_DIM
            k_n = kvn[:, kc:kc + HEAD_DIM]
            v_n = kvn[:, KV_HALF + kc:KV_HALF + kc + HEAD_DIM]
            scores.append(_dot_nt(q8.astype(BF16), k_rows))
            s_new.append(jnp.sum(q8 * k_n, axis=-1, keepdims=True))
            vals.append(v_rows)
            v_new.append(v_n)
        m = functools.reduce(jnp.maximum,
                             [jnp.max(s, axis=-1, keepdims=True) for s in scores] + s_new)
        ps = [jnp.exp(s - m) for s in scores]
        pn = [jnp.exp(s - m) for s in s_new]
        l = sum(jnp.sum(p, axis=-1, keepdims=True) for p in ps) + sum(pn)
        o = sum(_dot(ps[g].astype(BF16), vals[g]) + pn[g] * v_new[g] for g in range(N_GROUPS)) / l
        pieces += [o[j:j + 1, :] for j in range(Q_PER_KV)]
    o_row = jnp.concatenate(pieces, axis=-1)
    return o_row * _silu(qz[:, Q_WIDTH:])


def _attn_sample_kernel(qz_ref, kvn_ref, win_ref, y_ref):
    for p in range(SAMPLE_ROWS):
        def window_kv(g, kh, p=p):
            slot = 2 * (g * KV_HEADS + kh)
            return win_ref[p, slot], win_ref[p, slot + 1]

        y_ref[p] = _attn_sample_row(qz_ref[p], kvn_ref[p], window_kv)


def _attn_sample(qz, kv_new, win):
    n = qz.shape[0]
    assert n % SAMPLE_ROWS == 0
    row_spec = lambda a: pl.BlockSpec((SAMPLE_ROWS, 1, a.shape[-1]), lambda b: (b, 0, 0))
    return pl.pallas_call(
        _attn_sample_kernel,
        grid=(n // SAMPLE_ROWS,),
        in_specs=[row_spec(qz), row_spec(kv_new),
                  pl.BlockSpec((SAMPLE_ROWS,) + win.shape[1:], lambda b: (b, 0, 0, 0))],
        out_specs=pl.BlockSpec((SAMPLE_ROWS, 1, D_MODEL), lambda b: (b, 0, 0)),
        out_shape=jax.ShapeDtypeStruct((n, 1, D_MODEL), F32),
        compiler_params=_params(("arbitrary",)),
        name="attn_sample",
    )(qz, kv_new, win)


def _out_sample_kernel(y_ref, x_ref, wout_ref, gpost_ref, o_ref):
    out = _dot(y_ref[...].astype(BF16), wout_ref[...])
    o_ref[...] = x_ref[...] + _rms(out, gpost_ref[...])


def _out_sample(y, x, w_out, layer, g_post):
    return pl.pallas_call(
        _out_sample_kernel,
        grid=(1,),
        in_specs=[_whole_spec(y), _whole_spec(x), _layer_spec(w_out, layer), _whole_spec(g_post)],
        out_shape=jax.ShapeDtypeStruct(x.shape, F32),
        compiler_params=_params(),
        name="out_sample",
    )(y, x, w_out, g_post)


def _inv_freq():
    return ROPE_THETA ** (-jnp.arange(HEAD_DIM // 2, dtype=F32) * 2.0 / HEAD_DIM)


def _full_width(cos, sin):
    return jnp.concatenate([cos, cos], axis=-1), jnp.concatenate([-sin, sin], axis=-1)


def _rope_tables(pos):
    ang = pos.astype(F32)[:, None] * _inv_freq()[None, :]
    return _full_width(jnp.cos(ang), jnp.sin(ang))


def _rope_tables_residue_major(seq):
    f = jnp.tile(_inv_freq(), 2)
    sign = jnp.where(jnp.arange(HEAD_DIM) < HEAD_DIM // 2, -1.0, 1.0).astype(F32)
    base = (jnp.arange(seq // TA)[:, None] * TA + NRES * jnp.arange(BAND)[None, :]).astype(F32)
    a = base[:, None, :, None] * f
    b = jnp.arange(NRES, dtype=F32)[None, :, None, None] * f
    cos_a, sin_a, cos_b, sin_b = jnp.cos(a), jnp.sin(a), jnp.cos(b), jnp.sin(b)
    cos = (cos_a * cos_b - sin_a * sin_b).reshape(seq, HEAD_DIM)
    sin = (sign * (sin_a * cos_b + cos_a * sin_b)).reshape(seq, HEAD_DIM)
    return cos, sin


def kernel(x_prompt, x_sample, state_conv, cache_kv_g0, cache_kv_g1, cache_kv_g2,
           a_norm_pre, a_w_in, a_conv, a_w_out, a_norm_post,
           kv_norm, w_kv, b_norm_pre, b_w_in, b_w_out, b_norm_post):
    bsz, seq, _ = x_prompt.shape
    n_s = x_sample.shape[0]
    n_a = a_w_in.shape[0]
    n_b = b_w_in.shape[0]
    assert x_sample.shape[1] == 1 and seq % TA == 0 and n_a >= 1 and n_b >= 1

    cos_p, sin_p = _rope_tables_residue_major(seq)
    cos_s, sin_s = _rope_tables(PAST_LEN + jnp.arange(1))
    row = lambda v: v.reshape(1, -1)

    a_w_in16 = a_w_in.astype(BF16)
    a_w_out16 = a_w_out.astype(BF16)
    w_kv16 = w_kv.astype(BF16)
    b_w_in16 = b_w_in.astype(BF16)
    b_w_out16 = b_w_out.astype(BF16)

    conv_args = [(row(a_norm_pre[layer]), a_w_in16, a_conv[layer], a_w_out16,
                  row(a_norm_post[layer]), layer) for layer in range(n_a)]

    hs = x_sample.reshape(n_s, D_MODEL)
    conv_s = []
    for layer in range(n_a):
        past0 = state_conv[layer, :, 0]
        past1 = state_conv[layer, :, 1]
        hs, u_s = _conv_layer_sample(hs, past0, past1, *conv_args[layer])
        conv_s.append(jnp.stack([past1, u_s], axis=1))
    kv_s = _proj_sample(hs, row(kv_norm), w_kv16, cos_s, sin_s, N_GROUPS * KV_HEADS)
    nr = 2 * KV_HEADS
    cache_rows = [c.reshape(n_s, c.shape[1] * nr, HEAD_DIM)
                  for c in (cache_kv_g0, cache_kv_g1, cache_kv_g2)]
    new_rows = jnp.stack([kv_s[:, :KV_HALF].reshape(n_s, N_GROUPS, KV_HEADS, HEAD_DIM),
                          kv_s[:, KV_HALF:].reshape(n_s, N_GROUPS, KV_HEADS, HEAD_DIM)],
                         axis=2).reshape(n_s, N_GROUPS, nr, HEAD_DIM)

    hp = x_prompt
    conv_p = []
    for layer in range(n_a):
        hp, st_p, *extra = _conv_layer_prompt(
            hp, *conv_args[layer], residue_major_out=(layer == n_a - 1),
            shift=(cache_rows, new_rows) if layer == 0 else None)
        conv_p.append(st_p)
        if layer == 0:
            *shifted, win = extra
            kv_new_s = [c.reshape(n_s, WINDOWS[g], 2, KV_HEADS, HEAD_DIM)
                        for g, c in enumerate(shifted)]

    kv_s3 = kv_s.reshape(n_s, 1, -1)
    for j in range(n_b):
        qz = _proj_sample(hs, row(b_norm_pre[j]), b_w_in16, cos_s * SCALE, sin_s * SCALE,
                          N_GROUPS * HEADS, layer=j)
        y_s = _attn_sample(qz.reshape(n_s, 1, -1), kv_s3, win)
        hs = _out_sample(y_s.reshape(n_s, D_MODEL), hs, b_w_out16, j, row(b_norm_post[j]))

    tab5 = lambda t: t.reshape(seq // TA, NRES, BAND, HEAD_DIM)
    kv_p, kv_new_p = _kv_proj_prompt(hp, row(kv_norm), w_kv16, tab5(cos_p), tab5(sin_p))
    hp = hp.reshape(bsz, seq, D_MODEL)
    bias = _band_bias()
    cos_q = cos_p * (SCALE * LOG2E)
    sin_q = sin_p * (SCALE * LOG2E)
    *q, z = _q_proj_prompt(hp, row(b_norm_pre[0]), b_w_in16, 0, cos_q, sin_q)
    for j in range(n_b):
        g_post = row(b_norm_post[j])
        o = _attn_prompt(q, kv_p, bias)
        if j + 1 < n_b:
            hp, *q, z = _out_q_proj_prompt(o, z, hp, b_w_out16, g_post, row(b_norm_pre[j + 1]),
                                           b_w_in16, j, cos_q, sin_q)
        else:
            hp = _out_proj_prompt(o, z, hp, b_w_out16, j, g_post)

    return (hp, hs.reshape(n_s, 1, D_MODEL), jnp.stack(conv_p, axis=0),
            kv_new_p[0], kv_new_p[1], kv_new_p[2],
            jnp.stack(conv_s, axis=0), kv_new_s[0], kv_new_s[1], kv_new_s[2])
```

```python
import functools

import jax
import jax.numpy as jnp
from jax import lax
from jax.experimental import pallas as pl
from jax.experimental.pallas import tpu as pltpu

D_MODEL = 1024
HEAD_DIM = 128
N_GROUPS = 3
WINDOWS = (128, 512, 2048)
DILATIONS = (1, 4, 16)
HEADS = 8
KV_HEADS = 2
Q_PER_KV = HEADS // KV_HEADS
Q_WIDTH = N_GROUPS * HEADS * HEAD_DIM
KV_HALF = N_GROUPS * KV_HEADS * HEAD_DIM
PAST_LEN = 8192
ROPE_THETA = 10000.0
EPS = 1e-6
SCALE = HEAD_DIM ** -0.5
NEG = -1e30
LOG2E = 1.4426950408889634

BAND = 128
NRES = 16
TA = NRES * BAND
TM = 512
SUB = TM // NRES
CH = 256
SAMPLE_ROWS = 16
QW = Q_PER_KV * HEAD_DIM
KW = 2 * HEAD_DIM
VMEM_LIMIT = 54 * 1024 * 1024

F32 = jnp.float32
BF16 = jnp.bfloat16
GROUP_DTYPES = (F32, BF16, BF16)


def _dot(a, b):
    return jnp.dot(a, b, preferred_element_type=F32)


def _dot_nt(a, b):
    return lax.dot_general(a, b, (((1,), (1,)), ((), ())), preferred_element_type=F32)


def _rms(x, g):
    ms = jnp.mean(x * x, axis=-1, keepdims=True)
    return x * lax.rsqrt(ms + EPS) * g


def _silu(z):
    return z * jax.nn.sigmoid(z)


def _rope(x, cos, sin_signed):
    return x * cos + pltpu.roll(x, HEAD_DIM // 2, 1) * sin_signed


def _params(sem=None, flags=None):
    return pltpu.CompilerParams(dimension_semantics=sem, vmem_limit_bytes=VMEM_LIMIT, flags=flags)


def _const_spec(shape):
    zeros = (0,) * len(shape)
    return pl.BlockSpec(shape, lambda *_: zeros, pipeline_mode=pl.Buffered(1))


def _layer_spec(w, layer):
    if w.ndim == 2:
        return _const_spec(w.shape)
    return pl.BlockSpec((None,) + w.shape[1:], lambda *_: (layer, 0, 0),
                        pipeline_mode=pl.Buffered(1))


def _whole_spec(a):
    zeros = (0,) * a.ndim
    return pl.BlockSpec(a.shape, lambda *_: zeros)


def _split_residues(slab_ref, val):
    q = TM // 4
    slab_ref[0] = val
    for m in range(4):
        slab_ref[1, m * q:(m + 1) * q, :] = slab_ref[0, pl.ds(m, q, stride=4), :]
    return {4 * m2 + m: slab_ref[1, pl.ds(m * q + m2, SUB, stride=4), :]
            for m in range(4) for m2 in range(4)}


def _merge_residues(slab_ref, pieces):
    q = TM // 4
    for m in range(4):
        for m2 in range(4):
            slab_ref[1, pl.ds(m * q + m2, SUB, stride=4), :] = pieces[4 * m2 + m]
    for m in range(4):
        slab_ref[0, pl.ds(m, q, stride=4), :] = slab_ref[1, m * q:(m + 1) * q, :]
    return slab_ref[0]


def _to_residue_major(slab_ref, val):
    ncol = val.shape[1] // HEAD_DIM
    cols = [_split_residues(slab_ref.at[c], val[:, c * HEAD_DIM:(c + 1) * HEAD_DIM])
            for c in range(ncol)]
    return [jnp.concatenate([cols[c][r] for c in range(ncol)], axis=-1) for r in range(NRES)]


def _to_token_order(slab_ref, val):
    ncol = val.shape[1] // HEAD_DIM
    cols = [_merge_residues(slab_ref.at[c],
                            {r: val[r * SUB:(r + 1) * SUB, c * HEAD_DIM:(c + 1) * HEAD_DIM]
                             for r in range(NRES)})
            for c in range(ncol)]
    return jnp.concatenate(cols, axis=-1)


def _set_spec(width, sets=1):
    per = TA // (sets * TM)
    return pl.BlockSpec((1, 1, NRES, sets * SUB, width), lambda b, i: (b, i // per, 0, i % per, 0))


def _load_set(ref, t=0):
    return jnp.concatenate([ref[0, 0, r, t * SUB:(t + 1) * SUB, :] for r in range(NRES)], axis=0)


def _conv_project_chunk(h, win_ref, j):
    return [_dot(h, win_ref[:, k * D_MODEL + j * CH:k * D_MODEL + (j + 1) * CH]) for k in range(4)]


def _conv_mix_chunk(proj, wconv_ref, j, u_prev1, u_prev2):
    cs = slice(j * CH, (j + 1) * CH)
    b_gate, c_gate, u_in, z = proj
    u = c_gate * u_in
    conv = (u_prev2(u, cs) * wconv_ref[0:1, cs] + u_prev1(u, cs) * wconv_ref[1:2, cs]
            + u * wconv_ref[2:3, cs])
    return u, b_gate * conv * _silu(z)


def _conv_chunks(h, win_ref, wconv_ref, wout_ref, u_prev1, u_prev2, keep_u):
    nch = D_MODEL // CH
    acc = jnp.zeros((h.shape[0], D_MODEL), F32)
    proj = _conv_project_chunk(h, win_ref, 0)
    for j in range(nch):
        nxt = _conv_project_chunk(h, win_ref, j + 1) if j + 1 < nch else None
        u, y = _conv_mix_chunk(proj, wconv_ref, j, u_prev1, u_prev2)
        keep_u(u, slice(j * CH, (j + 1) * CH))
        acc = acc + _dot(y.astype(BF16), wout_ref[j * CH:(j + 1) * CH, :])
        proj = nxt
    return acc


def _shift_sample_caches(cache_refs, new_ref, out_refs, win_ref, rows):
    nr = 2 * KV_HEADS
    for p in range(rows):
        for g in range(N_GROUPS):
            c_ref, o_ref = cache_refs[g], out_refs[g]
            keep = (WINDOWS[g] - 1) * nr
            o_ref[p, 0:keep, :] = c_ref[p, nr:nr + keep, :]
            o_ref[p, keep:keep + nr, :] = new_ref[p, g]
            stride = nr * DILATIONS[g]
            for kh in range(KV_HEADS):
                slot = 2 * (g * KV_HEADS + kh)
                win_ref[p, slot] = c_ref[p, pl.ds(kh, BAND, stride=stride), :].astype(BF16)
                win_ref[p, slot + 1] = c_ref[p, pl.ds(KV_HEADS + kh, BAND, stride=stride), :].astype(BF16)


def _conv_prompt_kernel(*refs, residue_major_out, shift_rows, tiles):
    x_ref, gpre_ref, win_ref, wconv_ref, wout_ref, gpost_ref = refs[:6]
    refs = refs[6:]
    if shift_rows:
        cache_refs, new_ref, refs = refs[:3], refs[3], refs[4:]
    (y_ref, st_ref), refs = refs[:2], refs[2:]
    if shift_rows:
        out_cache_refs, window_ref, refs = refs[:3], refs[3], refs[4:]
    tail_ref, refs = refs[0], refs[1:]
    if residue_major_out:
        slab_ref, = refs

    @pl.when(pl.program_id(1) == 0)
    def _():
        tail_ref[...] = jnp.zeros_like(tail_ref)

    row = lax.broadcasted_iota(jnp.int32, (TM, CH), 0)

    def prev1(u, cs):
        return jnp.where(row == 0, tail_ref[1:2, cs], pltpu.roll(u, 1, 0))

    def prev2(u, cs):
        return jnp.where(row == 0, tail_ref[0:1, cs],
                         jnp.where(row == 1, tail_ref[1:2, cs], pltpu.roll(u, 2, 0)))

    def keep_tail(u, cs):
        tail_ref[0:2, cs] = u[TM - 2:TM, :]

    nch = D_MODEL // CH
    xs = [x_ref[0, t * TM:(t + 1) * TM, :] for t in range(tiles)]
    hs = [_rms(x, gpre_ref[...]).astype(BF16) for x in xs]
    units = [(t, j) for t in range(tiles) for j in range(nch)]
    proj = _conv_project_chunk(hs[0], win_ref, 0)
    ys = []
    for n, (t, j) in enumerate(units):
        if n + 1 < len(units):
            tn, jn = units[n + 1]
            nxt = _conv_project_chunk(hs[tn], win_ref, jn)
        u, y = _conv_mix_chunk(proj, wconv_ref, j, prev1, prev2)
        keep_tail(u, slice(j * CH, (j + 1) * CH))
        ys.append(y.astype(BF16))
        proj = nxt
        if j == nch - 1:
            acc = _dot(jnp.concatenate(ys, axis=-1), wout_ref[...])
            ys = []
            out = xs[t] + _rms(acc, gpost_ref[...])
            if residue_major_out:
                for r, piece in enumerate(_to_residue_major(slab_ref, out)):
                    y_ref[0, 0, r, t * SUB:(t + 1) * SUB, :] = piece
            else:
                y_ref[0, t * TM:(t + 1) * TM, :] = out
    st_ref[0] = tail_ref[0:2, :]

    if shift_rows:
        _shift_sample_caches(cache_refs, new_ref, out_cache_refs, window_ref, shift_rows)


def _conv_layer_prompt(x, g_pre, w_in, w_conv, w_out, g_post, layer, residue_major_out, shift=None):
    bsz, seq, _ = x.shape
    tiles = 1 if shift is not None else 2
    rows = tiles * TM
    nt = seq // rows
    n_steps = bsz * nt
    scratch = [pltpu.VMEM((8, D_MODEL), F32)]
    if residue_major_out:
        y_spec = _set_spec(D_MODEL, tiles)
        y_shape = jax.ShapeDtypeStruct((bsz, seq // TA, NRES, BAND, D_MODEL), F32)
        scratch.append(pltpu.VMEM((D_MODEL // HEAD_DIM, 2, TM, HEAD_DIM), F32))
    else:
        y_spec = pl.BlockSpec((1, rows, D_MODEL), lambda b, i: (b, i, 0))
        y_shape = jax.ShapeDtypeStruct((bsz, seq, D_MODEL), F32)
    extra_in, extra_in_specs, extra_out, extra_out_specs, shift_rows = [], [], [], [], 0
    if shift is not None:
        caches, new_rows = shift
        n_s = caches[0].shape[0]
        shift_rows = -(-n_s // n_steps)
        assert n_s % shift_rows == 0
        last = n_s // shift_rows - 1
        blk = lambda a: pl.BlockSpec(
            (shift_rows,) + a.shape[1:],
            lambda b, i: (jnp.minimum(b * nt + i, last),) + (0,) * (a.ndim - 1))
        win_sds = jax.ShapeDtypeStruct((n_s, 2 * N_GROUPS * KV_HEADS, BAND, HEAD_DIM), BF16)
        extra_in = list(caches) + [new_rows]
        extra_in_specs = [blk(a) for a in extra_in]
        extra_out = [jax.ShapeDtypeStruct(c.shape, c.dtype) for c in caches] + [win_sds]
        extra_out_specs = [blk(a) for a in extra_out]
    return pl.pallas_call(
        functools.partial(_conv_prompt_kernel, residue_major_out=residue_major_out,
                          shift_rows=shift_rows, tiles=tiles),
        grid=(bsz, nt),
        in_specs=[
            pl.BlockSpec((1, rows, D_MODEL), lambda b, i: (b, i, 0)),
            _const_spec((1, D_MODEL)),
            _layer_spec(w_in, layer),
            _const_spec((3, D_MODEL)),
            _layer_spec(w_out, layer),
            _const_spec((1, D_MODEL)),
        ] + extra_in_specs,
        out_specs=[y_spec, pl.BlockSpec((1, 2, D_MODEL), lambda b, i: (b, 0, 0))] + extra_out_specs,
        out_shape=[y_shape, jax.ShapeDtypeStruct((bsz, 2, D_MODEL), F32)] + extra_out,
        scratch_shapes=scratch,
        compiler_params=_params(("arbitrary", "arbitrary")),
        name="conv_layer_prompt",
    )(x, g_pre, w_in, w_conv, w_out, g_post, *extra_in)


def _conv_sample_kernel(x_ref, p0_ref, p1_ref, gpre_ref, win32_ref, wconv_ref, wout32_ref, gpost_ref,
                        y_ref, u_ref, win_ref, wout_ref):
    win_ref[...] = win32_ref[...].astype(BF16)
    wout_ref[...] = wout32_ref[...].astype(BF16)
    x = x_ref[...]
    h = _rms(x, gpre_ref[...]).astype(BF16)
    def keep_u(u, cs):
        u_ref[:, cs] = u

    acc = _conv_chunks(h, win_ref, wconv_ref, wout_ref,
                       lambda u, cs: p1_ref[:, cs], lambda u, cs: p0_ref[:, cs], keep_u)
    y_ref[...] = x + _rms(acc, gpost_ref[...])


def _conv_layer_sample(x, past0, past1, g_pre, w_in, w_conv, w_out, g_post, layer):
    n = x.shape[0]
    return pl.pallas_call(
        _conv_sample_kernel,
        grid=(1,),
        in_specs=[_whole_spec(x), _whole_spec(past0), _whole_spec(past1), _whole_spec(g_pre),
                  _layer_spec(w_in, layer), _whole_spec(w_conv), _layer_spec(w_out, layer),
                  _whole_spec(g_post)],
        out_shape=[jax.ShapeDtypeStruct((n, D_MODEL), F32)] * 2
        + [jax.ShapeDtypeStruct(w_in.shape[-2:], BF16), jax.ShapeDtypeStruct(w_out.shape[-2:], BF16)],
        compiler_params=_params(),
        name="conv_layer_sample",
    )(x, past0, past1, g_pre, w_in, w_conv, w_out, g_post)


def _kv_prompt_kernel(x_ref, g_ref, w_ref, cos_ref, sin_ref, kv0_ref, kv1_ref, kv2_ref,
                      c0_ref, c1_ref, c2_ref, slab_ref, *, first_cache_step, tiles):
    kv_refs = (kv0_ref, kv1_ref, kv2_ref)
    fulls = []
    for t in range(tiles):
        sub = slice(t * SUB, (t + 1) * SUB)
        h = _rms(_load_set(x_ref, t), g_ref[...]).astype(BF16)
        cos = jnp.concatenate([cos_ref[0, r, sub, :] for r in range(NRES)], axis=0)
        sin = jnp.concatenate([sin_ref[0, r, sub, :] for r in range(NRES)], axis=0)
        kv = _dot(h, w_ref[...])
        per_group = []
        for g in range(N_GROUPS):
            ks, vs = [], []
            for hh in range(KV_HEADS):
                c = (g * KV_HEADS + hh) * HEAD_DIM
                k = _rope(kv[:, c:c + HEAD_DIM], cos, sin)
                v = kv[:, KV_HALF + c:KV_HALF + c + HEAD_DIM]
                both = jnp.concatenate([k, v], axis=-1).astype(GROUP_DTYPES[g])
                for r in range(NRES):
                    kv_refs[g][0, hh, 0, r, sub, :] = both[r * SUB:(r + 1) * SUB]
                ks.append(k)
                vs.append(v)
            per_group.append(jnp.concatenate(ks + vs, axis=-1))
        fulls.append(per_group)

    @pl.when(pl.program_id(1) >= first_cache_step)
    def _():
        nr = 2 * KV_HEADS
        for t in range(tiles):
            for g, c_ref in enumerate((c0_ref, c1_ref, c2_ref)):
                if WINDOWS[g] <= TM and t < tiles - 1:
                    continue
                keep = min(TM, WINDOWS[g])
                base = t * TM * nr if WINDOWS[g] > TM else 0
                nat = _to_token_order(slab_ref, fulls[t][g])
                for c in range(nr):
                    c_ref[0, pl.ds(base + c, keep, stride=nr), :] = (
                        nat[TM - keep:, c * HEAD_DIM:(c + 1) * HEAD_DIM])


def _kv_proj_prompt(x5, g, w_kv, cos5, sin5):
    bsz, nta = x5.shape[:2]
    seq = nta * TA
    tiles = 2
    rows_step = tiles * TM
    nt = seq // rows_step
    per = TA // rows_step
    nr = 2 * KV_HEADS
    cache_specs, cache_shapes, firsts = [], [], []
    for grp in range(N_GROUPS):
        w = min(WINDOWS[grp], seq)
        rows = min(rows_step, w) if w > TM else w
        first = nt - max(w // rows_step, 1)
        firsts.append(first)
        cache_specs.append(pl.BlockSpec(
            (1, rows * nr, HEAD_DIM), lambda b, i, first=first: (b, jnp.maximum(i - first, 0), 0)))
        cache_shapes.append(jax.ShapeDtypeStruct((bsz, w * nr, HEAD_DIM), F32))
    kv_spec = pl.BlockSpec((1, KV_HEADS, 1, NRES, tiles * SUB, KW),
                           lambda b, i: (b, 0, i // per, 0, i % per, 0))
    kv_shapes = [jax.ShapeDtypeStruct((bsz, KV_HEADS, nta, NRES, BAND, KW), dt)
                 for dt in GROUP_DTYPES]
    tab_spec = pl.BlockSpec((1, NRES, tiles * SUB, HEAD_DIM), lambda b, i: (i // per, 0, i % per, 0))
    outs = pl.pallas_call(
        functools.partial(_kv_prompt_kernel, first_cache_step=min(firsts), tiles=tiles),
        grid=(bsz, nt),
        in_specs=[_set_spec(D_MODEL, tiles), _const_spec((1, D_MODEL)),
                  _const_spec((D_MODEL, 2 * KV_HALF)), tab_spec, tab_spec],
        out_specs=[kv_spec] * N_GROUPS + cache_specs,
        out_shape=kv_shapes + cache_shapes,
        scratch_shapes=[pltpu.VMEM((4, 2, TM, HEAD_DIM), F32)],
        compiler_params=_params(("arbitrary", "arbitrary")),
        name="kv_proj_prompt",
    )(x5, g, w_kv, cos5, sin5)
    kvs = [o.reshape(bsz, KV_HEADS, seq, KW) for o in outs[:N_GROUPS]]
    caches = [c.reshape(bsz, c.shape[1] // nr, 2, KV_HEADS, HEAD_DIM) for c in outs[N_GROUPS:]]
    return kvs, caches


def _q_project(x, g_ref, w_ref, cos_ref, sin_ref, q_refs, z_ref, rows=slice(0, TM)):
    h = _rms(x, g_ref[...]).astype(BF16)
    cos = cos_ref[rows, :]
    sin = sin_ref[rows, :]
    units = [(g, kh) for g in range(N_GROUPS) for kh in range(KV_HEADS)]
    project = lambda u: _dot(h, w_ref[:, (u[0] * KV_HEADS + u[1]) * QW:(u[0] * KV_HEADS + u[1] + 1) * QW])
    cur = project(units[0])
    for n, (g, kh) in enumerate(units):
        nxt = project(units[n + 1]) if n + 1 < len(units) else _dot(h, w_ref[:, Q_WIDTH:])
        heads = [_rope(cur[:, j * HEAD_DIM:(j + 1) * HEAD_DIM], cos, sin) for j in range(Q_PER_KV)]
        q_refs[g][0, kh, rows, :] = jnp.concatenate(heads, axis=-1).astype(GROUP_DTYPES[g])
        cur = nxt
    z_ref[0, rows, :] = cur.astype(BF16)


def _q_prompt_kernel(x_ref, g_ref, w_ref, cos_ref, sin_ref, q0_ref, q1_ref, q2_ref, z_ref, *, tiles):
    for t in range(tiles):
        rows = slice(t * TM, (t + 1) * TM)
        _q_project(x_ref[0, rows, :], g_ref, w_ref, cos_ref, sin_ref, (q0_ref, q1_ref, q2_ref),
                   z_ref, rows)


def _q_specs(bsz, seq, w_in, layer, tiles=1):
    rows = tiles * TM
    x_spec = pl.BlockSpec((1, rows, D_MODEL), lambda b, i: (b, i, 0))
    tab_spec = pl.BlockSpec((rows, HEAD_DIM), lambda b, i: (i, 0))
    in_specs = [_const_spec((1, D_MODEL)), _layer_spec(w_in, layer), tab_spec, tab_spec]
    out_specs = [pl.BlockSpec((1, KV_HEADS, rows, QW), lambda b, i: (b, 0, i, 0))] * N_GROUPS + [x_spec]
    out_shape = ([jax.ShapeDtypeStruct((bsz, KV_HEADS, seq, QW), dt) for dt in GROUP_DTYPES]
                 + [jax.ShapeDtypeStruct((bsz, seq, D_MODEL), BF16)])
    return x_spec, in_specs, out_specs, out_shape


def _q_proj_prompt(x, g, w_in, layer, cos_q, sin_q):
    bsz, seq, _ = x.shape
    tiles = 2
    x_spec, in_specs, out_specs, out_shape = _q_specs(bsz, seq, w_in, layer, tiles)
    return pl.pallas_call(
        functools.partial(_q_prompt_kernel, tiles=tiles),
        grid=(bsz, seq // (tiles * TM)),
        in_specs=[x_spec] + in_specs,
        out_specs=out_specs,
        out_shape=out_shape,
        compiler_params=_params(("arbitrary", "arbitrary")),
        name="q_proj_prompt",
    )(x, g, w_in, cos_q, sin_q)


def _block_runs(g, blk):
    if g == 2:
        return [(blk * BAND, BAND)]
    if g == 1:
        r4, c = blk
        return [((4 * mm + r4) * BAND + 32 * c, 32) for mm in range(4)]
    return [(r * BAND + 8 * blk, 8) for r in range(NRES)]


def _block_pos(g):
    a = jnp.arange(BAND)
    if g == 2:
        return a
    if g == 1:
        return 4 * (a % 32) + a // 32
    return NRES * (a % 8) + a // 8


def _band_bias():
    out = []
    for g in range(N_GROUPS):
        pos = _block_pos(g)
        qpos = jnp.tile(pos, Q_PER_KV)[:, None]
        ok_prev = pos[None, :] >= qpos
        ok_cur = pos[None, :] <= qpos
        neg = jnp.full(ok_cur.shape, NEG, F32)
        cur = jnp.where(ok_cur, 0.0, neg)
        out.append(jnp.concatenate([jnp.where(ok_prev, 0.0, neg), cur], axis=1))
        out.append(jnp.concatenate([neg, cur], axis=1))
    return jnp.stack(out)


def _attn_prompt_kernel(q0_ref, q1_ref, q2_ref, kv0_ref, kv1_ref, kv2_ref, bias_ref, o_ref,
                        acc_ref, m_ref, l_ref, p0_ref, p1_ref, p2_ref):
    q_refs = (q0_ref, q1_ref, q2_ref)
    kv_refs = (kv0_ref, kv1_ref, kv2_ref)
    prev_refs = (p0_ref, p1_ref, p2_ref)
    prev_rows = (8, 32, BAND)
    first = pl.program_id(2) == 0

    @pl.when(first)
    def _():
        for p_ref in prev_refs:
            p_ref[...] = jnp.zeros_like(p_ref)

    no_prev = first.astype(jnp.int32)
    ones = jnp.ones((2 * BAND, HEAD_DIM), BF16)
    m_rows = Q_PER_KV * BAND

    def gather(read, runs):
        return jnp.concatenate([read(rs, n) for rs, n in runs], axis=0)

    def heads_state(ref, runs):
        return jnp.concatenate(
            [gather(lambda rs, n, j=j: ref[j, rs:rs + n, :], runs) for j in range(Q_PER_KV)], axis=0)

    def scatter_state(ref, val, runs):
        for j in range(Q_PER_KV):
            off = j * BAND
            for rs, n in runs:
                ref[j, rs:rs + n, :] = val[off:off + n]
                off += n

    for g in range(N_GROUPS):
        q_ref, kv_ref, p_ref = q_refs[g], kv_refs[g], prev_refs[g]
        if g == 2:
            blocks = [(r, None) for r in range(NRES)]
        elif g == 1:
            blocks = [((r4, c), (r4, c - 1) if c else None) for r4 in range(4) for c in range(4)]
        else:
            blocks = [(c, c - 1 if c else None) for c in range(NRES)]
        for blk, prev_blk in blocks:
            runs = _block_runs(g, blk)
            q4 = jnp.concatenate(
                [gather(lambda rs, n, j=j: q_ref[0, 0, rs:rs + n, j * HEAD_DIM:(j + 1) * HEAD_DIM],
                        runs) for j in range(Q_PER_KV)], axis=0).astype(BF16)
            if prev_blk is None:
                pr = prev_rows[g]
                if g == 2:
                    pruns = [(blk * BAND, BAND)]
                elif g == 1:
                    pruns = [((4 * mm + blk[0]) * pr, pr) for mm in range(4)]
                else:
                    pruns = [(r * pr, pr) for r in range(NRES)]
                read_prev = lambda cols: gather(lambda rs, n: p_ref[rs:rs + n, cols], pruns)
                bias = bias_ref[2 * g + no_prev]
            else:
                pruns = _block_runs(g, prev_blk)
                read_prev = lambda cols: gather(lambda rs, n: kv_ref[0, 0, rs:rs + n, cols], pruns)
                bias = bias_ref[2 * g]
            read_cur = lambda cols: gather(lambda rs, n: kv_ref[0, 0, rs:rs + n, cols], runs)
            kcols, vcols = slice(0, HEAD_DIM), slice(HEAD_DIM, KW)
            k_cat = jnp.concatenate([read_prev(kcols), read_cur(kcols)], axis=0).astype(BF16)
            v_cat = jnp.concatenate([read_prev(vcols), read_cur(vcols)], axis=0).astype(BF16)

            s = _dot_nt(q4, k_cat) + bias
            m_new = jnp.broadcast_to(jnp.max(s, axis=-1, keepdims=True), (m_rows, HEAD_DIM))
            if g > 0:
                m_old = heads_state(m_ref, runs)
                m_new = jnp.maximum(m_old, m_new)
                alpha = jnp.exp2(m_old - m_new)
            p = jnp.exp2(jnp.concatenate([s[:, :BAND] - m_new, s[:, BAND:] - m_new], axis=1))
            ov = _dot(p.astype(BF16), jnp.concatenate([v_cat, ones], axis=1))
            acc_new, l_new = ov[:, :HEAD_DIM], ov[:, HEAD_DIM:]
            if g > 0:
                acc_new = alpha * heads_state(acc_ref, runs) + acc_new
                l_new = alpha * heads_state(l_ref, runs) + l_new
            if g < N_GROUPS - 1:
                scatter_state(acc_ref, acc_new, runs)
                scatter_state(m_ref, m_new, runs)
                scatter_state(l_ref, l_new, runs)
            else:
                o = (acc_new / l_new).astype(BF16)
                (rs, n), = runs
                for j in range(Q_PER_KV):
                    o_ref[0, rs:rs + n, j * HEAD_DIM:(j + 1) * HEAD_DIM] = o[j * BAND:(j + 1) * BAND]

    for g in range(N_GROUPS):
        pr = prev_rows[g]
        for r in range(NRES):
            prev_refs[g][r * pr:(r + 1) * pr, :] = kv_refs[g][0, 0, (r + 1) * BAND - pr:(r + 1) * BAND, :]


def _attn_prompt(qs, kvs, bias):
    bsz, _, seq, _ = qs[0].shape
    blk = lambda width: pl.BlockSpec((1, 1, TA, width), lambda b, h, i: (b, h, i, 0))
    return pl.pallas_call(
        _attn_prompt_kernel,
        grid=(bsz, KV_HEADS, seq // TA),
        in_specs=[blk(QW)] * N_GROUPS + [blk(KW)] * N_GROUPS + [_const_spec(bias.shape)],
        out_specs=pl.BlockSpec((1, TA, QW), lambda b, h, i: (b, i, h)),
        out_shape=jax.ShapeDtypeStruct((bsz, seq, KV_HEADS * QW), BF16),
        scratch_shapes=[
            pltpu.VMEM((Q_PER_KV, TA, HEAD_DIM), F32),
            pltpu.VMEM((Q_PER_KV, TA, HEAD_DIM), F32),
            pltpu.VMEM((Q_PER_KV, TA, HEAD_DIM), F32),
            pltpu.VMEM((NRES * 8, KW), GROUP_DTYPES[0]),
            pltpu.VMEM((NRES * 32, KW), GROUP_DTYPES[1]),
            pltpu.VMEM((NRES * BAND, KW), GROUP_DTYPES[2]),
        ],
        compiler_params=_params(("arbitrary", "arbitrary", "arbitrary")),
        name="attn_prompt",
    )(*qs, *kvs, bias)


def _out_prompt_kernel(o_ref, z_ref, x_ref, wout_ref, gpost_ref, y_ref, slab_ref, *, tiles):
    for t in range(tiles):
        gate = _silu(_load_set(z_ref, t).astype(F32))
        y = (_load_set(o_ref, t).astype(F32) * gate).astype(BF16)
        res = _load_set(x_ref, t) + _rms(_dot(y, wout_ref[...]), gpost_ref[...])
        y_ref[0, t * TM:(t + 1) * TM, :] = _to_token_order(slab_ref, res)


def _out_q_prompt_kernel(o_ref, z_ref, x_ref, wout_ref, gpost_ref, gpre_ref, win_ref, cos_ref, sin_ref,
                         y_ref, q0_ref, q1_ref, q2_ref, zn_ref):
    half = TM // 2
    outs = []
    for hf in range(2):
        rows = slice(hf * half, (hf + 1) * half)
        y = (o_ref[0, rows, :].astype(F32) * _silu(z_ref[0, rows, :].astype(F32))).astype(BF16)
        outs.append(_dot(y, wout_ref[...]))
    for hf in range(2):
        rows = slice(hf * half, (hf + 1) * half)
        x_new = x_ref[0, rows, :] + _rms(outs[hf], gpost_ref[...])
        y_ref[0, rows, :] = x_new
        _q_project(x_new, gpre_ref, win_ref, cos_ref, sin_ref, (q0_ref, q1_ref, q2_ref), zn_ref, rows)


def _out_q_proj_prompt(o, z, x, w_out, g_post, g_pre_next, w_in, layer, cos_q, sin_q):
    bsz, seq, _ = x.shape
    x_spec, q_in_specs, q_out_specs, q_out_shape = _q_specs(bsz, seq, w_in, layer + 1)
    return pl.pallas_call(
        _out_q_prompt_kernel,
        grid=(bsz, seq // TM),
        in_specs=[x_spec] * 3 + [_layer_spec(w_out, layer), _const_spec((1, D_MODEL))]
        + q_in_specs,
        out_specs=[x_spec] + q_out_specs,
        out_shape=[jax.ShapeDtypeStruct((bsz, seq, D_MODEL), F32)] + q_out_shape,
        compiler_params=_params(("arbitrary", "arbitrary")),
        name="out_q_proj_prompt",
    )(o, z, x, w_out, g_post, g_pre_next, w_in, cos_q, sin_q)


def _out_proj_prompt(o, z, x, w_out, layer, g_post):
    bsz, seq, _ = x.shape
    view = lambda a: a.reshape(bsz, seq // TA, NRES, BAND, D_MODEL)
    tiles = 2
    return pl.pallas_call(
        functools.partial(_out_prompt_kernel, tiles=tiles),
        grid=(bsz, seq // (tiles * TM)),
        in_specs=[_set_spec(D_MODEL, tiles)] * 3
        + [_layer_spec(w_out, layer), _const_spec((1, D_MODEL))],
        out_specs=pl.BlockSpec((1, tiles * TM, D_MODEL), lambda b, i: (b, i, 0)),
        out_shape=jax.ShapeDtypeStruct((bsz, seq, D_MODEL), F32),
        scratch_shapes=[pltpu.VMEM((D_MODEL // HEAD_DIM, 2, TM, HEAD_DIM), F32)],
        compiler_params=_params(("arbitrary", "arbitrary")),
        name="out_proj_prompt",
    )(view(o), view(z), view(x), w_out, g_post)


def _proj_sample_kernel(x_ref, g_ref, w32_ref, cos_ref, sin_ref, y_ref, w_ref, *, rope_heads):
    w_ref[...] = w32_ref[...].astype(BF16)
    h = _rms(x_ref[...], g_ref[...]).astype(BF16)
    y = _dot(h, w_ref[...])
    cos = cos_ref[...]
    sin = sin_ref[...]
    for hd in range(rope_heads):
        cs = slice(hd * HEAD_DIM, (hd + 1) * HEAD_DIM)
        y_ref[:, cs] = _rope(y[:, cs], cos, sin)
    rest = rope_heads * HEAD_DIM
    y_ref[:, rest:] = y[:, rest:]


def _proj_sample(x, g, w, cos, sin, rope_heads, layer=0):
    n = x.shape[0]
    return pl.pallas_call(
        functools.partial(_proj_sample_kernel, rope_heads=rope_heads),
        grid=(1,),
        in_specs=[_whole_spec(x), _whole_spec(g), _layer_spec(w, layer), _whole_spec(cos),
                  _whole_spec(sin)],
        out_shape=[jax.ShapeDtypeStruct((n, w.shape[-1]), F32),
                   jax.ShapeDtypeStruct(w.shape[-2:], BF16)],
        compiler_params=_params(),
        name=f"proj_sample_{rope_heads}",
    )(x, g, w, cos, sin)


def _attn_sample_scores(qz, kvn, window_kv):
    parts = []
    for kh in range(KV_HEADS):
        scores, s_new, vals, v_new = [], [], [], []
        for g in range(N_GROUPS):
            c = (g * HEADS + kh * Q_PER_KV) * HEAD_DIM
            q4 = jnp.concatenate(
                [qz[:, c + j * HEAD_DIM:c + (j + 1) * HEAD_DIM] for j in range(Q_PER_KV)], axis=0)
            q8 = jnp.concatenate([q4, q4], axis=0)
            k_rows, v_rows = window_kv(g, kh)
            kc = (g * KV_HEADS + kh) * HEAD_DIM
            k_n = kvn[:, kc:kc + HEAD_DIM]
            v_n = kvn[:, KV_HALF + kc:KV_HALF + kc + HEAD_DIM]
            scores.append(_dot_nt(q8.astype(BF16), k_rows))
            s_new.append(jnp.sum(q8 * k_n, axis=-1, keepdims=True))
            vals.append(v_rows)
            v_new.append(v_n)
        parts.append((scores, s_new, vals, v_new))
    return parts


def _attn_sample_finish(qz, parts):
    pieces = []
    for scores, s_new, vals, v_new in parts:
        m = functools.reduce(jnp.maximum,
                             [jnp.max(s, axis=-1, keepdims=True) for s in scores] + s_new)
        ps = [jnp.exp(s - m) for s in scores]
        pn = [jnp.exp(s - m) for s in s_new]
        l = sum(jnp.sum(p, axis=-1, keepdims=True) for p in ps) + sum(pn)
        o = sum(_dot(ps[g].astype(BF16), vals[g]) + pn[g] * v_new[g] for g in range(N_GROUPS)) / l
        pieces += [o[j:j + 1, :] for j in range(Q_PER_KV)]
    o_row = jnp.concatenate(pieces, axis=-1)
    return o_row * _silu(qz[:, Q_WIDTH:])


def _attn_sample_kernel(qz_ref, kvn_ref, win_ref, y_ref):
    parts = []
    for p in range(SAMPLE_ROWS):
        def window_kv(g, kh, p=p):
            slot = 2 * (g * KV_HEADS + kh)
            return win_ref[p, slot], win_ref[p, slot + 1]

        parts.append(_attn_sample_scores(qz_ref[p], kvn_ref[p], window_kv))
    for p in range(SAMPLE_ROWS):
        y_ref[p] = _attn_sample_finish(qz_ref[p], parts[p])


def _attn_sample(qz, kv_new, win):
    n = qz.shape[0]
    assert n % SAMPLE_ROWS == 0
    row_spec = lambda a: pl.BlockSpec((SAMPLE_ROWS, 1, a.shape[-1]), lambda b: (b, 0, 0))
    return pl.pallas_call(
        _attn_sample_kernel,
        grid=(n // SAMPLE_ROWS,),
        in_specs=[row_spec(qz), row_spec(kv_new),
                  pl.BlockSpec((SAMPLE_ROWS,) + win.shape[1:], lambda b: (b, 0, 0, 0))],
        out_specs=pl.BlockSpec((SAMPLE_ROWS, 1, D_MODEL), lambda b: (b, 0, 0)),
        out_shape=jax.ShapeDtypeStruct((n, 1, D_MODEL), F32),
        compiler_params=_params(("arbitrary",)),
        name="attn_sample",
    )(qz, kv_new, win)


def _out_sample_kernel(y_ref, x_ref, wout32_ref, gpost_ref, o_ref, wout_ref):
    wout_ref[...] = wout32_ref[...].astype(BF16)
    out = _dot(y_ref[...].astype(BF16), wout_ref[...])
    o_ref[...] = x_ref[...] + _rms(out, gpost_ref[...])


def _out_sample(y, x, w_out, layer, g_post):
    return pl.pallas_call(
        _out_sample_kernel,
        grid=(1,),
        in_specs=[_whole_spec(y), _whole_spec(x), _layer_spec(w_out, layer), _whole_spec(g_post)],
        out_shape=[jax.ShapeDtypeStruct(x.shape, F32),
                   jax.ShapeDtypeStruct(w_out.shape[-2:], BF16)],
        compiler_params=_params(),
        name="out_sample",
    )(y, x, w_out, g_post)


def _inv_freq():
    return ROPE_THETA ** (-jnp.arange(HEAD_DIM // 2, dtype=F32) * 2.0 / HEAD_DIM)


def _full_width(cos, sin):
    return jnp.concatenate([cos, cos], axis=-1), jnp.concatenate([-sin, sin], axis=-1)


def _rope_tables(pos):
    ang = pos.astype(F32)[:, None] * _inv_freq()[None, :]
    return _full_width(jnp.cos(ang), jnp.sin(ang))


def _rope_tables_residue_major(seq):
    f = jnp.tile(_inv_freq(), 2)
    sign = jnp.where(jnp.arange(HEAD_DIM) < HEAD_DIM // 2, -1.0, 1.0).astype(F32)
    base = (jnp.arange(seq // TA)[:, None] * TA + NRES * jnp.arange(BAND)[None, :]).astype(F32)
    a = base[:, None, :, None] * f
    b = jnp.arange(NRES, dtype=F32)[None, :, None, None] * f
    cos_a, sin_a, cos_b, sin_b = jnp.cos(a), jnp.sin(a), jnp.cos(b), jnp.sin(b)
    cos = (cos_a * cos_b - sin_a * sin_b).reshape(seq, HEAD_DIM)
    sin = (sign * (sin_a * cos_b + cos_a * sin_b)).reshape(seq, HEAD_DIM)
    return cos, sin


def kernel(x_prompt, x_sample, state_conv, cache_kv_g0, cache_kv_g1, cache_kv_g2,
           a_norm_pre, a_w_in, a_conv, a_w_out, a_norm_post,
           kv_norm, w_kv, b_norm_pre, b_w_in, b_w_out, b_norm_post):
    bsz, seq, _ = x_prompt.shape
    n_s = x_sample.shape[0]
    n_a = a_w_in.shape[0]
    n_b = b_w_in.shape[0]
    assert x_sample.shape[1] == 1 and seq % TA == 0 and n_a >= 1 and n_b >= 1

    cos_p, sin_p = _rope_tables_residue_major(seq)
    cos_s, sin_s = _rope_tables(PAST_LEN + jnp.arange(1))
    row = lambda v: v.reshape(1, -1)

    hs = x_sample.reshape(n_s, D_MODEL)
    conv_s, conv_args = [], []
    for layer in range(n_a):
        past0 = state_conv[layer, :, 0]
        past1 = state_conv[layer, :, 1]
        hs, u_s, w_in16, w_out16 = _conv_layer_sample(
            hs, past0, past1, row(a_norm_pre[layer]), a_w_in, a_conv[layer], a_w_out,
            row(a_norm_post[layer]), layer)
        conv_s.append(jnp.stack([past1, u_s], axis=1))
        conv_args.append((row(a_norm_pre[layer]), w_in16, a_conv[layer], w_out16,
                          row(a_norm_post[layer]), layer))
    kv_s, w_kv16 = _proj_sample(hs, row(kv_norm), w_kv, cos_s, sin_s, N_GROUPS * KV_HEADS)
    nr = 2 * KV_HEADS
    cache_rows = [c.reshape(n_s, c.shape[1] * nr, HEAD_DIM)
                  for c in (cache_kv_g0, cache_kv_g1, cache_kv_g2)]
    new_rows = jnp.stack([kv_s[:, :KV_HALF].reshape(n_s, N_GROUPS, KV_HEADS, HEAD_DIM),
                          kv_s[:, KV_HALF:].reshape(n_s, N_GROUPS, KV_HEADS, HEAD_DIM)],
                         axis=2).reshape(n_s, N_GROUPS, nr, HEAD_DIM)

    hp = x_prompt
    conv_p = []
    for layer in range(n_a):
        hp, st_p, *extra = _conv_layer_prompt(
            hp, *conv_args[layer], residue_major_out=(layer == n_a - 1),
            shift=(cache_rows, new_rows) if layer == 0 else None)
        conv_p.append(st_p)
        if layer == 0:
            *shifted, win = extra
            kv_new_s = [c.reshape(n_s, WINDOWS[g], 2, KV_HEADS, HEAD_DIM)
                        for g, c in enumerate(shifted)]

    kv_s3 = kv_s.reshape(n_s, 1, -1)
    b_w_in16, b_w_out16 = [], []
    for j in range(n_b):
        qz, w16 = _proj_sample(hs, row(b_norm_pre[j]), b_w_in, cos_s * SCALE, sin_s * SCALE,
                               N_GROUPS * HEADS, layer=j)
        b_w_in16.append(w16)
        y_s = _attn_sample(qz.reshape(n_s, 1, -1), kv_s3, win)
        hs, w16 = _out_sample(y_s.reshape(n_s, D_MODEL), hs, b_w_out, j, row(b_norm_post[j]))
        b_w_out16.append(w16)

    tab5 = lambda t: t.reshape(seq // TA, NRES, BAND, HEAD_DIM)
    kv_p, kv_new_p = _kv_proj_prompt(hp, row(kv_norm), w_kv16, tab5(cos_p), tab5(sin_p))
    hp = hp.reshape(bsz, seq, D_MODEL)
    bias = _band_bias()
    cos_q = cos_p * (SCALE * LOG2E)
    sin_q = sin_p * (SCALE * LOG2E)
    *q, z = _q_proj_prompt(hp, row(b_norm_pre[0]), b_w_in16[0], 0, cos_q, sin_q)
    for j in range(n_b):
        g_post = row(b_norm_post[j])
        o = _attn_prompt(q, kv_p, bias)
        if j + 1 < n_b:
            hp, *q, z = _out_q_proj_prompt(o, z, hp, b_w_out16[j], g_post, row(b_norm_pre[j + 1]),
                                           b_w_in16[j + 1], j, cos_q, sin_q)
        else:
            hp = _out_proj_prompt(o, z, hp, b_w_out16[j], j, g_post)

    return (hp, hs.reshape(n_s, 1, D_MODEL), jnp.stack(conv_p, axis=0),
            kv_new_p[0], kv_new_p[1], kv_new_p[2],
            jnp.stack(conv_s, axis=0), kv_new_s[0], kv_new_s[1], kv_new_s[2])
```

```python
import functools

import jax
import jax.numpy as jnp
from jax import lax
from jax.experimental import pallas as pl
from jax.experimental.pallas import tpu as pltpu

D_MODEL = 1024
HEAD_DIM = 128
N_GROUPS = 3
WINDOWS = (128, 512, 2048)
DILATIONS = (1, 4, 16)
HEADS = 8
KV_HEADS = 2
Q_PER_KV = HEADS // KV_HEADS
Q_WIDTH = N_GROUPS * HEADS * HEAD_DIM
KV_HALF = N_GROUPS * KV_HEADS * HEAD_DIM
PAST_LEN = 8192
ROPE_THETA = 10000.0
EPS = 1e-6
SCALE = HEAD_DIM ** -0.5
NEG = -1e30
LOG2E = 1.4426950408889634

BAND = 128
NRES = 16
TA = NRES * BAND
TM = 512
SUB = TM // NRES
CH = 256
SAMPLE_ROWS = 16
KS = 256
QW = Q_PER_KV * HEAD_DIM
KW = 2 * HEAD_DIM
VMEM_LIMIT = 54 * 1024 * 1024

F32 = jnp.float32
BF16 = jnp.bfloat16
GROUP_DTYPES = (F32, BF16, BF16)


def _dot(a, b):
    return jnp.dot(a, b, preferred_element_type=F32)


def _dot_nt(a, b):
    return lax.dot_general(a, b, (((1,), (1,)), ((), ())), preferred_element_type=F32)


def _rms(x, g):
    ms = jnp.mean(x * x, axis=-1, keepdims=True)
    return x * lax.rsqrt(ms + EPS) * g


def _silu(z):
    return z * jax.nn.sigmoid(z)


def _rope(x, cos, sin_signed):
    return x * cos + pltpu.roll(x, HEAD_DIM // 2, 1) * sin_signed


def _params(sem=None, flags=None):
    return pltpu.CompilerParams(dimension_semantics=sem, vmem_limit_bytes=VMEM_LIMIT, flags=flags)


def _const_spec(shape):
    zeros = (0,) * len(shape)
    return pl.BlockSpec(shape, lambda *_: zeros, pipeline_mode=pl.Buffered(1))


def _layer_spec(w, layer):
    if w.ndim == 2:
        return _const_spec(w.shape)
    return pl.BlockSpec((None,) + w.shape[1:], lambda *_: (layer, 0, 0),
                        pipeline_mode=pl.Buffered(1))


def _whole_spec(a):
    zeros = (0,) * a.ndim
    return pl.BlockSpec(a.shape, lambda *_: zeros)


def _split_residues(slab_ref, val):
    q = TM // 4
    slab_ref[0] = val
    for m in range(4):
        slab_ref[1, m * q:(m + 1) * q, :] = slab_ref[0, pl.ds(m, q, stride=4), :]
    return {4 * m2 + m: slab_ref[1, pl.ds(m * q + m2, SUB, stride=4), :]
            for m in range(4) for m2 in range(4)}


def _merge_residues(slab_ref, pieces):
    q = TM // 4
    for m in range(4):
        for m2 in range(4):
            slab_ref[1, pl.ds(m * q + m2, SUB, stride=4), :] = pieces[4 * m2 + m]
    for m in range(4):
        slab_ref[0, pl.ds(m, q, stride=4), :] = slab_ref[1, m * q:(m + 1) * q, :]
    return slab_ref[0]


def _to_residue_major(slab_ref, val):
    ncol = val.shape[1] // HEAD_DIM
    cols = [_split_residues(slab_ref.at[c], val[:, c * HEAD_DIM:(c + 1) * HEAD_DIM])
            for c in range(ncol)]
    return [jnp.concatenate([cols[c][r] for c in range(ncol)], axis=-1) for r in range(NRES)]


def _to_token_order(slab_ref, val):
    ncol = val.shape[1] // HEAD_DIM
    cols = [_merge_residues(slab_ref.at[c],
                            {r: val[r * SUB:(r + 1) * SUB, c * HEAD_DIM:(c + 1) * HEAD_DIM]
                             for r in range(NRES)})
            for c in range(ncol)]
    return jnp.concatenate(cols, axis=-1)


def _set_spec(width, sets=1):
    per = TA // (sets * TM)
    return pl.BlockSpec((1, 1, NRES, sets * SUB, width), lambda b, i: (b, i // per, 0, i % per, 0))


def _load_set(ref, t=0):
    return jnp.concatenate([ref[0, 0, r, t * SUB:(t + 1) * SUB, :] for r in range(NRES)], axis=0)


def _conv_project_chunk(h, win_ref, j):
    return [_dot(h, win_ref[:, k * D_MODEL + j * CH:k * D_MODEL + (j + 1) * CH]) for k in range(4)]


def _conv_mix_chunk(proj, wconv_ref, j, u_prev1, u_prev2):
    cs = slice(j * CH, (j + 1) * CH)
    b_gate, c_gate, u_in, z = proj
    u = c_gate * u_in
    conv = (u_prev2(u, cs) * wconv_ref[0:1, cs] + u_prev1(u, cs) * wconv_ref[1:2, cs]
            + u * wconv_ref[2:3, cs])
    return u, b_gate * conv * _silu(z)


def _shift_sample_caches(cache_refs, new_ref, out_refs, win_ref, rows):
    nr = 2 * KV_HEADS
    for p in range(rows):
        for g in range(N_GROUPS):
            c_ref, o_ref = cache_refs[g], out_refs[g]
            keep = (WINDOWS[g] - 1) * nr
            o_ref[p, 0:keep, :] = c_ref[p, nr:nr + keep, :]
            o_ref[p, keep:keep + nr, :] = new_ref[p, g]
            stride = nr * DILATIONS[g]
            for kh in range(KV_HEADS):
                slot = 2 * (g * KV_HEADS + kh)
                win_ref[p, slot] = c_ref[p, pl.ds(kh, BAND, stride=stride), :].astype(BF16)
                win_ref[p, slot + 1] = c_ref[p, pl.ds(KV_HEADS + kh, BAND, stride=stride), :].astype(BF16)


def _conv_prompt_kernel(*refs, residue_major_out, shift_rows, tiles):
    x_ref, gpre_ref, win_ref, wconv_ref, wout_ref, gpost_ref = refs[:6]
    refs = refs[6:]
    if shift_rows:
        cache_refs, new_ref, refs = refs[:3], refs[3], refs[4:]
    (y_ref, st_ref), refs = refs[:2], refs[2:]
    if shift_rows:
        out_cache_refs, window_ref, refs = refs[:3], refs[3], refs[4:]
    tail_ref, refs = refs[0], refs[1:]
    if residue_major_out:
        slab_ref, = refs

    @pl.when(pl.program_id(1) == 0)
    def _():
        tail_ref[...] = jnp.zeros_like(tail_ref)

    row = lax.broadcasted_iota(jnp.int32, (TM, CH), 0)

    def prev1(u, cs):
        return jnp.where(row == 0, tail_ref[1:2, cs], pltpu.roll(u, 1, 0))

    def prev2(u, cs):
        return jnp.where(row == 0, tail_ref[0:1, cs],
                         jnp.where(row == 1, tail_ref[1:2, cs], pltpu.roll(u, 2, 0)))

    def keep_tail(u, cs):
        tail_ref[0:2, cs] = u[TM - 2:TM, :]

    nch = D_MODEL // CH
    xs = [x_ref[0, t * TM:(t + 1) * TM, :] for t in range(tiles)]
    hs = [_rms(x, gpre_ref[...]).astype(BF16) for x in xs]
    units = [(t, j) for t in range(tiles) for j in range(nch)]
    proj = _conv_project_chunk(hs[0], win_ref, 0)
    ys = []
    for n, (t, j) in enumerate(units):
        if n + 1 < len(units):
            tn, jn = units[n + 1]
            nxt = _conv_project_chunk(hs[tn], win_ref, jn)
        u, y = _conv_mix_chunk(proj, wconv_ref, j, prev1, prev2)
        keep_tail(u, slice(j * CH, (j + 1) * CH))
        ys.append(y.astype(BF16))
        proj = nxt
        if j == nch - 1:
            acc = _dot(jnp.concatenate(ys, axis=-1), wout_ref[...])
            ys = []
            out = xs[t] + _rms(acc, gpost_ref[...])
            if residue_major_out:
                for r, piece in enumerate(_to_residue_major(slab_ref, out)):
                    y_ref[0, 0, r, t * SUB:(t + 1) * SUB, :] = piece
            else:
                y_ref[0, t * TM:(t + 1) * TM, :] = out
    st_ref[0] = tail_ref[0:2, :]

    if shift_rows:
        _shift_sample_caches(cache_refs, new_ref, out_cache_refs, window_ref, shift_rows)


def _conv_layer_prompt(x, g_pre, w_in, w_conv, w_out, g_post, layer, residue_major_out, shift=None):
    bsz, seq, _ = x.shape
    tiles = 1 if shift is not None else 2
    rows = tiles * TM
    nt = seq // rows
    n_steps = bsz * nt
    scratch = [pltpu.VMEM((8, D_MODEL), F32)]
    if residue_major_out:
        y_spec = _set_spec(D_MODEL, tiles)
        y_shape = jax.ShapeDtypeStruct((bsz, seq // TA, NRES, BAND, D_MODEL), F32)
        scratch.append(pltpu.VMEM((D_MODEL // HEAD_DIM, 2, TM, HEAD_DIM), F32))
    else:
        y_spec = pl.BlockSpec((1, rows, D_MODEL), lambda b, i: (b, i, 0))
        y_shape = jax.ShapeDtypeStruct((bsz, seq, D_MODEL), F32)
    extra_in, extra_in_specs, extra_out, extra_out_specs, shift_rows = [], [], [], [], 0
    if shift is not None:
        caches, new_rows = shift
        n_s = caches[0].shape[0]
        shift_rows = -(-n_s // n_steps)
        assert n_s % shift_rows == 0
        last = n_s // shift_rows - 1
        blk = lambda a: pl.BlockSpec(
            (shift_rows,) + a.shape[1:],
            lambda b, i: (jnp.minimum(b * nt + i, last),) + (0,) * (a.ndim - 1))
        win_sds = jax.ShapeDtypeStruct((n_s, 2 * N_GROUPS * KV_HEADS, BAND, HEAD_DIM), BF16)
        extra_in = list(caches) + [new_rows]
        extra_in_specs = [blk(a) for a in extra_in]
        extra_out = [jax.ShapeDtypeStruct(c.shape, c.dtype) for c in caches] + [win_sds]
        extra_out_specs = [blk(a) for a in extra_out]
    return pl.pallas_call(
        functools.partial(_conv_prompt_kernel, residue_major_out=residue_major_out,
                          shift_rows=shift_rows, tiles=tiles),
        grid=(bsz, nt),
        in_specs=[
            pl.BlockSpec((1, rows, D_MODEL), lambda b, i: (b, i, 0)),
            _const_spec((1, D_MODEL)),
            _layer_spec(w_in, layer),
            _const_spec((3, D_MODEL)),
            _layer_spec(w_out, layer),
            _const_spec((1, D_MODEL)),
        ] + extra_in_specs,
        out_specs=[y_spec, pl.BlockSpec((1, 2, D_MODEL), lambda b, i: (b, 0, 0))] + extra_out_specs,
        out_shape=[y_shape, jax.ShapeDtypeStruct((bsz, 2, D_MODEL), F32)] + extra_out,
        scratch_shapes=scratch,
        compiler_params=_params(("arbitrary", "arbitrary")),
        name="conv_layer_prompt",
    )(x, g_pre, w_in, w_conv, w_out, g_post, *extra_in)


def _weight_step_spec(w, layer, rows):
    if w.ndim == 2:
        return pl.BlockSpec((rows, w.shape[1]), lambda k: (k, 0))
    return pl.BlockSpec((None, rows, w.shape[2]), lambda k: (layer, k, 0))


def _stream_project(x_ref, g_ref, w32_ref, w_ref, h_ref, acc_ref):
    k = pl.program_id(0)

    @pl.when(k == 0)
    def _():
        h = _rms(x_ref[...], g_ref[...]).astype(BF16)
        for c in range(D_MODEL // KS):
            h_ref[c] = h[:, c * KS:(c + 1) * KS]
        acc_ref[...] = jnp.zeros_like(acc_ref)

    w_ref[...] = w32_ref[...].astype(BF16)
    acc_ref[...] += _dot(h_ref[k], w_ref[...])


def _conv_sample_kernel(x_ref, p0_ref, p1_ref, gpre_ref, win32_ref, wconv_ref, wout32_ref, gpost_ref,
                        y_ref, u_ref, win_ref, wout_ref, h_ref, acc_ref):
    _stream_project(x_ref, gpre_ref, win32_ref, win_ref, h_ref, acc_ref)

    @pl.when(pl.program_id(0) == 0)
    def _():
        wout_ref[...] = wout32_ref[...].astype(BF16)

    @pl.when(pl.program_id(0) == pl.num_programs(0) - 1)
    def _():
        ys = []
        for j in range(D_MODEL // CH):
            cs = slice(j * CH, (j + 1) * CH)
            proj = [acc_ref[:, k * D_MODEL + j * CH:k * D_MODEL + (j + 1) * CH] for k in range(4)]
            u, y = _conv_mix_chunk(proj, wconv_ref, j,
                                   lambda u, cs: p1_ref[:, cs], lambda u, cs: p0_ref[:, cs])
            u_ref[:, cs] = u
            ys.append(y.astype(BF16))
        out = _dot(jnp.concatenate(ys, axis=-1), wout_ref[...])
        y_ref[...] = x_ref[...] + _rms(out, gpost_ref[...])


def _conv_layer_sample(x, past0, past1, g_pre, w_in, w_conv, w_out, g_post, layer):
    n = x.shape[0]
    whole = lambda a: pl.BlockSpec(a.shape, lambda k: (0,) * a.ndim)
    wout_spec = (pl.BlockSpec((None,) + w_out.shape[1:], lambda k: (layer, 0, 0)))
    out_shape = [jax.ShapeDtypeStruct((n, D_MODEL), F32)] * 2 + [
        jax.ShapeDtypeStruct(w_in.shape[-2:], BF16), jax.ShapeDtypeStruct(w_out.shape[-2:], BF16)]
    return pl.pallas_call(
        _conv_sample_kernel,
        grid=(D_MODEL // KS,),
        in_specs=[whole(x), whole(past0), whole(past1), whole(g_pre),
                  _weight_step_spec(w_in, layer, KS), whole(w_conv), wout_spec, whole(g_post)],
        out_specs=[whole(out_shape[0]), whole(out_shape[1]),
                   pl.BlockSpec((KS, w_in.shape[-1]), lambda k: (k, 0)), whole(out_shape[3])],
        out_shape=out_shape,
        scratch_shapes=[pltpu.VMEM((D_MODEL // KS, n, KS), BF16),
                        pltpu.VMEM((n, w_in.shape[-1]), F32)],
        compiler_params=_params(("arbitrary",)),
        name="conv_layer_sample",
    )(x, past0, past1, g_pre, w_in, w_conv, w_out, g_post)


def _kv_prompt_kernel(x_ref, g_ref, w_ref, cos_ref, sin_ref, kv0_ref, kv1_ref, kv2_ref,
                      c0_ref, c1_ref, c2_ref, slab_ref, *, first_cache_step, tiles):
    kv_refs = (kv0_ref, kv1_ref, kv2_ref)
    fulls = []
    for t in range(tiles):
        sub = slice(t * SUB, (t + 1) * SUB)
        h = _rms(_load_set(x_ref, t), g_ref[...]).astype(BF16)
        cos = jnp.concatenate([cos_ref[0, r, sub, :] for r in range(NRES)], axis=0)
        sin = jnp.concatenate([sin_ref[0, r, sub, :] for r in range(NRES)], axis=0)
        kv = _dot(h, w_ref[...])
        per_group = []
        for g in range(N_GROUPS):
            ks, vs = [], []
            for hh in range(KV_HEADS):
                c = (g * KV_HEADS + hh) * HEAD_DIM
                k = _rope(kv[:, c:c + HEAD_DIM], cos, sin)
                v = kv[:, KV_HALF + c:KV_HALF + c + HEAD_DIM]
                both = jnp.concatenate([k, v], axis=-1).astype(GROUP_DTYPES[g])
                for r in range(NRES):
                    kv_refs[g][0, hh, 0, r, sub, :] = both[r * SUB:(r + 1) * SUB]
                ks.append(k)
                vs.append(v)
            per_group.append(jnp.concatenate(ks + vs, axis=-1))
        fulls.append(per_group)

    @pl.when(pl.program_id(1) >= first_cache_step)
    def _():
        nr = 2 * KV_HEADS
        for t in range(tiles):
            for g, c_ref in enumerate((c0_ref, c1_ref, c2_ref)):
                if WINDOWS[g] <= TM and t < tiles - 1:
                    continue
                keep = min(TM, WINDOWS[g])
                base = t * TM * nr if WINDOWS[g] > TM else 0
                nat = _to_token_order(slab_ref, fulls[t][g])
                for c in range(nr):
                    c_ref[0, pl.ds(base + c, keep, stride=nr), :] = (
                        nat[TM - keep:, c * HEAD_DIM:(c + 1) * HEAD_DIM])


def _kv_proj_prompt(x5, g, w_kv, cos5, sin5):
    bsz, nta = x5.shape[:2]
    seq = nta * TA
    tiles = 2
    rows_step = tiles * TM
    nt = seq // rows_step
    per = TA // rows_step
    nr = 2 * KV_HEADS
    cache_specs, cache_shapes, firsts = [], [], []
    for grp in range(N_GROUPS):
        w = min(WINDOWS[grp], seq)
        rows = min(rows_step, w) if w > TM else w
        first = nt - max(w // rows_step, 1)
        firsts.append(first)
        cache_specs.append(pl.BlockSpec(
            (1, rows * nr, HEAD_DIM), lambda b, i, first=first: (b, jnp.maximum(i - first, 0), 0)))
        cache_shapes.append(jax.ShapeDtypeStruct((bsz, w * nr, HEAD_DIM), F32))
    kv_spec = pl.BlockSpec((1, KV_HEADS, 1, NRES, tiles * SUB, KW),
                           lambda b, i: (b, 0, i // per, 0, i % per, 0))
    kv_shapes = [jax.ShapeDtypeStruct((bsz, KV_HEADS, nta, NRES, BAND, KW), dt)
                 for dt in GROUP_DTYPES]
    tab_spec = pl.BlockSpec((1, NRES, tiles * SUB, HEAD_DIM), lambda b, i: (i // per, 0, i % per, 0))
    outs = pl.pallas_call(
        functools.partial(_kv_prompt_kernel, first_cache_step=min(firsts), tiles=tiles),
        grid=(bsz, nt),
        in_specs=[_set_spec(D_MODEL, tiles), _const_spec((1, D_MODEL)),
                  _const_spec((D_MODEL, 2 * KV_HALF)), tab_spec, tab_spec],
        out_specs=[kv_spec] * N_GROUPS + cache_specs,
        out_shape=kv_shapes + cache_shapes,
        scratch_shapes=[pltpu.VMEM((4, 2, TM, HEAD_DIM), F32)],
        compiler_params=_params(("arbitrary", "arbitrary")),
        name="kv_proj_prompt",
    )(x5, g, w_kv, cos5, sin5)
    kvs = [o.reshape(bsz, KV_HEADS, seq, KW) for o in outs[:N_GROUPS]]
    caches = [c.reshape(bsz, c.shape[1] // nr, 2, KV_HEADS, HEAD_DIM) for c in outs[N_GROUPS:]]
    return kvs, caches


def _q_project(x, g_ref, w_ref, cos_ref, sin_ref, q_refs, z_ref, rows=slice(0, TM)):
    h = _rms(x, g_ref[...]).astype(BF16)
    cos = cos_ref[rows, :]
    sin = sin_ref[rows, :]
    units = [(g, kh) for g in range(N_GROUPS) for kh in range(KV_HEADS)]
    project = lambda u: _dot(h, w_ref[:, (u[0] * KV_HEADS + u[1]) * QW:(u[0] * KV_HEADS + u[1] + 1) * QW])
    cur = project(units[0])
    for n, (g, kh) in enumerate(units):
        nxt = project(units[n + 1]) if n + 1 < len(units) else _dot(h, w_ref[:, Q_WIDTH:])
        heads = [_rope(cur[:, j * HEAD_DIM:(j + 1) * HEAD_DIM], cos, sin) for j in range(Q_PER_KV)]
        q_refs[g][0, kh, rows, :] = jnp.concatenate(heads, axis=-1).astype(GROUP_DTYPES[g])
        cur = nxt
    z_ref[0, rows, :] = cur.astype(BF16)


def _q_prompt_kernel(x_ref, g_ref, w_ref, cos_ref, sin_ref, q0_ref, q1_ref, q2_ref, z_ref, *, tiles):
    for t in range(tiles):
        rows = slice(t * TM, (t + 1) * TM)
        _q_project(x_ref[0, rows, :], g_ref, w_ref, cos_ref, sin_ref, (q0_ref, q1_ref, q2_ref),
                   z_ref, rows)


def _q_specs(bsz, seq, w_in, layer, tiles=1):
    rows = tiles * TM
    x_spec = pl.BlockSpec((1, rows, D_MODEL), lambda b, i: (b, i, 0))
    tab_spec = pl.BlockSpec((rows, HEAD_DIM), lambda b, i: (i, 0))
    in_specs = [_const_spec((1, D_MODEL)), _layer_spec(w_in, layer), tab_spec, tab_spec]
    out_specs = [pl.BlockSpec((1, KV_HEADS, rows, QW), lambda b, i: (b, 0, i, 0))] * N_GROUPS + [x_spec]
    out_shape = ([jax.ShapeDtypeStruct((bsz, KV_HEADS, seq, QW), dt) for dt in GROUP_DTYPES]
                 + [jax.ShapeDtypeStruct((bsz, seq, D_MODEL), BF16)])
    return x_spec, in_specs, out_specs, out_shape


def _q_proj_prompt(x, g, w_in, layer, cos_q, sin_q):
    bsz, seq, _ = x.shape
    tiles = 2
    x_spec, in_specs, out_specs, out_shape = _q_specs(bsz, seq, w_in, layer, tiles)
    return pl.pallas_call(
        functools.partial(_q_prompt_kernel, tiles=tiles),
        grid=(bsz, seq // (tiles * TM)),
        in_specs=[x_spec] + in_specs,
        out_specs=out_specs,
        out_shape=out_shape,
        compiler_params=_params(("arbitrary", "arbitrary")),
        name="q_proj_prompt",
    )(x, g, w_in, cos_q, sin_q)


def _block_runs(g, blk):
    if g == 2:
        return [(blk * BAND, BAND)]
    if g == 1:
        r4, c = blk
        return [((4 * mm + r4) * BAND + 32 * c, 32) for mm in range(4)]
    return [(r * BAND + 8 * blk, 8) for r in range(NRES)]


def _block_pos(g):
    a = jnp.arange(BAND)
    if g == 2:
        return a
    if g == 1:
        return 4 * (a % 32) + a // 32
    return NRES * (a % 8) + a // 8


def _band_bias():
    out = []
    for g in range(N_GROUPS):
        pos = _block_pos(g)
        qpos = jnp.tile(pos, Q_PER_KV)[:, None]
        ok_prev = pos[None, :] >= qpos
        ok_cur = pos[None, :] <= qpos
        neg = jnp.full(ok_cur.shape, NEG, F32)
        cur = jnp.where(ok_cur, 0.0, neg)
        out.append(jnp.concatenate([jnp.where(ok_prev, 0.0, neg), cur], axis=1))
        out.append(jnp.concatenate([neg, cur], axis=1))
    return jnp.stack(out)


def _attn_prompt_kernel(q0_ref, q1_ref, q2_ref, kv0_ref, kv1_ref, kv2_ref, bias_ref, o_ref,
                        acc_ref, m_ref, l_ref, p0_ref, p1_ref, p2_ref):
    q_refs = (q0_ref, q1_ref, q2_ref)
    kv_refs = (kv0_ref, kv1_ref, kv2_ref)
    prev_refs = (p0_ref, p1_ref, p2_ref)
    prev_rows = (8, 32, BAND)
    first = pl.program_id(2) == 0

    @pl.when(first)
    def _():
        for p_ref in prev_refs:
            p_ref[...] = jnp.zeros_like(p_ref)

    no_prev = first.astype(jnp.int32)
    ones = jnp.ones((2 * BAND, HEAD_DIM), BF16)
    m_rows = Q_PER_KV * BAND

    def gather(read, runs):
        return jnp.concatenate([read(rs, n) for rs, n in runs], axis=0)

    def heads_state(ref, runs):
        return jnp.concatenate(
            [gather(lambda rs, n, j=j: ref[j, rs:rs + n, :], runs) for j in range(Q_PER_KV)], axis=0)

    def scatter_state(ref, val, runs):
        for j in range(Q_PER_KV):
            off = j * BAND
            for rs, n in runs:
                ref[j, rs:rs + n, :] = val[off:off + n]
                off += n

    for g in range(N_GROUPS):
        q_ref, kv_ref, p_ref = q_refs[g], kv_refs[g], prev_refs[g]
        if g == 2:
            blocks = [(r, None) for r in range(NRES)]
        elif g == 1:
            blocks = [((r4, c), (r4, c - 1) if c else None) for r4 in range(4) for c in range(4)]
        else:
            blocks = [(c, c - 1 if c else None) for c in range(NRES)]
        for blk, prev_blk in blocks:
            runs = _block_runs(g, blk)
            q4 = jnp.concatenate(
                [gather(lambda rs, n, j=j: q_ref[0, 0, rs:rs + n, j * HEAD_DIM:(j + 1) * HEAD_DIM],
                        runs) for j in range(Q_PER_KV)], axis=0).astype(BF16)
            if prev_blk is None:
                pr = prev_rows[g]
                if g == 2:
                    pruns = [(blk * BAND, BAND)]
                elif g == 1:
                    pruns = [((4 * mm + blk[0]) * pr, pr) for mm in range(4)]
                else:
                    pruns = [(r * pr, pr) for r in range(NRES)]
                read_prev = lambda cols: gather(lambda rs, n: p_ref[rs:rs + n, cols], pruns)
                bias = bias_ref[2 * g + no_prev]
            else:
                pruns = _block_runs(g, prev_blk)
                read_prev = lambda cols: gather(lambda rs, n: kv_ref[0, 0, rs:rs + n, cols], pruns)
                bias = bias_ref[2 * g]
            read_cur = lambda cols: gather(lambda rs, n: kv_ref[0, 0, rs:rs + n, cols], runs)
            kcols, vcols = slice(0, HEAD_DIM), slice(HEAD_DIM, KW)
            k_cat = jnp.concatenate([read_prev(kcols), read_cur(kcols)], axis=0).astype(BF16)
            v_cat = jnp.concatenate([read_prev(vcols), read_cur(vcols)], axis=0).astype(BF16)

            s = _dot_nt(q4, k_cat) + bias
            m_new = jnp.broadcast_to(jnp.max(s, axis=-1, keepdims=True), (m_rows, HEAD_DIM))
            if g > 0:
                m_old = heads_state(m_ref, runs)
                m_new = jnp.maximum(m_old, m_new)
                alpha = jnp.exp2(m_old - m_new)
            p = jnp.exp2(jnp.concatenate([s[:, :BAND] - m_new, s[:, BAND:] - m_new], axis=1))
            ov = _dot(p.astype(BF16), jnp.concatenate([v_cat, ones], axis=1))
            acc_new, l_new = ov[:, :HEAD_DIM], ov[:, HEAD_DIM:]
            if g > 0:
                acc_new = alpha * heads_state(acc_ref, runs) + acc_new
                l_new = alpha * heads_state(l_ref, runs) + l_new
            if g < N_GROUPS - 1:
                scatter_state(acc_ref, acc_new, runs)
                scatter_state(m_ref, m_new, runs)
                scatter_state(l_ref, l_new, runs)
            else:
                o = (acc_new / l_new).astype(BF16)
                (rs, n), = runs
                for j in range(Q_PER_KV):
                    o_ref[0, rs:rs + n, j * HEAD_DIM:(j + 1) * HEAD_DIM] = o[j * BAND:(j + 1) * BAND]

    for g in range(N_GROUPS):
        pr = prev_rows[g]
        for r in range(NRES):
            prev_refs[g][r * pr:(r + 1) * pr, :] = kv_refs[g][0, 0, (r + 1) * BAND - pr:(r + 1) * BAND, :]


def _attn_prompt(qs, kvs, bias):
    bsz, _, seq, _ = qs[0].shape
    blk = lambda width: pl.BlockSpec((1, 1, TA, width), lambda b, h, i: (b, h, i, 0))
    return pl.pallas_call(
        _attn_prompt_kernel,
        grid=(bsz, KV_HEADS, seq // TA),
        in_specs=[blk(QW)] * N_GROUPS + [blk(KW)] * N_GROUPS + [_const_spec(bias.shape)],
        out_specs=pl.BlockSpec((1, TA, QW), lambda b, h, i: (b, i, h)),
        out_shape=jax.ShapeDtypeStruct((bsz, seq, KV_HEADS * QW), BF16),
        scratch_shapes=[
            pltpu.VMEM((Q_PER_KV, TA, HEAD_DIM), F32),
            pltpu.VMEM((Q_PER_KV, TA, HEAD_DIM), F32),
            pltpu.VMEM((Q_PER_KV, TA, HEAD_DIM), F32),
            pltpu.VMEM((NRES * 8, KW), GROUP_DTYPES[0]),
            pltpu.VMEM((NRES * 32, KW), GROUP_DTYPES[1]),
            pltpu.VMEM((NRES * BAND, KW), GROUP_DTYPES[2]),
        ],
        compiler_params=_params(("arbitrary", "arbitrary", "arbitrary")),
        name="attn_prompt",
    )(*qs, *kvs, bias)


def _out_prompt_kernel(o_ref, z_ref, x_ref, wout_ref, gpost_ref, y_ref, slab_ref, *, tiles):
    for t in range(tiles):
        gate = _silu(_load_set(z_ref, t).astype(F32))
        y = (_load_set(o_ref, t).astype(F32) * gate).astype(BF16)
        res = _load_set(x_ref, t) + _rms(_dot(y, wout_ref[...]), gpost_ref[...])
        y_ref[0, t * TM:(t + 1) * TM, :] = _to_token_order(slab_ref, res)


def _out_q_prompt_kernel(o_ref, z_ref, x_ref, wout_ref, gpost_ref, gpre_ref, win_ref, cos_ref, sin_ref,
                         y_ref, q0_ref, q1_ref, q2_ref, zn_ref):
    half = TM // 2
    outs = []
    for hf in range(2):
        rows = slice(hf * half, (hf + 1) * half)
        y = (o_ref[0, rows, :].astype(F32) * _silu(z_ref[0, rows, :].astype(F32))).astype(BF16)
        outs.append(_dot(y, wout_ref[...]))
    for hf in range(2):
        rows = slice(hf * half, (hf + 1) * half)
        x_new = x_ref[0, rows, :] + _rms(outs[hf], gpost_ref[...])
        y_ref[0, rows, :] = x_new
        _q_project(x_new, gpre_ref, win_ref, cos_ref, sin_ref, (q0_ref, q1_ref, q2_ref), zn_ref, rows)


def _out_q_proj_prompt(o, z, x, w_out, g_post, g_pre_next, w_in, layer, cos_q, sin_q):
    bsz, seq, _ = x.shape
    x_spec, q_in_specs, q_out_specs, q_out_shape = _q_specs(bsz, seq, w_in, layer + 1)
    return pl.pallas_call(
        _out_q_prompt_kernel,
        grid=(bsz, seq // TM),
        in_specs=[x_spec] * 3 + [_layer_spec(w_out, layer), _const_spec((1, D_MODEL))]
        + q_in_specs,
        out_specs=[x_spec] + q_out_specs,
        out_shape=[jax.ShapeDtypeStruct((bsz, seq, D_MODEL), F32)] + q_out_shape,
        compiler_params=_params(("arbitrary", "arbitrary")),
        name="out_q_proj_prompt",
    )(o, z, x, w_out, g_post, g_pre_next, w_in, cos_q, sin_q)


def _out_proj_prompt(o, z, x, w_out, layer, g_post):
    bsz, seq, _ = x.shape
    view = lambda a: a.reshape(bsz, seq // TA, NRES, BAND, D_MODEL)
    tiles = 2
    return pl.pallas_call(
        functools.partial(_out_prompt_kernel, tiles=tiles),
        grid=(bsz, seq // (tiles * TM)),
        in_specs=[_set_spec(D_MODEL, tiles)] * 3
        + [_layer_spec(w_out, layer), _const_spec((1, D_MODEL))],
        out_specs=pl.BlockSpec((1, tiles * TM, D_MODEL), lambda b, i: (b, i, 0)),
        out_shape=jax.ShapeDtypeStruct((bsz, seq, D_MODEL), F32),
        scratch_shapes=[pltpu.VMEM((D_MODEL // HEAD_DIM, 2, TM, HEAD_DIM), F32)],
        compiler_params=_params(("arbitrary", "arbitrary")),
        name="out_proj_prompt",
    )(view(o), view(z), view(x), w_out, g_post)


def _proj_sample_kernel(x_ref, g_ref, w32_ref, cos_ref, sin_ref, y_ref, w_ref, h_ref, acc_ref, *,
                        rope_heads):
    _stream_project(x_ref, g_ref, w32_ref, w_ref, h_ref, acc_ref)

    @pl.when(pl.program_id(0) == pl.num_programs(0) - 1)
    def _():
        cos = cos_ref[...]
        sin = sin_ref[...]
        for hd in range(rope_heads):
            cs = slice(hd * HEAD_DIM, (hd + 1) * HEAD_DIM)
            y_ref[:, cs] = _rope(acc_ref[:, cs], cos, sin)
        rest = rope_heads * HEAD_DIM
        y_ref[:, rest:] = acc_ref[:, rest:]


def _proj_sample(x, g, w, cos, sin, rope_heads, layer=0):
    n = x.shape[0]
    cols = w.shape[-1]
    whole = lambda a: pl.BlockSpec(a.shape, lambda k: (0,) * a.ndim)
    y_sds = jax.ShapeDtypeStruct((n, cols), F32)
    return pl.pallas_call(
        functools.partial(_proj_sample_kernel, rope_heads=rope_heads),
        grid=(D_MODEL // KS,),
        in_specs=[whole(x), whole(g), _weight_step_spec(w, layer, KS), whole(cos), whole(sin)],
        out_specs=[whole(y_sds), pl.BlockSpec((KS, cols), lambda k: (k, 0))],
        out_shape=[y_sds, jax.ShapeDtypeStruct(w.shape[-2:], BF16)],
        scratch_shapes=[pltpu.VMEM((D_MODEL // KS, n, KS), BF16), pltpu.VMEM((n, cols), F32)],
        compiler_params=_params(("arbitrary",)),
        name=f"proj_sample_{rope_heads}",
    )(x, g, w, cos, sin)


def _attn_sample_scores(qz, kvn, window_kv):
    parts = []
    for kh in range(KV_HEADS):
        scores, s_new, vals, v_new = [], [], [], []
        for g in range(N_GROUPS):
            c = (g * HEADS + kh * Q_PER_KV) * HEAD_DIM
            q4 = jnp.concatenate(
                [qz[:, c + j * HEAD_DIM:c + (j + 1) * HEAD_DIM] for j in range(Q_PER_KV)], axis=0)
            q8 = jnp.concatenate([q4, q4], axis=0)
            k_rows, v_rows = window_kv(g, kh)
            kc = (g * KV_HEADS + kh) * HEAD_DIM
            k_n = kvn[:, kc:kc + HEAD_DIM]
            v_n = kvn[:, KV_HALF + kc:KV_HALF + kc + HEAD_DIM]
            scores.append(_dot_nt(q8.astype(BF16), k_rows))
            s_new.append(jnp.sum(q8 * k_n, axis=-1, keepdims=True))
            vals.append(v_rows)
            v_new.append(v_n)
        parts.append((scores, s_new, vals, v_new))
    return parts


def _attn_sample_finish(qz, parts):
    pieces = []
    for scores, s_new, vals, v_new in parts:
        m = functools.reduce(jnp.maximum,
                             [jnp.max(s, axis=-1, keepdims=True) for s in scores] + s_new)
        ps = [jnp.exp(s - m) for s in scores]
        pn = [jnp.exp(s - m) for s in s_new]
        l = sum(jnp.sum(p, axis=-1, keepdims=True) for p in ps) + sum(pn)
        o = sum(_dot(ps[g].astype(BF16), vals[g]) + pn[g] * v_new[g] for g in range(N_GROUPS)) / l
        pieces += [o[j:j + 1, :] for j in range(Q_PER_KV)]
    o_row = jnp.concatenate(pieces, axis=-1)
    return o_row * _silu(qz[:, Q_WIDTH:])


def _attn_sample_kernel(qz_ref, kvn_ref, win_ref, y_ref):
    parts = []
    for p in range(SAMPLE_ROWS):
        def window_kv(g, kh, p=p):
            slot = 2 * (g * KV_HEADS + kh)
            return win_ref[p, slot], win_ref[p, slot + 1]

        parts.append(_attn_sample_scores(qz_ref[p], kvn_ref[p], window_kv))
    for p in range(SAMPLE_ROWS):
        y_ref[p] = _attn_sample_finish(qz_ref[p], parts[p])


def _attn_sample(qz, kv_new, win):
    n = qz.shape[0]
    assert n % SAMPLE_ROWS == 0
    row_spec = lambda a: pl.BlockSpec((SAMPLE_ROWS, 1, a.shape[-1]), lambda b: (b, 0, 0))
    return pl.pallas_call(
        _attn_sample_kernel,
        grid=(n // SAMPLE_ROWS,),
        in_specs=[row_spec(qz), row_spec(kv_new),
                  pl.BlockSpec((SAMPLE_ROWS,) + win.shape[1:], lambda b: (b, 0, 0, 0))],
        out_specs=pl.BlockSpec((SAMPLE_ROWS, 1, D_MODEL), lambda b: (b, 0, 0)),
        out_shape=jax.ShapeDtypeStruct((n, 1, D_MODEL), F32),
        compiler_params=_params(("arbitrary",)),
        name="attn_sample",
    )(qz, kv_new, win)


def _out_sample_kernel(y_ref, x_ref, wout32_ref, gpost_ref, o_ref, wout_ref):
    wout_ref[...] = wout32_ref[...].astype(BF16)
    out = _dot(y_ref[...].astype(BF16), wout_ref[...])
    o_ref[...] = x_ref[...] + _rms(out, gpost_ref[...])


def _out_sample(y, x, w_out, layer, g_post):
    return pl.pallas_call(
        _out_sample_kernel,
        grid=(1,),
        in_specs=[_whole_spec(y), _whole_spec(x), _layer_spec(w_out, layer), _whole_spec(g_post)],
        out_shape=[jax.ShapeDtypeStruct(x.shape, F32),
                   jax.ShapeDtypeStruct(w_out.shape[-2:], BF16)],
        compiler_params=_params(),
        name="out_sample",
    )(y, x, w_out, g_post)


def _inv_freq():
    return ROPE_THETA ** (-jnp.arange(HEAD_DIM // 2, dtype=F32) * 2.0 / HEAD_DIM)


def _full_width(cos, sin):
    return jnp.concatenate([cos, cos], axis=-1), jnp.concatenate([-sin, sin], axis=-1)


def _rope_tables(pos):
    ang = pos.astype(F32)[:, None] * _inv_freq()[None, :]
    return _full_width(jnp.cos(ang), jnp.sin(ang))


def _rope_tables_residue_major(seq):
    f = jnp.tile(_inv_freq(), 2)
    sign = jnp.where(jnp.arange(HEAD_DIM) < HEAD_DIM // 2, -1.0, 1.0).astype(F32)
    base = (jnp.arange(seq // TA)[:, None] * TA + NRES * jnp.arange(BAND)[None, :]).astype(F32)
    a = base[:, None, :, None] * f
    b = jnp.arange(NRES, dtype=F32)[None, :, None, None] * f
    cos_a, sin_a, cos_b, sin_b = jnp.cos(a), jnp.sin(a), jnp.cos(b), jnp.sin(b)
    cos = (cos_a * cos_b - sin_a * sin_b).reshape(seq, HEAD_DIM)
    sin = (sign * (sin_a * cos_b + cos_a * sin_b)).reshape(seq, HEAD_DIM)
    return cos, sin


def kernel(x_prompt, x_sample, state_conv, cache_kv_g0, cache_kv_g1, cache_kv_g2,
           a_norm_pre, a_w_in, a_conv, a_w_out, a_norm_post,
           kv_norm, w_kv, b_norm_pre, b_w_in, b_w_out, b_norm_post):
    bsz, seq, _ = x_prompt.shape
    n_s = x_sample.shape[0]
    n_a = a_w_in.shape[0]
    n_b = b_w_in.shape[0]
    assert x_sample.shape[1] == 1 and seq % TA == 0 and n_a >= 1 and n_b >= 1

    cos_p, sin_p = _rope_tables_residue_major(seq)
    cos_s, sin_s = _rope_tables(PAST_LEN + jnp.arange(1))
    row = lambda v: v.reshape(1, -1)

    hs = x_sample.reshape(n_s, D_MODEL)
    conv_s, conv_args = [], []
    for layer in range(n_a):
        past0 = state_conv[layer, :, 0]
        past1 = state_conv[layer, :, 1]
        hs, u_s, w_in16, w_out16 = _conv_layer_sample(
            hs, past0, past1, row(a_norm_pre[layer]), a_w_in, a_conv[layer], a_w_out,
            row(a_norm_post[layer]), layer)
        conv_s.append(jnp.stack([past1, u_s], axis=1))
        conv_args.append((row(a_norm_pre[layer]), w_in16, a_conv[layer], w_out16,
                          row(a_norm_post[layer]), layer))
    kv_s, w_kv16 = _proj_sample(hs, row(kv_norm), w_kv, cos_s, sin_s, N_GROUPS * KV_HEADS)
    nr = 2 * KV_HEADS
    cache_rows = [c.reshape(n_s, c.shape[1] * nr, HEAD_DIM)
                  for c in (cache_kv_g0, cache_kv_g1, cache_kv_g2)]
    new_rows = jnp.stack([kv_s[:, :KV_HALF].reshape(n_s, N_GROUPS, KV_HEADS, HEAD_DIM),
                          kv_s[:, KV_HALF:].reshape(n_s, N_GROUPS, KV_HEADS, HEAD_DIM)],
                         axis=2).reshape(n_s, N_GROUPS, nr, HEAD_DIM)

    hp = x_prompt
    conv_p = []
    for layer in range(n_a):
        hp, st_p, *extra = _conv_layer_prompt(
            hp, *conv_args[layer], residue_major_out=(layer == n_a - 1),
            shift=(cache_rows, new_rows) if layer == 0 else None)
        conv_p.append(st_p)
        if layer == 0:
            *shifted, win = extra
            kv_new_s = [c.reshape(n_s, WINDOWS[g], 2, KV_HEADS, HEAD_DIM)
                        for g, c in enumerate(shifted)]

    kv_s3 = kv_s.reshape(n_s, 1, -1)
    b_w_in16, b_w_out16 = [], []
    for j in range(n_b):
        qz, w16 = _proj_sample(hs, row(b_norm_pre[j]), b_w_in, cos_s * SCALE, sin_s * SCALE,
                               N_GROUPS * HEADS, layer=j)
        b_w_in16.append(w16)
        y_s = _attn_sample(qz.reshape(n_s, 1, -1), kv_s3, win)
        hs, w16 = _out_sample(y_s.reshape(n_s, D_MODEL), hs, b_w_out, j, row(b_norm_post[j]))
        b_w_out16.append(w16)

    tab5 = lambda t: t.reshape(seq // TA, NRES, BAND, HEAD_DIM)
    kv_p, kv_new_p = _kv_proj_prompt(hp, row(kv_norm), w_kv16, tab5(cos_p), tab5(sin_p))
    hp = hp.reshape(bsz, seq, D_MODEL)
    bias = _band_bias()
    cos_q = cos_p * (SCALE * LOG2E)
    sin_q = sin_p * (SCALE * LOG2E)
    *q, z = _q_proj_prompt(hp, row(b_norm_pre[0]), b_w_in16[0], 0, cos_q, sin_q)
    for j in range(n_b):
        g_post = row(b_norm_post[j])
        o = _attn_prompt(q, kv_p, bias)
        if j + 1 < n_b:
            hp, *q, z = _out_q_proj_prompt(o, z, hp, b_w_out16[j], g_post, row(b_norm_pre[j + 1]),
                                           b_w_in16[j + 1], j, cos_q, sin_q)
        else:
            hp = _out_proj_prompt(o, z, hp, b_w_out16[j], j, g_post)

    return (hp, hs.reshape(n_s, 1, D_MODEL), jnp.stack(conv_p, axis=0),
            kv_new_p[0], kv_new_p[1], kv_new_p[2],
            jnp.stack(conv_s, axis=0), kv_new_s[0], kv_new_s[1], kv_new_s[2])
```

```python
import functools

import jax
import jax.numpy as jnp
from jax import lax
from jax.experimental import pallas as pl
from jax.experimental.pallas import tpu as pltpu

D_MODEL = 1024
HEAD_DIM = 128
N_GROUPS = 3
WINDOWS = (128, 512, 2048)
DILATIONS = (1, 4, 16)
HEADS = 8
KV_HEADS = 2
Q_PER_KV = HEADS // KV_HEADS
Q_WIDTH = N_GROUPS * HEADS * HEAD_DIM
KV_HALF = N_GROUPS * KV_HEADS * HEAD_DIM
PAST_LEN = 8192
ROPE_THETA = 10000.0
EPS = 1e-6
SCALE = HEAD_DIM ** -0.5
NEG = -1e30
LOG2E = 1.4426950408889634

BAND = 128
NRES = 16
TA = NRES * BAND
TM = 512
SUB = TM // NRES
CH = 256
SAMPLE_ROWS = 16
QW = Q_PER_KV * HEAD_DIM
KW = 2 * HEAD_DIM
VMEM_LIMIT = 54 * 1024 * 1024

F32 = jnp.float32
BF16 = jnp.bfloat16
GROUP_DTYPES = (F32, BF16, BF16)


def _dot(a, b):
    return jnp.dot(a, b, preferred_element_type=F32)


def _dot_nt(a, b):
    return lax.dot_general(a, b, (((1,), (1,)), ((), ())), preferred_element_type=F32)


def _rms(x, g):
    ms = jnp.mean(x * x, axis=-1, keepdims=True)
    return x * lax.rsqrt(ms + EPS) * g


def _silu(z):
    return z * jax.nn.sigmoid(z)


def _rope(x, cos, sin_signed):
    return x * cos + pltpu.roll(x, HEAD_DIM // 2, 1) * sin_signed


def _params(sem=None, flags=None):
    return pltpu.CompilerParams(dimension_semantics=sem, vmem_limit_bytes=VMEM_LIMIT, flags=flags)


def _const_spec(shape):
    zeros = (0,) * len(shape)
    return pl.BlockSpec(shape, lambda *_: zeros, pipeline_mode=pl.Buffered(1))


def _layer_spec(w, layer):
    if w.ndim == 2:
        return _const_spec(w.shape)
    return pl.BlockSpec((None,) + w.shape[1:], lambda *_: (layer, 0, 0),
                        pipeline_mode=pl.Buffered(1))


def _whole_spec(a):
    zeros = (0,) * a.ndim
    return pl.BlockSpec(a.shape, lambda *_: zeros)


def _split_residues(slab_ref, val):
    q = TM // 4
    slab_ref[0] = val
    for m in range(4):
        slab_ref[1, m * q:(m + 1) * q, :] = slab_ref[0, pl.ds(m, q, stride=4), :]
    return {4 * m2 + m: slab_ref[1, pl.ds(m * q + m2, SUB, stride=4), :]
            for m in range(4) for m2 in range(4)}


def _merge_residues(slab_ref, pieces):
    q = TM // 4
    for m in range(4):
        for m2 in range(4):
            slab_ref[1, pl.ds(m * q + m2, SUB, stride=4), :] = pieces[4 * m2 + m]
    for m in range(4):
        slab_ref[0, pl.ds(m, q, stride=4), :] = slab_ref[1, m * q:(m + 1) * q, :]
    return slab_ref[0]


def _to_residue_major(slab_ref, val):
    ncol = val.shape[1] // HEAD_DIM
    cols = [_split_residues(slab_ref.at[c], val[:, c * HEAD_DIM:(c + 1) * HEAD_DIM])
            for c in range(ncol)]
    return [jnp.concatenate([cols[c][r] for c in range(ncol)], axis=-1) for r in range(NRES)]


def _to_token_order(slab_ref, val):
    ncol = val.shape[1] // HEAD_DIM
    cols = [_merge_residues(slab_ref.at[c],
                            {r: val[r * SUB:(r + 1) * SUB, c * HEAD_DIM:(c + 1) * HEAD_DIM]
                             for r in range(NRES)})
            for c in range(ncol)]
    return jnp.concatenate(cols, axis=-1)


def _set_spec(width, sets=1):
    per = TA // (sets * TM)
    return pl.BlockSpec((1, 1, NRES, sets * SUB, width), lambda b, i: (b, i // per, 0, i % per, 0))


def _load_set(ref, t=0):
    return jnp.concatenate([ref[0, 0, r, t * SUB:(t + 1) * SUB, :] for r in range(NRES)], axis=0)


def _conv_project_chunk(h, win_ref, j):
    return [_dot(h, win_ref[:, k * D_MODEL + j * CH:k * D_MODEL + (j + 1) * CH]) for k in range(4)]


def _conv_mix_chunk(proj, wconv_ref, j, u_prev1, u_prev2):
    cs = slice(j * CH, (j + 1) * CH)
    b_gate, c_gate, u_in, z = proj
    u = c_gate * u_in
    conv = (u_prev2(u, cs) * wconv_ref[0:1, cs] + u_prev1(u, cs) * wconv_ref[1:2, cs]
            + u * wconv_ref[2:3, cs])
    return u, b_gate * conv * _silu(z)


def _conv_chunks(h, win_ref, wconv_ref, wout_ref, u_prev1, u_prev2, keep_u):
    nch = D_MODEL // CH
    acc = jnp.zeros((h.shape[0], D_MODEL), F32)
    proj = _conv_project_chunk(h, win_ref, 0)
    for j in range(nch):
        nxt = _conv_project_chunk(h, win_ref, j + 1) if j + 1 < nch else None
        u, y = _conv_mix_chunk(proj, wconv_ref, j, u_prev1, u_prev2)
        keep_u(u, slice(j * CH, (j + 1) * CH))
        acc = acc + _dot(y.astype(BF16), wout_ref[j * CH:(j + 1) * CH, :])
        proj = nxt
    return acc


def _shift_sample_caches(cache_refs, new_ref, out_refs, win_ref, rows):
    nr = 2 * KV_HEADS
    for p in range(rows):
        for g in range(N_GROUPS):
            c_ref, o_ref = cache_refs[g], out_refs[g]
            keep = (WINDOWS[g] - 1) * nr
            o_ref[p, 0:keep, :] = c_ref[p, nr:nr + keep, :]
            o_ref[p, keep:keep + nr, :] = new_ref[p, g]
            stride = nr * DILATIONS[g]
            for kh in range(KV_HEADS):
                slot = 2 * (g * KV_HEADS + kh)
                win_ref[p, slot] = c_ref[p, pl.ds(kh, BAND, stride=stride), :].astype(BF16)
                win_ref[p, slot + 1] = c_ref[p, pl.ds(KV_HEADS + kh, BAND, stride=stride), :].astype(BF16)


def _conv_prompt_kernel(*refs, residue_major_out, shift_rows, tiles):
    x_ref, gpre_ref, win_ref, wconv_ref, wout_ref, gpost_ref = refs[:6]
    refs = refs[6:]
    if shift_rows:
        cache_refs, new_ref, refs = refs[:3], refs[3], refs[4:]
    (y_ref, st_ref), refs = refs[:2], refs[2:]
    if shift_rows:
        out_cache_refs, window_ref, refs = refs[:3], refs[3], refs[4:]
    tail_ref, refs = refs[0], refs[1:]
    if residue_major_out:
        slab_ref, = refs

    @pl.when(pl.program_id(1) == 0)
    def _():
        tail_ref[...] = jnp.zeros_like(tail_ref)

    row = lax.broadcasted_iota(jnp.int32, (TM, CH), 0)

    def prev1(u, cs):
        return jnp.where(row == 0, tail_ref[1:2, cs], pltpu.roll(u, 1, 0))

    def prev2(u, cs):
        return jnp.where(row == 0, tail_ref[0:1, cs],
                         jnp.where(row == 1, tail_ref[1:2, cs], pltpu.roll(u, 2, 0)))

    def keep_tail(u, cs):
        tail_ref[0:2, cs] = u[TM - 2:TM, :]

    nch = D_MODEL // CH
    xs = [x_ref[0, t * TM:(t + 1) * TM, :] for t in range(tiles)]
    hs = [_rms(x, gpre_ref[...]).astype(BF16) for x in xs]
    units = [(t, j) for t in range(tiles) for j in range(nch)]
    proj = _conv_project_chunk(hs[0], win_ref, 0)
    ys = []
    for n, (t, j) in enumerate(units):
        if n + 1 < len(units):
            tn, jn = units[n + 1]
            nxt = _conv_project_chunk(hs[tn], win_ref, jn)
        u, y = _conv_mix_chunk(proj, wconv_ref, j, prev1, prev2)
        keep_tail(u, slice(j * CH, (j + 1) * CH))
        ys.append(y.astype(BF16))
        proj = nxt
        if j == nch - 1:
            acc = _dot(jnp.concatenate(ys, axis=-1), wout_ref[...])
            ys = []
            out = xs[t] + _rms(acc, gpost_ref[...])
            if residue_major_out:
                for r, piece in enumerate(_to_residue_major(slab_ref, out)):
                    y_ref[0, 0, r, t * SUB:(t + 1) * SUB, :] = piece
            else:
                y_ref[0, t * TM:(t + 1) * TM, :] = out
    st_ref[0] = tail_ref[0:2, :]

    if shift_rows:
        _shift_sample_caches(cache_refs, new_ref, out_cache_refs, window_ref, shift_rows)


def _conv_layer_prompt(x, g_pre, w_in, w_conv, w_out, g_post, layer, residue_major_out, shift=None):
    bsz, seq, _ = x.shape
    tiles = 1 if shift is not None else 2
    rows = tiles * TM
    nt = seq // rows
    n_steps = bsz * nt
    scratch = [pltpu.VMEM((8, D_MODEL), F32)]
    if residue_major_out:
        y_spec = _set_spec(D_MODEL, tiles)
        y_shape = jax.ShapeDtypeStruct((bsz, seq // TA, NRES, BAND, D_MODEL), F32)
        scratch.append(pltpu.VMEM((D_MODEL // HEAD_DIM, 2, TM, HEAD_DIM), F32))
    else:
        y_spec = pl.BlockSpec((1, rows, D_MODEL), lambda b, i: (b, i, 0))
        y_shape = jax.ShapeDtypeStruct((bsz, seq, D_MODEL), F32)
    extra_in, extra_in_specs, extra_out, extra_out_specs, shift_rows = [], [], [], [], 0
    if shift is not None:
        caches, new_rows = shift
        n_s = caches[0].shape[0]
        shift_rows = -(-n_s // n_steps)
        assert n_s % shift_rows == 0
        last = n_s // shift_rows - 1
        blk = lambda a: pl.BlockSpec(
            (shift_rows,) + a.shape[1:],
            lambda b, i: (jnp.minimum(b * nt + i, last),) + (0,) * (a.ndim - 1))
        win_sds = jax.ShapeDtypeStruct((n_s, 2 * N_GROUPS * KV_HEADS, BAND, HEAD_DIM), BF16)
        extra_in = list(caches) + [new_rows]
        extra_in_specs = [blk(a) for a in extra_in]
        extra_out = [jax.ShapeDtypeStruct(c.shape, c.dtype) for c in caches] + [win_sds]
        extra_out_specs = [blk(a) for a in extra_out]
    return pl.pallas_call(
        functools.partial(_conv_prompt_kernel, residue_major_out=residue_major_out,
                          shift_rows=shift_rows, tiles=tiles),
        grid=(bsz, nt),
        in_specs=[
            pl.BlockSpec((1, rows, D_MODEL), lambda b, i: (b, i, 0)),
            _const_spec((1, D_MODEL)),
            _layer_spec(w_in, layer),
            _const_spec((3, D_MODEL)),
            _layer_spec(w_out, layer),
            _const_spec((1, D_MODEL)),
        ] + extra_in_specs,
        out_specs=[y_spec, pl.BlockSpec((1, 2, D_MODEL), lambda b, i: (b, 0, 0))] + extra_out_specs,
        out_shape=[y_shape, jax.ShapeDtypeStruct((bsz, 2, D_MODEL), F32)] + extra_out,
        scratch_shapes=scratch,
        compiler_params=_params(("arbitrary", "arbitrary")),
        name="conv_layer_prompt",
    )(x, g_pre, w_in, w_conv, w_out, g_post, *extra_in)


def _conv_sample_kernel(x_ref, p0_ref, p1_ref, gpre_ref, win32_ref, wconv_ref, wout32_ref, gpost_ref,
                        y_ref, u_ref, win_ref, wout_ref):
    win_ref[...] = win32_ref[...].astype(BF16)
    wout_ref[...] = wout32_ref[...].astype(BF16)
    x = x_ref[...]
    h = _rms(x, gpre_ref[...]).astype(BF16)
    def keep_u(u, cs):
        u_ref[:, cs] = u

    acc = _conv_chunks(h, win_ref, wconv_ref, wout_ref,
                       lambda u, cs: p1_ref[:, cs], lambda u, cs: p0_ref[:, cs], keep_u)
    y_ref[...] = x + _rms(acc, gpost_ref[...])


def _conv_layer_sample(x, past0, past1, g_pre, w_in, w_conv, w_out, g_post, layer):
    n = x.shape[0]
    return pl.pallas_call(
        _conv_sample_kernel,
        grid=(1,),
        in_specs=[_whole_spec(x), _whole_spec(past0), _whole_spec(past1), _whole_spec(g_pre),
                  _layer_spec(w_in, layer), _whole_spec(w_conv), _layer_spec(w_out, layer),
                  _whole_spec(g_post)],
        out_shape=[jax.ShapeDtypeStruct((n, D_MODEL), F32)] * 2
        + [jax.ShapeDtypeStruct(w_in.shape[-2:], BF16), jax.ShapeDtypeStruct(w_out.shape[-2:], BF16)],
        compiler_params=_params(),
        name="conv_layer_sample",
    )(x, past0, past1, g_pre, w_in, w_conv, w_out, g_post)


def _kv_prompt_kernel(x_ref, g_ref, w_ref, cos_ref, sin_ref, kv0_ref, kv1_ref, kv2_ref,
                      c0_ref, c1_ref, c2_ref, slab_ref, *, first_cache_step, tiles):
    kv_refs = (kv0_ref, kv1_ref, kv2_ref)
    fulls = []
    for t in range(tiles):
        sub = slice(t * SUB, (t + 1) * SUB)
        h = _rms(_load_set(x_ref, t), g_ref[...]).astype(BF16)
        cos = jnp.concatenate([cos_ref[0, r, sub, :] for r in range(NRES)], axis=0)
        sin = jnp.concatenate([sin_ref[0, r, sub, :] for r in range(NRES)], axis=0)
        kv = _dot(h, w_ref[...])
        per_group = []
        for g in range(N_GROUPS):
            ks, vs = [], []
            for hh in range(KV_HEADS):
                c = (g * KV_HEADS + hh) * HEAD_DIM
                k = _rope(kv[:, c:c + HEAD_DIM], cos, sin)
                v = kv[:, KV_HALF + c:KV_HALF + c + HEAD_DIM]
                both = jnp.concatenate([k, v], axis=-1).astype(GROUP_DTYPES[g])
                for r in range(NRES):
                    kv_refs[g][0, hh, 0, r, sub, :] = both[r * SUB:(r + 1) * SUB]
                ks.append(k)
                vs.append(v)
            per_group.append(jnp.concatenate(ks + vs, axis=-1))
        fulls.append(per_group)

    @pl.when(pl.program_id(1) >= first_cache_step)
    def _():
        nr = 2 * KV_HEADS
        for t in range(tiles):
            for g, c_ref in enumerate((c0_ref, c1_ref, c2_ref)):
                if WINDOWS[g] <= TM and t < tiles - 1:
                    continue
                keep = min(TM, WINDOWS[g])
                base = t * TM * nr if WINDOWS[g] > TM else 0
                nat = _to_token_order(slab_ref, fulls[t][g])
                for c in range(nr):
                    c_ref[0, pl.ds(base + c, keep, stride=nr), :] = (
                        nat[TM - keep:, c * HEAD_DIM:(c + 1) * HEAD_DIM])


def _kv_proj_prompt(x5, g, w_kv, cos5, sin5):
    bsz, nta = x5.shape[:2]
    seq = nta * TA
    tiles = 2
    rows_step = tiles * TM
    nt = seq // rows_step
    per = TA // rows_step
    nr = 2 * KV_HEADS
    cache_specs, cache_shapes, firsts = [], [], []
    for grp in range(N_GROUPS):
        w = min(WINDOWS[grp], seq)
        rows = min(rows_step, w) if w > TM else w
        first = nt - max(w // rows_step, 1)
        firsts.append(first)
        cache_specs.append(pl.BlockSpec(
            (1, rows * nr, HEAD_DIM), lambda b, i, first=first: (b, jnp.maximum(i - first, 0), 0)))
        cache_shapes.append(jax.ShapeDtypeStruct((bsz, w * nr, HEAD_DIM), F32))
    kv_spec = pl.BlockSpec((1, KV_HEADS, 1, NRES, tiles * SUB, KW),
                           lambda b, i: (b, 0, i // per, 0, i % per, 0))
    kv_shapes = [jax.ShapeDtypeStruct((bsz, KV_HEADS, nta, NRES, BAND, KW), dt)
                 for dt in GROUP_DTYPES]
    tab_spec = pl.BlockSpec((1, NRES, tiles * SUB, HEAD_DIM), lambda b, i: (i // per, 0, i % per, 0))
    outs = pl.pallas_call(
        functools.partial(_kv_prompt_kernel, first_cache_step=min(firsts), tiles=tiles),
        grid=(bsz, nt),
        in_specs=[_set_spec(D_MODEL, tiles), _const_spec((1, D_MODEL)),
                  _const_spec((D_MODEL, 2 * KV_HALF)), tab_spec, tab_spec],
        out_specs=[kv_spec] * N_GROUPS + cache_specs,
        out_shape=kv_shapes + cache_shapes,
        scratch_shapes=[pltpu.VMEM((4, 2, TM, HEAD_DIM), F32)],
        compiler_params=_params(("arbitrary", "arbitrary")),
        name="kv_proj_prompt",
    )(x5, g, w_kv, cos5, sin5)
    kvs = [o.reshape(bsz, KV_HEADS, seq, KW) for o in outs[:N_GROUPS]]
    caches = [c.reshape(bsz, c.shape[1] // nr, 2, KV_HEADS, HEAD_DIM) for c in outs[N_GROUPS:]]
    return kvs, caches


def _q_project(x, g_ref, w_ref, cos_ref, sin_ref, q_refs, z_ref, rows=slice(0, TM)):
    h = _rms(x, g_ref[...]).astype(BF16)
    cos = cos_ref[rows, :]
    sin = sin_ref[rows, :]
    units = [(g, kh) for g in range(N_GROUPS) for kh in range(KV_HEADS)]
    project = lambda u: _dot(h, w_ref[:, (u[0] * KV_HEADS + u[1]) * QW:(u[0] * KV_HEADS + u[1] + 1) * QW])
    cur = project(units[0])
    for n, (g, kh) in enumerate(units):
        nxt = project(units[n + 1]) if n + 1 < len(units) else _dot(h, w_ref[:, Q_WIDTH:])
        heads = [_rope(cur[:, j * HEAD_DIM:(j + 1) * HEAD_DIM], cos, sin) for j in range(Q_PER_KV)]
        q_refs[g][0, kh, rows, :] = jnp.concatenate(heads, axis=-1).astype(GROUP_DTYPES[g])
        cur = nxt
    z_ref[0, rows, :] = cur.astype(BF16)


def _q_prompt_kernel(x_ref, g_ref, w_ref, cos_ref, sin_ref, q0_ref, q1_ref, q2_ref, z_ref, *, tiles):
    for t in range(tiles):
        rows = slice(t * TM, (t + 1) * TM)
        _q_project(x_ref[0, rows, :], g_ref, w_ref, cos_ref, sin_ref, (q0_ref, q1_ref, q2_ref),
                   z_ref, rows)


def _q_specs(bsz, seq, w_in, layer, tiles=1):
    rows = tiles * TM
    x_spec = pl.BlockSpec((1, rows, D_MODEL), lambda b, i: (b, i, 0))
    tab_spec = pl.BlockSpec((rows, HEAD_DIM), lambda b, i: (i, 0))
    in_specs = [_const_spec((1, D_MODEL)), _layer_spec(w_in, layer), tab_spec, tab_spec]
    out_specs = [pl.BlockSpec((1, KV_HEADS, rows, QW), lambda b, i: (b, 0, i, 0))] * N_GROUPS + [x_spec]
    out_shape = ([jax.ShapeDtypeStruct((bsz, KV_HEADS, seq, QW), dt) for dt in GROUP_DTYPES]
                 + [jax.ShapeDtypeStruct((bsz, seq, D_MODEL), BF16)])
    return x_spec, in_specs, out_specs, out_shape


def _q_proj_prompt(x, g, w_in, layer, cos_q, sin_q):
    bsz, seq, _ = x.shape
    tiles = 2
    x_spec, in_specs, out_specs, out_shape = _q_specs(bsz, seq, w_in, layer, tiles)
    return pl.pallas_call(
        functools.partial(_q_prompt_kernel, tiles=tiles),
        grid=(bsz, seq // (tiles * TM)),
        in_specs=[x_spec] + in_specs,
        out_specs=out_specs,
        out_shape=out_shape,
        compiler_params=_params(("arbitrary", "arbitrary")),
        name="q_proj_prompt",
    )(x, g, w_in, cos_q, sin_q)


def _block_runs(g, blk):
    if g == 2:
        return [(blk * BAND, BAND)]
    if g == 1:
        r4, c = blk
        return [((4 * mm + r4) * BAND + 32 * c, 32) for mm in range(4)]
    return [(r * BAND + 8 * blk, 8) for r in range(NRES)]


def _block_pos(g):
    a = jnp.arange(BAND)
    if g == 2:
        return a
    if g == 1:
        return 4 * (a % 32) + a // 32
    return NRES * (a % 8) + a // 8


def _band_bias():
    pos = jnp.stack([_block_pos(g) for g in range(N_GROUPS)])
    kpos = jnp.tile(pos, (1, 2))[:, None, None, :]
    qpos = jnp.tile(pos, (1, Q_PER_KV))[:, None, :, None]
    is_prev = (jnp.arange(2 * BAND) < BAND)[None, None, None, :]
    no_prev = jnp.arange(2)[None, :, None, None] == 1
    ok = jnp.where(is_prev, (kpos >= qpos) & ~no_prev, kpos <= qpos)
    return jnp.where(ok, 0.0, NEG).astype(F32).reshape(2 * N_GROUPS, Q_PER_KV * BAND, 2 * BAND)


def _attn_prompt_kernel(q0_ref, q1_ref, q2_ref, kv0_ref, kv1_ref, kv2_ref, bias_ref, o_ref,
                        acc_ref, m_ref, l_ref, p0_ref, p1_ref, p2_ref):
    q_refs = (q0_ref, q1_ref, q2_ref)
    kv_refs = (kv0_ref, kv1_ref, kv2_ref)
    prev_refs = (p0_ref, p1_ref, p2_ref)
    prev_rows = (8, 32, BAND)
    first = pl.program_id(2) == 0

    @pl.when(first)
    def _():
        for p_ref in prev_refs:
            p_ref[...] = jnp.zeros_like(p_ref)

    no_prev = first.astype(jnp.int32)
    ones = jnp.ones((2 * BAND, HEAD_DIM), BF16)
    m_rows = Q_PER_KV * BAND

    def gather(read, runs):
        return jnp.concatenate([read(rs, n) for rs, n in runs], axis=0)

    def heads_state(ref, runs):
        return jnp.concatenate(
            [gather(lambda rs, n, j=j: ref[j, rs:rs + n, :], runs) for j in range(Q_PER_KV)], axis=0)

    def scatter_state(ref, val, runs):
        for j in range(Q_PER_KV):
            off = j * BAND
            for rs, n in runs:
                ref[j, rs:rs + n, :] = val[off:off + n]
                off += n

    for g in range(N_GROUPS):
        q_ref, kv_ref, p_ref = q_refs[g], kv_refs[g], prev_refs[g]
        if g == 2:
            blocks = [(r, None) for r in range(NRES)]
        elif g == 1:
            blocks = [((r4, c), (r4, c - 1) if c else None) for r4 in range(4) for c in range(4)]
        else:
            blocks = [(c, c - 1 if c else None) for c in range(NRES)]
        for blk, prev_blk in blocks:
            runs = _block_runs(g, blk)
            q4 = jnp.concatenate(
                [gather(lambda rs, n, j=j: q_ref[0, 0, rs:rs + n, j * HEAD_DIM:(j + 1) * HEAD_DIM],
                        runs) for j in range(Q_PER_KV)], axis=0).astype(BF16)
            if prev_blk is None:
                pr = prev_rows[g]
                if g == 2:
                    pruns = [(blk * BAND, BAND)]
                elif g == 1:
                    pruns = [((4 * mm + blk[0]) * pr, pr) for mm in range(4)]
                else:
                    pruns = [(r * pr, pr) for r in range(NRES)]
                read_prev = lambda cols: gather(lambda rs, n: p_ref[rs:rs + n, cols], pruns)
                bias = bias_ref[2 * g + no_prev]
            else:
                pruns = _block_runs(g, prev_blk)
                read_prev = lambda cols: gather(lambda rs, n: kv_ref[0, 0, rs:rs + n, cols], pruns)
                bias = bias_ref[2 * g]
            read_cur = lambda cols: gather(lambda rs, n: kv_ref[0, 0, rs:rs + n, cols], runs)
            kcols, vcols = slice(0, HEAD_DIM), slice(HEAD_DIM, KW)
            k_cat = jnp.concatenate([read_prev(kcols), read_cur(kcols)], axis=0).astype(BF16)
            v_cat = jnp.concatenate([read_prev(vcols), read_cur(vcols)], axis=0).astype(BF16)

            s = _dot_nt(q4, k_cat) + bias
            m_new = jnp.broadcast_to(jnp.max(s, axis=-1, keepdims=True), (m_rows, HEAD_DIM))
            if g > 0:
                m_old = heads_state(m_ref, runs)
                m_new = jnp.maximum(m_old, m_new)
                alpha = jnp.exp2(m_old - m_new)
            p = jnp.exp2(jnp.concatenate([s[:, :BAND] - m_new, s[:, BAND:] - m_new], axis=1))
            ov = _dot(p.astype(BF16), jnp.concatenate([v_cat, ones], axis=1))
            acc_new, l_new = ov[:, :HEAD_DIM], ov[:, HEAD_DIM:]
            if g > 0:
                acc_new = alpha * heads_state(acc_ref, runs) + acc_new
                l_new = alpha * heads_state(l_ref, runs) + l_new
            if g < N_GROUPS - 1:
                scatter_state(acc_ref, acc_new, runs)
                scatter_state(m_ref, m_new, runs)
                scatter_state(l_ref, l_new, runs)
            else:
                o = (acc_new / l_new).astype(BF16)
                (rs, n), = runs
                for j in range(Q_PER_KV):
                    o_ref[0, rs:rs + n, j * HEAD_DIM:(j + 1) * HEAD_DIM] = o[j * BAND:(j + 1) * BAND]

    for g in range(N_GROUPS):
        pr = prev_rows[g]
        for r in range(NRES):
            prev_refs[g][r * pr:(r + 1) * pr, :] = kv_refs[g][0, 0, (r + 1) * BAND - pr:(r + 1) * BAND, :]


def _attn_prompt(qs, kvs, bias):
    bsz, _, seq, _ = qs[0].shape
    blk = lambda width: pl.BlockSpec((1, 1, TA, width), lambda b, h, i: (b, h, i, 0))
    return pl.pallas_call(
        _attn_prompt_kernel,
        grid=(bsz, KV_HEADS, seq // TA),
        in_specs=[blk(QW)] * N_GROUPS + [blk(KW)] * N_GROUPS + [_const_spec(bias.shape)],
        out_specs=pl.BlockSpec((1, TA, QW), lambda b, h, i: (b, i, h)),
        out_shape=jax.ShapeDtypeStruct((bsz, seq, KV_HEADS * QW), BF16),
        scratch_shapes=[
            pltpu.VMEM((Q_PER_KV, TA, HEAD_DIM), F32),
            pltpu.VMEM((Q_PER_KV, TA, HEAD_DIM), F32),
            pltpu.VMEM((Q_PER_KV, TA, HEAD_DIM), F32),
            pltpu.VMEM((NRES * 8, KW), GROUP_DTYPES[0]),
            pltpu.VMEM((NRES * 32, KW), GROUP_DTYPES[1]),
            pltpu.VMEM((NRES * BAND, KW), GROUP_DTYPES[2]),
        ],
        compiler_params=_params(("arbitrary", "arbitrary", "arbitrary")),
        name="attn_prompt",
    )(*qs, *kvs, bias)


def _out_prompt_kernel(o_ref, z_ref, x_ref, wout_ref, gpost_ref, y_ref, slab_ref, *, tiles):
    for t in range(tiles):
        gate = _silu(_load_set(z_ref, t).astype(F32))
        y = (_load_set(o_ref, t).astype(F32) * gate).astype(BF16)
        res = _load_set(x_ref, t) + _rms(_dot(y, wout_ref[...]), gpost_ref[...])
        y_ref[0, t * TM:(t + 1) * TM, :] = _to_token_order(slab_ref, res)


def _out_q_prompt_kernel(o_ref, z_ref, x_ref, wout_ref, gpost_ref, gpre_ref, win_ref, cos_ref, sin_ref,
                         y_ref, q0_ref, q1_ref, q2_ref, zn_ref):
    half = TM // 2
    outs = []
    for hf in range(2):
        rows = slice(hf * half, (hf + 1) * half)
        y = (o_ref[0, rows, :].astype(F32) * _silu(z_ref[0, rows, :].astype(F32))).astype(BF16)
        outs.append(_dot(y, wout_ref[...]))
    for hf in range(2):
        rows = slice(hf * half, (hf + 1) * half)
        x_new = x_ref[0, rows, :] + _rms(outs[hf], gpost_ref[...])
        y_ref[0, rows, :] = x_new
        _q_project(x_new, gpre_ref, win_ref, cos_ref, sin_ref, (q0_ref, q1_ref, q2_ref), zn_ref, rows)


def _out_q_proj_prompt(o, z, x, w_out, g_post, g_pre_next, w_in, layer, cos_q, sin_q):
    bsz, seq, _ = x.shape
    x_spec, q_in_specs, q_out_specs, q_out_shape = _q_specs(bsz, seq, w_in, layer + 1)
    return pl.pallas_call(
        _out_q_prompt_kernel,
        grid=(bsz, seq // TM),
        in_specs=[x_spec] * 3 + [_layer_spec(w_out, layer), _const_spec((1, D_MODEL))]
        + q_in_specs,
        out_specs=[x_spec] + q_out_specs,
        out_shape=[jax.ShapeDtypeStruct((bsz, seq, D_MODEL), F32)] + q_out_shape,
        compiler_params=_params(("arbitrary", "arbitrary")),
        name="out_q_proj_prompt",
    )(o, z, x, w_out, g_post, g_pre_next, w_in, cos_q, sin_q)


def _out_proj_prompt(o, z, x, w_out, layer, g_post):
    bsz, seq, _ = x.shape
    view = lambda a: a.reshape(bsz, seq // TA, NRES, BAND, D_MODEL)
    tiles = 2
    return pl.pallas_call(
        functools.partial(_out_prompt_kernel, tiles=tiles),
        grid=(bsz, seq // (tiles * TM)),
        in_specs=[_set_spec(D_MODEL, tiles)] * 3
        + [_layer_spec(w_out, layer), _const_spec((1, D_MODEL))],
        out_specs=pl.BlockSpec((1, tiles * TM, D_MODEL), lambda b, i: (b, i, 0)),
        out_shape=jax.ShapeDtypeStruct((bsz, seq, D_MODEL), F32),
        scratch_shapes=[pltpu.VMEM((D_MODEL // HEAD_DIM, 2, TM, HEAD_DIM), F32)],
        compiler_params=_params(("arbitrary", "arbitrary")),
        name="out_proj_prompt",
    )(view(o), view(z), view(x), w_out, g_post)


def _proj_sample_kernel(x_ref, g_ref, w32_ref, cos_ref, sin_ref, y_ref, w_ref, *, rope_heads):
    w_ref[...] = w32_ref[...].astype(BF16)
    h = _rms(x_ref[...], g_ref[...]).astype(BF16)
    y = _dot(h, w_ref[...])
    cos = cos_ref[...]
    sin = sin_ref[...]
    for hd in range(rope_heads):
        cs = slice(hd * HEAD_DIM, (hd + 1) * HEAD_DIM)
        y_ref[:, cs] = _rope(y[:, cs], cos, sin)
    rest = rope_heads * HEAD_DIM
    y_ref[:, rest:] = y[:, rest:]


def _proj_sample(x, g, w, cos, sin, rope_heads, layer=0):
    n = x.shape[0]
    return pl.pallas_call(
        functools.partial(_proj_sample_kernel, rope_heads=rope_heads),
        grid=(1,),
        in_specs=[_whole_spec(x), _whole_spec(g), _layer_spec(w, layer), _whole_spec(cos),
                  _whole_spec(sin)],
        out_shape=[jax.ShapeDtypeStruct((n, w.shape[-1]), F32),
                   jax.ShapeDtypeStruct(w.shape[-2:], BF16)],
        compiler_params=_params(),
        name=f"proj_sample_{rope_heads}",
    )(x, g, w, cos, sin)


def _attn_sample_scores(qz, kvn, window_kv):
    parts = []
    for kh in range(KV_HEADS):
        scores, s_new, vals, v_new = [], [], [], []
        for g in range(N_GROUPS):
            c = (g * HEADS + kh * Q_PER_KV) * HEAD_DIM
            q4 = jnp.concatenate(
                [qz[:, c + j * HEAD_DIM:c + (j + 1) * HEAD_DIM] for j in range(Q_PER_KV)], axis=0)
            q8 = jnp.concatenate([q4, q4], axis=0)
            k_rows, v_rows = window_kv(g, kh)
            kc = (g * KV_HEADS + kh) * HEAD_DIM
            k_n = kvn[:, kc:kc + HEAD_DIM]
            v_n = kvn[:, KV_HALF + kc:KV_HALF + kc + HEAD_DIM]
            scores.append(_dot_nt(q8.astype(BF16), k_rows))
            s_new.append(jnp.sum(q8 * k_n, axis=-1, keepdims=True))
            vals.append(v_rows)
            v_new.append(v_n)
        parts.append((scores, s_new, vals, v_new))
    return parts


def _attn_sample_finish(qz, parts):
    pieces = []
    for scores, s_new, vals, v_new in parts:
        m = functools.reduce(jnp.maximum,
                             [jnp.max(s, axis=-1, keepdims=True) for s in scores] + s_new)
        ps = [jnp.exp(s - m) for s in scores]
        pn = [jnp.exp(s - m) for s in s_new]
        l = sum(jnp.sum(p, axis=-1, keepdims=True) for p in ps) + sum(pn)
        o = sum(_dot(ps[g].astype(BF16), vals[g]) + pn[g] * v_new[g] for g in range(N_GROUPS)) / l
        pieces += [o[j:j + 1, :] for j in range(Q_PER_KV)]
    o_row = jnp.concatenate(pieces, axis=-1)
    return o_row * _silu(qz[:, Q_WIDTH:])


def _attn_sample_kernel(qz_ref, kvn_ref, win_ref, y_ref):
    parts = []
    for p in range(SAMPLE_ROWS):
        def window_kv(g, kh, p=p):
            slot = 2 * (g * KV_HEADS + kh)
            return win_ref[p, slot], win_ref[p, slot + 1]

        parts.append(_attn_sample_scores(qz_ref[p:p + 1, :], kvn_ref[p:p + 1, :], window_kv))
    for p in range(SAMPLE_ROWS):
        y_ref[p:p + 1, :] = _attn_sample_finish(qz_ref[p:p + 1, :], parts[p])


def _attn_sample(qz, kv_new, win):
    n = qz.shape[0]
    assert n % SAMPLE_ROWS == 0
    row_spec = lambda a: pl.BlockSpec((SAMPLE_ROWS, a.shape[-1]), lambda b: (b, 0))
    return pl.pallas_call(
        _attn_sample_kernel,
        grid=(n // SAMPLE_ROWS,),
        in_specs=[row_spec(qz), row_spec(kv_new),
                  pl.BlockSpec((SAMPLE_ROWS,) + win.shape[1:], lambda b: (b, 0, 0, 0))],
        out_specs=pl.BlockSpec((SAMPLE_ROWS, D_MODEL), lambda b: (b, 0)),
        out_shape=jax.ShapeDtypeStruct((n, D_MODEL), F32),
        compiler_params=_params(("arbitrary",)),
        name="attn_sample",
    )(qz, kv_new, win)


def _out_sample_kernel(y_ref, x_ref, wout32_ref, gpost_ref, o_ref, wout_ref):
    wout_ref[...] = wout32_ref[...].astype(BF16)
    out = _dot(y_ref[...].astype(BF16), wout_ref[...])
    o_ref[...] = x_ref[...] + _rms(out, gpost_ref[...])


def _out_sample(y, x, w_out, layer, g_post):
    return pl.pallas_call(
        _out_sample_kernel,
        grid=(1,),
        in_specs=[_whole_spec(y), _whole_spec(x), _layer_spec(w_out, layer), _whole_spec(g_post)],
        out_shape=[jax.ShapeDtypeStruct(x.shape, F32),
                   jax.ShapeDtypeStruct(w_out.shape[-2:], BF16)],
        compiler_params=_params(),
        name="out_sample",
    )(y, x, w_out, g_post)


def _inv_freq():
    return ROPE_THETA ** (-jnp.arange(HEAD_DIM // 2, dtype=F32) * 2.0 / HEAD_DIM)


def _full_width(cos, sin):
    return jnp.concatenate([cos, cos], axis=-1), jnp.concatenate([-sin, sin], axis=-1)


def _rope_tables(pos):
    ang = pos.astype(F32)[:, None] * _inv_freq()[None, :]
    return _full_width(jnp.cos(ang), jnp.sin(ang))


def _rope_tables_residue_major(seq):
    f = jnp.tile(_inv_freq(), 2)
    sign = jnp.where(jnp.arange(HEAD_DIM) < HEAD_DIM // 2, -1.0, 1.0).astype(F32)
    base = (jnp.arange(seq // TA)[:, None] * TA + NRES * jnp.arange(BAND)[None, :]).astype(F32)
    a = base[:, None, :, None] * f
    b = jnp.arange(NRES, dtype=F32)[None, :, None, None] * f
    cos_a, sin_a, cos_b, sin_b = jnp.cos(a), jnp.sin(a), jnp.cos(b), jnp.sin(b)
    cos = (cos_a * cos_b - sin_a * sin_b).reshape(seq, HEAD_DIM)
    sin = (sign * (sin_a * cos_b + cos_a * sin_b)).reshape(seq, HEAD_DIM)
    return cos, sin


def kernel(x_prompt, x_sample, state_conv, cache_kv_g0, cache_kv_g1, cache_kv_g2,
           a_norm_pre, a_w_in, a_conv, a_w_out, a_norm_post,
           kv_norm, w_kv, b_norm_pre, b_w_in, b_w_out, b_norm_post):
    bsz, seq, _ = x_prompt.shape
    n_s = x_sample.shape[0]
    n_a = a_w_in.shape[0]
    n_b = b_w_in.shape[0]
    assert x_sample.shape[1] == 1 and seq % TA == 0 and n_a >= 1 and n_b >= 1

    cos_p, sin_p = _rope_tables_residue_major(seq)
    cos_s, sin_s = _rope_tables(PAST_LEN + jnp.arange(1))
    row = lambda v: v.reshape(1, -1)

    hs = x_sample.reshape(n_s, D_MODEL)
    conv_s, conv_args = [], []
    for layer in range(n_a):
        past0 = state_conv[layer, :, 0]
        past1 = state_conv[layer, :, 1]
        hs, u_s, w_in16, w_out16 = _conv_layer_sample(
            hs, past0, past1, row(a_norm_pre[layer]), a_w_in, a_conv[layer], a_w_out,
            row(a_norm_post[layer]), layer)
        conv_s.append(jnp.stack([past1, u_s], axis=1))
        conv_args.append((row(a_norm_pre[layer]), w_in16, a_conv[layer], w_out16,
                          row(a_norm_post[layer]), layer))
    kv_s, w_kv16 = _proj_sample(hs, row(kv_norm), w_kv, cos_s, sin_s, N_GROUPS * KV_HEADS)
    nr = 2 * KV_HEADS
    cache_rows = [c.reshape(n_s, c.shape[1] * nr, HEAD_DIM)
                  for c in (cache_kv_g0, cache_kv_g1, cache_kv_g2)]
    new_rows = jnp.stack([kv_s[:, :KV_HALF].reshape(n_s, N_GROUPS, KV_HEADS, HEAD_DIM),
                          kv_s[:, KV_HALF:].reshape(n_s, N_GROUPS, KV_HEADS, HEAD_DIM)],
                         axis=2).reshape(n_s, N_GROUPS, nr, HEAD_DIM)

    hp = x_prompt
    conv_p = []
    for layer in range(n_a):
        hp, st_p, *extra = _conv_layer_prompt(
            hp, *conv_args[layer], residue_major_out=(layer == n_a - 1),
            shift=(cache_rows, new_rows) if layer == 0 else None)
        conv_p.append(st_p)
        if layer == 0:
            *shifted, win = extra
            kv_new_s = [c.reshape(n_s, WINDOWS[g], 2, KV_HEADS, HEAD_DIM)
                        for g, c in enumerate(shifted)]

    b_w_in16, b_w_out16 = [], []
    for j in range(n_b):
        qz, w16 = _proj_sample(hs, row(b_norm_pre[j]), b_w_in, cos_s * SCALE, sin_s * SCALE,
                               N_GROUPS * HEADS, layer=j)
        b_w_in16.append(w16)
        y_s = _attn_sample(qz, kv_s, win)
        hs, w16 = _out_sample(y_s, hs, b_w_out, j, row(b_norm_post[j]))
        b_w_out16.append(w16)

    tab5 = lambda t: t.reshape(seq // TA, NRES, BAND, HEAD_DIM)
    kv_p, kv_new_p = _kv_proj_prompt(hp, row(kv_norm), w_kv16, tab5(cos_p), tab5(sin_p))
    hp = hp.reshape(bsz, seq, D_MODEL)
    bias = _band_bias()
    cos_q = cos_p * (SCALE * LOG2E)
    sin_q = sin_p * (SCALE * LOG2E)
    *q, z = _q_proj_prompt(hp, row(b_norm_pre[0]), b_w_in16[0], 0, cos_q, sin_q)
    for j in range(n_b):
        g_post = row(b_norm_post[j])
        o = _attn_prompt(q, kv_p, bias)
        if j + 1 < n_b:
            hp, *q, z = _out_q_proj_prompt(o, z, hp, b_w_out16[j], g_post, row(b_norm_pre[j + 1]),
                                           b_w_in16[j + 1], j, cos_q, sin_q)
        else:
            hp = _out_proj_prompt(o, z, hp, b_w_out16[j], j, g_post)

    return (hp, hs.reshape(n_s, 1, D_MODEL), jnp.stack(conv_p, axis=0),
            kv_new_p[0], kv_new_p[1], kv_new_p[2],
            jnp.stack(conv_s, axis=0), kv_new_s[0], kv_new_s[1], kv_new_s[2])
```
